```python
import math
import jax
import jax.numpy as jnp
from jax import lax
import numpy as np

D_MODEL = 1024
BATCH = 16
SEQ = 256
DEPTH = 2
DEC_BATCH = 4
DEC_SEQ = 4096
PAST_LEN = 256

GRID_W = 64
N_EVEN = (DEPTH + 1) // 2
N_ODD = DEPTH // 2
D_MIX = D_MODEL
H_A = 4
DK_A = D_MIX // 2 // H_A
DV_A = DK_A
W_A = H_A * DV_A
CHUNK_A = 32
H_B = 4
DH_B = D_MIX // 2 // H_B // 2
DV_B = 2 * DH_B
QK_B = H_B * 2 * DH_B
W_B = H_B * DV_B
ROPE_BASE = 10000.0
Q_BLOCK = 128
H_C = 8
DK_C = D_MIX // H_C
DV_C = DK_C
W_C = H_C * DV_C
CONV_K = 3
CHUNK_C = 64
N_GROUPS = 4
E_PER_GROUP = 8
N_EXPERTS = N_GROUPS * E_PER_GROUP
TOP_K = 2
D_EXPERT = D_MODEL // 2
MOE_BLOCK = 128
EPS = 1e-6
IN_AB = 5 * W_A + 2 * QK_B + W_B
IN_C = 4 * W_C + 4 * H_C

kernel_name = 'hybrid_diffusion_trunk_step'


def rms_norm(x, w):
    xf = x.astype(jnp.float32)
    y = xf * lax.rsqrt(jnp.mean(xf * xf, axis=-1, keepdims=True) + EPS)
    return (y * w.astype(jnp.float32)).astype(x.dtype)


def l2_normalize(x):
    xf = x.astype(jnp.float32)
    return (xf * lax.rsqrt(jnp.sum(xf * xf, axis=-1, keepdims=True) + EPS)).astype(x.dtype)


def rope_1d(x, pos):
    half = x.shape[-1] // 2
    inv_freq = ROPE_BASE ** (-jnp.arange(half, dtype=jnp.float32) / half)
    ang = pos[:, None] * inv_freq[None, :]
    cos = jnp.cos(ang)[:, None, None, :]
    sin = jnp.sin(ang)[:, None, None, :]
    xf = x.astype(jnp.float32)
    x1, x2 = xf[..., :half], xf[..., half:]
    return jnp.concatenate([x1 * cos - x2 * sin, x2 * cos + x1 * sin], axis=-1).astype(x.dtype)


def rope_2d(x):
    t_len = x.shape[1]
    rows = t_len // GRID_W
    row = jnp.repeat(jnp.arange(rows), GRID_W).astype(jnp.float32)
    col = jnp.tile(jnp.arange(GRID_W), rows).astype(jnp.float32)
    r = x.shape[-1] // 2
    return jnp.concatenate([rope_1d(x[..., :r], row), rope_1d(x[..., r:], col)], axis=-1)


def centred_conv(x, w):
    pad = CONV_K // 2
    t_len = x.shape[1]
    xp = jnp.pad(x, ((0, 0), (pad, pad), (0, 0)))
    y = xp[:, 0:t_len] * w[0]
    for j in range(1, CONV_K):
        y = y + xp[:, j:j + t_len] * w[j]
    return y


def hgrn2_chunked(q, k, v, log_f, s0):
    bsz, nh, t_len, dk = q.shape
    dv = v.shape[-1]
    n = t_len // CHUNK_A
    q, k, log_f = (t.reshape(bsz, nh, n, CHUNK_A, dk) for t in (q, k, log_f))
    v = v.reshape(bsz, nh, n, CHUNK_A, dv)
    b = jnp.cumsum(log_f, axis=3)
    b_last = b[..., -1:, :]
    q_in = q * jnp.exp(b)
    a = jnp.einsum('bhncd,bhnsd->bhncs', q_in, k * jnp.exp(-b))
    a = jnp.where(jnp.tril(jnp.ones((CHUNK_A, CHUNK_A), bool)), a, 0.0)
    o_intra = jnp.einsum('bhncs,bhnse->bhnce', a, v)
    u = jnp.einsum('bhncd,bhnce->bhnde', k * jnp.exp(b_last - b), v)
    g = jnp.exp(b_last[..., 0, :])

    def step(s, inp):
        g_n, u_n = inp
        return g_n[..., None] * s + u_n, s

    s_fin, s_prev = lax.scan(step, s0, (jnp.moveaxis(g, 2, 0), jnp.moveaxis(u, 2, 0)))
    o_inter = jnp.einsum('bhncd,nbhde->bhnce', q_in, s_prev)
    return (o_intra + o_inter).reshape(bsz, nh, t_len, dv), s_fin


def gdn_chunked(q, k, v, g, beta, s0):
    bsz, nh, t_len, dk = q.shape
    dv = v.shape[-1]
    n = t_len // CHUNK_C
    q = q.reshape(bsz, nh, n, CHUNK_C, dk)
    k = k.reshape(bsz, nh, n, CHUNK_C, dk)
    v = v.reshape(bsz, nh, n, CHUNK_C, dv)
    g = g.reshape(bsz, nh, n, CHUNK_C)
    beta = beta.reshape(bsz, nh, n, CHUNK_C)
    gc = jnp.cumsum(g, axis=-1)
    incl = jnp.tril(jnp.ones((CHUNK_C, CHUNK_C), bool))
    strict = jnp.tril(jnp.ones((CHUNK_C, CHUNK_C), bool), -1)
    decay = jnp.exp(jnp.where(incl, gc[..., :, None] - gc[..., None, :], -jnp.inf))
    k_beta = k * beta[..., None]
    m = jnp.where(strict, jnp.einsum('bhncd,bhnsd->bhncs', k_beta, k) * decay, 0.0)
    lhs = m + jnp.eye(CHUNK_C, dtype=m.dtype)
    rhs = jnp.concatenate([v * beta[..., None], k_beta * jnp.exp(gc)[..., None]], axis=-1)
    sol = lax.linalg.triangular_solve(lhs, rhs, left_side=True, lower=True, unit_diagonal=True)
    u, w = sol[..., :dv], sol[..., dv:]
    qk = jnp.einsum('bhncd,bhnsd->bhncs', q, k) * decay
    q_dec = q * jnp.exp(gc)[..., None]
    k_dec = k * jnp.exp(gc[..., -1:] - gc)[..., None]
    g_last = jnp.exp(gc[..., -1])

    def step(s, inp):
        u_n, w_n, qk_n, qd_n, kd_n, gl_n = inp
        v_new = u_n - jnp.einsum('bhcd,bhde->bhce', w_n, s)
        o_n = jnp.einsum('bhcd,bhde->bhce', qd_n, s) + jnp.einsum('bhcs,bhse->bhce', qk_n, v_new)
        s = s * gl_n[..., None, None] + jnp.einsum('bhcd,bhce->bhde', kd_n, v_new)
        return s, o_n

    xs = tuple(jnp.moveaxis(t, 2, 0) for t in (u, w, qk, q_dec, k_dec, g_last))
    s_fin, o = lax.scan(step, s0, xs)
    return jnp.moveaxis(o, 0, 2).reshape(bsz, nh, t_len, dv), s_fin


def diff_attention(q, k, v, lam):
    bsz, t_q = q.shape[:2]
    nb = t_q // Q_BLOCK
    qb = jnp.moveaxis(q.reshape(bsz, nb, Q_BLOCK, H_B, 2, DH_B), 1, 0)
    scale = DH_B ** -0.5

    def one_block(qi):
        s = jnp.einsum('bqhmd,bkhmd->bhmqk', qi, k).astype(jnp.float32) * scale
        p = jax.nn.softmax(s, axis=-1)
        a = p[:, :, 0] - lam * p[:, :, 1]
        return jnp.einsum('bhqk,bkhe->bqhe', a.astype(v.dtype), v)

    o = lax.map(one_block, qb)
    return jnp.moveaxis(o, 0, 1).reshape(bsz, t_q, H_B, DV_B)


def hier_moe(h, router_group, router_expert, w_gate, w_up, w_down):
    bsz, t_len, d = h.shape
    n_tok = bsz * t_len
    xf = h.reshape(n_tok, d)
    glog = (xf @ router_group).astype(jnp.float32)
    gprob = jax.nn.softmax(glog, axis=-1)
    gsel = jnp.argmax(glog, axis=-1)
    p_grp = jnp.take_along_axis(gprob, gsel[:, None], axis=1)
    elog = (xf @ router_expert).astype(jnp.float32).reshape(n_tok, N_GROUPS, E_PER_GROUP)
    elog = jnp.take_along_axis(elog, gsel[:, None, None], axis=1)[:, 0]
    top_v, top_i = lax.top_k(elog, TOP_K)
    wts = jax.nn.softmax(top_v, axis=-1) * p_grp
    eidx = gsel[:, None].astype(jnp.int32) * E_PER_GROUP + top_i.astype(jnp.int32)
    n_asg = n_tok * TOP_K
    flat_e = eidx.reshape(-1)
    flat_tok = jnp.repeat(jnp.arange(n_tok, dtype=jnp.int32), TOP_K)
    flat_w = wts.reshape(-1)
    order = jnp.argsort(flat_e)
    e_s, tok_s, w_s = flat_e[order], flat_tok[order], flat_w[order]
    counts = jnp.bincount(flat_e, length=N_EXPERTS).astype(jnp.int32)
    padded = (counts + MOE_BLOCK - 1) // MOE_BLOCK * MOE_BLOCK
    pad_end = jnp.cumsum(padded)
    pad_start = pad_end - padded
    start = jnp.cumsum(counts) - counts
    dest = pad_start[e_s] + jnp.arange(n_asg, dtype=jnp.int32) - start[e_s]
    n_blocks = -(-n_asg // MOE_BLOCK) + N_EXPERTS
    n_rows = n_blocks * MOE_BLOCK
    row_tok = jnp.full((n_rows,), n_tok, jnp.int32).at[dest].set(tok_s)
    x_pad = jnp.concatenate([xf, jnp.zeros((1, d), xf.dtype)], axis=0)
    xb = x_pad[row_tok].reshape(n_blocks, MOE_BLOCK, d)
    blk_start = jnp.arange(n_blocks, dtype=jnp.int32) * MOE_BLOCK
    blk_e = jnp.minimum(jnp.searchsorted(pad_end, blk_start, side='right'), N_EXPERTS - 1)

    def expert_block(args):
        xi, e = args
        return (jax.nn.silu(xi @ w_gate[e]) * (xi @ w_up[e])) @ w_down[e]

    yb = lax.map(expert_block, (xb, blk_e)).reshape(n_rows, d)
    y = jax.ops.segment_sum(yb[dest] * w_s[:, None].astype(yb.dtype), tok_s, num_segments=n_tok)
    return y.reshape(bsz, t_len, d)


def setup_inputs(seed: int = 0) -> dict:
    key = jax.random.key(seed)
    ks = jax.random.split(key, 32)
    f32 = jnp.float32

    def nrm(k, shape, scale):
        return jax.random.normal(k, shape, f32) * scale

    def gain(k, shape):
        return 1.0 + 0.05 * jax.random.normal(k, shape, f32)

    dt = jnp.exp(jax.random.uniform(ks[20], (N_ODD, 2, H_C), f32, math.log(1e-3), math.log(1e-1)))
    return {
        'x_prompt': nrm(ks[0], (BATCH, SEQ, D_MODEL), 1.0),
        'x_sample': nrm(ks[1], (DEC_BATCH, DEC_SEQ, D_MODEL), 1.0),
        'c': nrm(ks[2], (DEC_BATCH, D_MODEL), 1.0),
        'cache_diff_k': nrm(ks[3], (DEC_BATCH, N_EVEN, PAST_LEN, H_B, 2, DH_B), 1.0),
        'cache_diff_v': nrm(ks[4], (DEC_BATCH, N_EVEN, PAST_LEN, H_B, DV_B), 1.0),
        'state_hgrn': nrm(ks[5], (DEC_BATCH, N_EVEN, 2, H_A, DK_A, DV_A), 0.5),
        'state_gdn': nrm(ks[6], (DEC_BATCH, N_ODD, 2, H_C, DK_C, DV_C), 0.3),
        'c_ctx': nrm(ks[7], (D_MODEL,), 1.0),
        'ada_w': nrm(ks[8], (DEPTH, D_MODEL, 6 * D_MODEL), 0.5 * D_MODEL ** -0.5),
        'ada_b': nrm(ks[9], (DEPTH, 6 * D_MODEL), 0.02),
        'norm_w': gain(ks[10], (DEPTH, 2, D_MODEL)),
        'final_norm_w': gain(ks[11], (D_MODEL,)),
        'w_in_ab': nrm(ks[12], (N_EVEN, D_MODEL, IN_AB), D_MODEL ** -0.5),
        'hgrn_lb_logits': nrm(ks[13], (2, DEPTH + 1, W_A), 0.5),
        'hgrn_norm_w': gain(ks[14], (N_EVEN, DV_A)),
        'diff_lambda': nrm(ks[15], (N_EVEN, 4, DH_B), 0.1),
        'diff_norm_w': gain(ks[16], (N_EVEN, DV_B)),
        'w_in_c': nrm(ks[17], (N_ODD, D_MODEL, IN_C), D_MODEL ** -0.5),
        'gdn_conv_w': nrm(ks[18], (N_ODD, CONV_K, 3 * W_C), CONV_K ** -0.5),
        'gdn_a_log': jnp.log(jax.random.uniform(ks[19], (N_ODD, 2, H_C), f32, 1.0, 16.0)),
        'gdn_dt_bias': dt + jnp.log(-jnp.expm1(-dt)),
        'gdn_norm_w': gain(ks[21], (N_ODD, DV_C)),
        'w_out': nrm(ks[22], (DEPTH, D_MIX, D_MODEL), D_MIX ** -0.5),
        'moe_router_group': nrm(ks[23], (DEPTH, D_MODEL, N_GROUPS), D_MODEL ** -0.5),
        'moe_router_expert': nrm(ks[24], (DEPTH, D_MODEL, N_EXPERTS), D_MODEL ** -0.5),
        'moe_w_gate': nrm(ks[25], (DEPTH, N_EXPERTS, D_MODEL, D_EXPERT), D_MODEL ** -0.5),
        'moe_w_up': nrm(ks[26], (DEPTH, N_EXPERTS, D_MODEL, D_EXPERT), D_MODEL ** -0.5),
        'moe_w_down': nrm(ks[27], (DEPTH, N_EXPERTS, D_EXPERT, D_MODEL), D_EXPERT ** -0.5),
    }


def reference(x_prompt, x_sample, c, cache_diff_k, cache_diff_v, state_hgrn, state_gdn, c_ctx,
              ada_w, ada_b, norm_w, final_norm_w, w_in_ab, hgrn_lb_logits, hgrn_norm_w,
              diff_lambda, diff_norm_w, w_in_c, gdn_conv_w, gdn_a_log, gdn_dt_bias, gdn_norm_w,
              w_out, moe_router_group, moe_router_expert, moe_w_gate, moe_w_up, moe_w_down):
    f32 = jnp.float32
    lower_bounds = jnp.cumsum(jax.nn.softmax(hgrn_lb_logits.astype(f32), axis=1), axis=1)

    def to_heads(t, n_heads):
        bsz, t_len = t.shape[:2]
        return t.reshape(bsz, t_len, n_heads, -1).transpose(0, 2, 1, 3).astype(f32)

    def flip(t):
        return jnp.flip(t, axis=2)

    def mixer_ab(h, l, context_pass):
        i = l // 2
        bsz, t_len = h.shape[:2]
        proj = h @ w_in_ab[i]
        q_h, f_fw, f_bw, inp, g_h, q_d, k_d, v_d = jnp.split(
            proj, [W_A, 2 * W_A, 3 * W_A, 4 * W_A, 5 * W_A, 5 * W_A + QK_B, 5 * W_A + 2 * QK_B], axis=-1)
        q_h = to_heads(jax.nn.silu(q_h), H_A)
        v_h = to_heads(inp, H_A)

        def forget(raw, d):
            lb = lower_bounds[d, l]
            f = lb + (1.0 - lb) * jax.nn.sigmoid(raw.astype(f32))
            return to_heads(jnp.log(f), H_A), to_heads(1.0 - f, H_A)

        logf_fw, k_fw = forget(f_fw, 0)
        logf_bw, k_bw = forget(f_bw, 1)
        if context_pass:
            s0 = jnp.zeros((bsz, 2, H_A, DK_A, DV_A), f32)
        else:
            s0 = state_hgrn[:, i].astype(f32)
        o_fw, s_fw = hgrn2_chunked(q_h, k_fw, v_h, logf_fw, s0[:, 0])
        o_bw, s_bw = hgrn2_chunked(flip(q_h), flip(k_bw), flip(v_h), flip(logf_bw), s0[:, 1])
        o_h = (o_fw + flip(o_bw)).transpose(0, 2, 1, 3)
        o_h = rms_norm(o_h, hgrn_norm_w[i]) * jax.nn.silu(g_h.reshape(bsz, t_len, H_A, DV_A).astype(f32))
        o_h = o_h.reshape(bsz, t_len, W_A).astype(h.dtype)
        q_d = q_d.reshape(bsz, t_len, H_B, 2, DH_B)
        k_d = k_d.reshape(bsz, t_len, H_B, 2, DH_B)
        v_d = v_d.reshape(bsz, t_len, H_B, DV_B)
        if context_pass:
            q_all, k_all, v_all = q_d, k_d, v_d
        else:
            q_all = rope_2d(q_d)
            k_all = jnp.concatenate([rope_2d(k_d), cache_diff_k[:, i]], axis=1)
            v_all = jnp.concatenate([v_d, cache_diff_v[:, i]], axis=1)
        lam_init = 0.8 - 0.6 * math.exp(-0.3 * l)
        lq1, lk1, lq2, lk2 = (diff_lambda[i, j].astype(f32) for j in range(4))
        lam = jnp.exp(jnp.sum(lq1 * lk1)) - jnp.exp(jnp.sum(lq2 * lk2)) + lam_init
        o_d = diff_attention(q_all, k_all, v_all, lam)
        o_d = (rms_norm(o_d, diff_norm_w[i]) * (1.0 - lam_init)).reshape(bsz, t_len, W_B)
        y = jnp.concatenate([o_h, o_d.astype(h.dtype)], axis=-1) @ w_out[l]
        new_ctx = (k_d, v_d, jnp.stack([s_fw, s_bw], axis=1)) if context_pass else None
        return y, new_ctx

    def mixer_c(h, l, context_pass):
        i = l // 2
        bsz, t_len = h.shape[:2]
        proj = h @ w_in_c[i]
        qkv, g_o, a_fw, a_bw, b_fw, b_bw = jnp.split(
            proj, [3 * W_C, 4 * W_C, 4 * W_C + H_C, 4 * W_C + 2 * H_C, 4 * W_C + 3 * H_C], axis=-1)
        qkv = jax.nn.silu(centred_conv(qkv, gdn_conv_w[i]))
        q, k, v = jnp.split(qkv, 3, axis=-1)
        q = l2_normalize(to_heads(q, H_C)) * (DK_C ** -0.5)
        k = l2_normalize(to_heads(k, H_C))
        v = to_heads(v, H_C)

        def decay_and_beta(a, b, d):
            g = -jnp.exp(gdn_a_log[i, d].astype(f32)) * jax.nn.softplus(
                a.astype(f32) + gdn_dt_bias[i, d].astype(f32))
            return g.transpose(0, 2, 1), jax.nn.sigmoid(b.astype(f32)).transpose(0, 2, 1)

        g_fw, beta_fw = decay_and_beta(a_fw, b_fw, 0)
        g_bw, beta_bw = decay_and_beta(a_bw, b_bw, 1)
        if context_pass:
            s0 = jnp.zeros((bsz, 2, H_C, DK_C, DV_C), f32)
        else:
            s0 = state_gdn[:, i].astype(f32)
        o_fw, s_fw = gdn_chunked(q, k, v, g_fw, beta_fw, s0[:, 0])
        o_bw, s_bw = gdn_chunked(flip(q), flip(k), flip(v), flip(g_bw), flip(beta_bw), s0[:, 1])
        o = (o_fw + flip(o_bw)).transpose(0, 2, 1, 3)
        o = rms_norm(o, gdn_norm_w[i]) * jax.nn.silu(g_o.reshape(bsz, t_len, H_C, DV_C).astype(f32))
        y = o.reshape(bsz, t_len, W_C).astype(h.dtype) @ w_out[l]
        new_ctx = jnp.stack([s_fw, s_bw], axis=1) if context_pass else None
        return y, new_ctx

    def trunk(x, cvec, context_pass):
        ctx_k, ctx_v, ctx_hgrn, ctx_gdn = [], [], [], []
        for l in range(DEPTH):
            mod = (jax.nn.silu(cvec) @ ada_w[l] + ada_b[l]).reshape(cvec.shape[0], 1, 6, D_MODEL)
            shift1, scale1, gate1, shift2, scale2, gate2 = (mod[:, :, j] for j in range(6))
            h = rms_norm(x, norm_w[l, 0]) * (1.0 + scale1) + shift1
            if l % 2 == 0:
                y, new_ctx = mixer_ab(h, l, context_pass)
                if context_pass:
                    ctx_k.append(new_ctx[0])
                    ctx_v.append(new_ctx[1])
                    ctx_hgrn.append(new_ctx[2])
            else:
                y, new_ctx = mixer_c(h, l, context_pass)
                if context_pass:
                    ctx_gdn.append(new_ctx)
            x = x + gate1 * y
            h = rms_norm(x, norm_w[l, 1]) * (1.0 + scale2) + shift2
            x = x + gate2 * hier_moe(h, moe_router_group[l], moe_router_expert[l],
                                     moe_w_gate[l], moe_w_up[l], moe_w_down[l])
        return rms_norm(x, final_norm_w), ctx_k, ctx_v, ctx_hgrn, ctx_gdn

    y_prompt, ck, cv, ch, cg = trunk(x_prompt, c_ctx[None, :], True)
    y_sample, _, _, _, _ = trunk(x_sample, c, False)
    new_cache_diff_k = jnp.stack(ck, axis=1)
    new_cache_diff_v = jnp.stack(cv, axis=1)
    new_state_hgrn = jnp.stack(ch, axis=1)
    new_state_gdn = jnp.stack(cg, axis=1)
    return (y_prompt, y_sample, new_cache_diff_k, new_cache_diff_v, new_state_hgrn, new_state_gdn)
```

```python
import functools
import math

import jax
import jax.numpy as jnp
from jax import lax
from jax.experimental import pallas as pl
from jax.experimental.pallas import tpu as pltpu

F32 = jnp.float32
BF16 = jnp.bfloat16
HIGHEST = lax.Precision.HIGHEST

D_MODEL = 1024
BATCH = 16
SEQ = 256
DEPTH = 2
DEC_BATCH = 4
DEC_SEQ = 4096
PAST_LEN = 256
GRID_W = 64
H_A = 4
DK_A = 128
W_A = 512
CHUNK_A = 32
H_B = 4
DH_B = 64
DV_B = 128
QK_B = 512
W_B = 512
ROPE_BASE = 10000.0
H_C = 8
DK_C = 128
W_C = 1024
N_GROUPS = 4
E_PER_GROUP = 8
N_EXPERTS = 32
D_EXPERT = 512
EPS = 1e-6
IN_AB = 5 * W_A + 2 * QK_B + W_B
IN_C = 4 * W_C + 4 * H_C

LANE = 128
N_PROMPT = BATCH * SEQ
N_SAMPLE = DEC_BATCH * DEC_SEQ
N_TOK = N_PROMPT + N_SAMPLE
TM = 256
N_TILES = N_TOK // TM
PROMPT_TILES = N_PROMPT // TM
TILES_PER_SAMPLE = DEC_SEQ // TM
N_MOD = 1 + DEC_BATCH
MOD_ROWS = 8
MOE_ROWS = 256
GDN_CHUNK = 128
VMEM_LIMIT = 56 * 1024 * 1024


def _mod_index(t):
    return jnp.where(t < PROMPT_TILES, 0, 1 + (t - PROMPT_TILES) // TILES_PER_SAMPLE)


def _sigmoid(x):
    return 1.0 / (1.0 + jnp.exp(-x))


def _silu(x):
    return x * _sigmoid(x)


def _rms(x, w):
    return x * lax.rsqrt(jnp.mean(x * x, axis=-1, keepdims=True) + EPS) * w


def _dot(a, b):
    return jnp.dot(a.astype(BF16), b.astype(BF16), preferred_element_type=F32)


def _dot_nt(a, b):
    return lax.dot_general(a.astype(BF16), b.astype(BF16), (((1,), (1,)), ((), ())),
                           preferred_element_type=F32)


def _dot_tn(a, b):
    return lax.dot_general(a.astype(BF16), b.astype(BF16), (((0,), (0,)), ((), ())),
                           preferred_element_type=F32)


def _dot_f32(a, b):
    return jnp.dot(a, b, precision=HIGHEST, preferred_element_type=F32)


def _tri_matmul(tri, x):
    hi = x.astype(BF16)
    rem = x - hi.astype(F32)
    mid = rem.astype(BF16)
    lo = (rem - mid.astype(F32)).astype(BF16)
    n = x.shape[1]
    r = jnp.dot(tri.astype(BF16), jnp.concatenate([hi, mid, lo], axis=1), preferred_element_type=F32)
    return r[:, :n] + r[:, n:2 * n] + r[:, 2 * n:]


def _params(sem):
    return pltpu.CompilerParams(dimension_semantics=sem, vmem_limit_bytes=VMEM_LIMIT)


def _adaln_kernel(c_ref, w_ref, b_ref, o_ref):
    s = _silu(c_ref[...])
    o_ref[0] = _dot(s, w_ref[0]) + b_ref[0]


def _adaln(cvec, ada_w, ada_b):
    nb = 4
    wb = 6 * D_MODEL // nb
    return pl.pallas_call(
        _adaln_kernel,
        out_shape=jax.ShapeDtypeStruct((DEPTH, MOD_ROWS, 6 * D_MODEL), F32),
        grid=(DEPTH, nb),
        in_specs=[pl.BlockSpec((MOD_ROWS, D_MODEL), lambda l, j: (0, 0)),
                  pl.BlockSpec((1, D_MODEL, wb), lambda l, j: (l, 0, j)),
                  pl.BlockSpec((1, 1, wb), lambda l, j: (l, 0, j))],
        out_specs=pl.BlockSpec((1, MOD_ROWS, wb), lambda l, j: (l, 0, j)),
        compiler_params=_params(("parallel", "parallel")),
        name="adaln",
    )(cvec, ada_w, ada_b.reshape(DEPTH, 1, 6 * D_MODEL))


def _inproj_kernel(*refs, rope_cols):
    if rope_cols is None:
        x_ref, mod_ref, nw_ref, w_ref, o_ref = refs
    else:
        x_ref, mod_ref, nw_ref, w_ref, cos_ref, sin_ref, o_ref = refs
    h = _rms(x_ref[...], nw_ref[...]) * (1.0 + mod_ref[0, 1:2, :]) + mod_ref[0, 0:1, :]
    r = jnp.dot(h.astype(BF16), w_ref[...], preferred_element_type=F32)
    if rope_cols is None:
        o_ref[...] = r
        return
    lo, hi = rope_cols
    o_ref[:, :lo] = r[:, :lo]
    o_ref[:, hi:] = r[:, hi:]
    cos = cos_ref[...]
    sin = sin_ref[...]
    lane = lax.broadcasted_iota(jnp.int32, (TM, LANE), 1)
    upper = (lane & 16) != 0
    for c0 in range(lo, hi, LANE):
        v = r[:, c0:c0 + LANE]
        partner = jnp.where(upper, pltpu.roll(v, 16, 1), pltpu.roll(v, LANE - 16, 1))
        o_ref[:, c0:c0 + LANE] = v * cos + partner * sin


def _inproj(x, mod, norm_w, w_bf16, rope=None):
    p = w_bf16.shape[1]
    in_specs = [pl.BlockSpec((TM, D_MODEL), lambda t: (t, 0)),
                pl.BlockSpec((1, MOD_ROWS, D_MODEL), lambda t: (_mod_index(t), 0, 0)),
                pl.BlockSpec((1, D_MODEL), lambda t: (0, 0)),
                pl.BlockSpec((D_MODEL, p), lambda t: (0, 0))]
    args = [x, mod, norm_w.reshape(1, D_MODEL), w_bf16]
    rope_cols = None
    if rope is not None:
        cos, sin, rope_cols = rope

        def rope_index(t):
            return (jnp.where(t < PROMPT_TILES, 0, 1 + (t - PROMPT_TILES) % TILES_PER_SAMPLE), 0)

        in_specs += [pl.BlockSpec((TM, LANE), rope_index), pl.BlockSpec((TM, LANE), rope_index)]
        args += [cos, sin]
    return pl.pallas_call(
        functools.partial(_inproj_kernel, rope_cols=rope_cols),
        out_shape=jax.ShapeDtypeStruct((N_TOK, p), F32),
        grid=(N_TILES,),
        in_specs=in_specs,
        out_specs=pl.BlockSpec((TM, p), lambda t: (t, 0)),
        compiler_params=_params(("parallel",)),
        name="inproj",
    )(*args)


def _rope_tables():
    lane = jnp.arange(LANE)
    d = lane % DH_B
    use_col = (d // 32) == 1
    j = d % 16
    upper = ((d % 32) // 16) == 1
    inv_freq = ROPE_BASE ** (-j.astype(F32) / 16.0)
    t = jnp.arange(DEC_SEQ)
    row = (t // GRID_W).astype(F32)
    col = (t % GRID_W).astype(F32)
    pos = jnp.where(use_col[None, :], col[:, None], row[:, None])
    ang = pos * inv_freq[None, :]
    cos = jnp.cos(ang)
    sin = jnp.where(upper[None, :], jnp.sin(ang), -jnp.sin(ang))
    cos = jnp.concatenate([jnp.ones((TM, LANE), F32), cos], axis=0)
    sin = jnp.concatenate([jnp.zeros((TM, LANE), F32), sin], axis=0)
    return cos, sin


def _route(logits):
    lane = lax.broadcasted_iota(jnp.int32, logits.shape, 1)
    lanef = lane.astype(F32)
    neg = jnp.float32(-jnp.inf)
    gl = jnp.where(lane < N_GROUPS, logits, neg)
    gmax = jnp.max(gl, axis=1, keepdims=True)
    gsel = jnp.min(jnp.where(gl == gmax, lanef, float(LANE)), axis=1, keepdims=True)
    p_grp = 1.0 / jnp.sum(jnp.exp(gl - gmax), axis=1, keepdims=True)
    lo = float(N_GROUPS) + gsel * float(E_PER_GROUP)
    el = jnp.where((lanef >= lo) & (lanef < lo + float(E_PER_GROUP)), logits, neg)
    v1 = jnp.max(el, axis=1, keepdims=True)
    i1 = jnp.min(jnp.where(el == v1, lanef, float(LANE)), axis=1, keepdims=True)
    el2 = jnp.where(lanef == i1, neg, el)
    v2 = jnp.max(el2, axis=1, keepdims=True)
    i2 = jnp.min(jnp.where(el2 == v2, lanef, float(LANE)), axis=1, keepdims=True)
    t = jnp.exp(v2 - v1)
    w1 = p_grp / (1.0 + t)
    w2 = p_grp * t / (1.0 + t)
    out = jnp.where(lane == 0, i1 - float(N_GROUPS), 0.0)
    out = jnp.where(lane == 1, i2 - float(N_GROUPS), out)
    out = jnp.where(lane == 2, w1, out)
    out = jnp.where(lane == 3, w2, out)
    return out


def _outproj_kernel(*refs, widths):
    n_in = len(widths)
    o_refs = refs[:n_in]
    w_ref, x_ref, mod_ref, nw_ref, wr_ref, xn_ref, h2_ref, rt_ref = refs[n_in:]
    y = None
    c0 = 0
    for o_ref, wd in zip(o_refs, widths):
        part = jnp.dot(o_ref[...].astype(BF16), w_ref[c0:c0 + wd, :], preferred_element_type=F32)
        y = part if y is None else y + part
        c0 += wd
    xn = x_ref[...] + mod_ref[0, 2:3, :] * y
    xn_ref[...] = xn
    h2 = _rms(xn, nw_ref[...]) * (1.0 + mod_ref[0, 4:5, :]) + mod_ref[0, 3:4, :]
    h2_ref[...] = h2
    rt_ref[...] = _route(_dot_f32(h2, wr_ref[...]))


def _outproj(parts, w_bf16, x, mod, norm_w2, w_router):
    widths = tuple(p.shape[1] for p in parts)
    in_specs = [pl.BlockSpec((TM, wd), lambda t: (t, 0)) for wd in widths]
    in_specs += [pl.BlockSpec((D_MODEL, D_MODEL), lambda t: (0, 0)),
                 pl.BlockSpec((TM, D_MODEL), lambda t: (t, 0)),
                 pl.BlockSpec((1, MOD_ROWS, D_MODEL), lambda t: (_mod_index(t), 0, 0)),
                 pl.BlockSpec((1, D_MODEL), lambda t: (0, 0)),
                 pl.BlockSpec((D_MODEL, LANE), lambda t: (0, 0))]
    return pl.pallas_call(
        functools.partial(_outproj_kernel, widths=widths),
        out_shape=(jax.ShapeDtypeStruct((N_TOK, D_MODEL), F32),
                   jax.ShapeDtypeStruct((N_TOK, D_MODEL), F32),
                   jax.ShapeDtypeStruct((N_TOK, LANE), F32)),
        grid=(N_TILES,),
        in_specs=in_specs,
        out_specs=(pl.BlockSpec((TM, D_MODEL), lambda t: (t, 0)),
                   pl.BlockSpec((TM, D_MODEL), lambda t: (t, 0)),
                   pl.BlockSpec((TM, LANE), lambda t: (t, 0))),
        compiler_params=_params(("parallel",)),
        name="outproj",
    )(*parts, w_bf16, x, mod, norm_w2.reshape(1, D_MODEL), w_router)


def _expert_kernel(blk_e_ref, nact_ref, x_ref, wg_ref, wu_ref, wd_ref, y_ref, wg_s, wu_s, wd_s):
    i = pl.program_id(0)
    e = blk_e_ref[i]
    prev = blk_e_ref[jnp.maximum(i - 1, 0)]

    @pl.when((i == 0) | (e != prev))
    def _():
        wg_s[...] = wg_ref[...].astype(BF16)
        wu_s[...] = wu_ref[...].astype(BF16)
        wd_s[...] = wd_ref[...].astype(BF16)

    @pl.when(i < nact_ref[0])
    def _():
        x = x_ref[...].astype(BF16)
        g = jnp.dot(x, wg_s[...], preferred_element_type=F32)
        u = jnp.dot(x, wu_s[...], preferred_element_type=F32)
        y_ref[...] = jnp.dot((_silu(g) * u).astype(BF16), wd_s[...], preferred_element_type=F32)

    @pl.when(i >= nact_ref[0])
    def _():
        y_ref[...] = jnp.zeros_like(y_ref)


def _experts(xb, blk_e, n_active, w_gate, w_up, w_down):
    n_rows = xb.shape[0]
    n_blocks = n_rows // MOE_ROWS
    grid_spec = pltpu.PrefetchScalarGridSpec(
        num_scalar_prefetch=2,
        grid=(n_blocks,),
        in_specs=[pl.BlockSpec((MOE_ROWS, D_MODEL), lambda i, be, na: (i, 0)),
                  pl.BlockSpec((None, D_MODEL, D_EXPERT), lambda i, be, na: (be[i], 0, 0)),
                  pl.BlockSpec((None, D_MODEL, D_EXPERT), lambda i, be, na: (be[i], 0, 0)),
                  pl.BlockSpec((None, D_EXPERT, D_MODEL), lambda i, be, na: (be[i], 0, 0))],
        out_specs=pl.BlockSpec((MOE_ROWS, D_MODEL), lambda i, be, na: (i, 0)),
        scratch_shapes=[pltpu.VMEM((D_MODEL, D_EXPERT), BF16),
                        pltpu.VMEM((D_MODEL, D_EXPERT), BF16),
                        pltpu.VMEM((D_EXPERT, D_MODEL), BF16)])
    return pl.pallas_call(
        _expert_kernel,
        out_shape=jax.ShapeDtypeStruct((n_rows, D_MODEL), F32),
        grid_spec=grid_spec,
        compiler_params=_params(("arbitrary",)),
        name="experts",
    )(blk_e, n_active, xb, w_gate, w_up, w_down)


def _combine_kernel(x_ref, y0_ref, y1_ref, rt_ref, mod_ref, o_ref):
    rt = rt_ref[...]
    y = rt[:, 2:3] * y0_ref[...] + rt[:, 3:4] * y1_ref[...]
    o_ref[...] = x_ref[...] + mod_ref[0, 5:6, :] * y


def _combine(x, y0, y1, route, mod):
    row = pl.BlockSpec((TM, D_MODEL), lambda t: (t, 0))
    return pl.pallas_call(
        _combine_kernel,
        out_shape=jax.ShapeDtypeStruct((N_TOK, D_MODEL), F32),
        grid=(N_TILES,),
        in_specs=[row, row, row,
                  pl.BlockSpec((TM, LANE), lambda t: (t, 0)),
                  pl.BlockSpec((1, MOD_ROWS, D_MODEL), lambda t: (_mod_index(t), 0, 0))],
        out_specs=row,
        compiler_params=_params(("parallel",)),
        name="combine",
    )(x, y0, y1, route, mod)


def _moe(x, h2, route, mod, w_gate, w_up, w_down):
    n_asg = 2 * N_TOK
    flat_e = route[:, :2].astype(jnp.int32).reshape(n_asg)
    flat_tok = jnp.arange(n_asg, dtype=jnp.int32) // 2
    onehot = (flat_e[:, None] == jnp.arange(N_EXPERTS, dtype=jnp.int32)[None, :]).astype(jnp.int32)
    csum = jnp.cumsum(onehot, axis=0)
    rank = jnp.take_along_axis(csum, flat_e[:, None], axis=1)[:, 0] - 1
    counts = csum[-1]
    padded = (counts + MOE_ROWS - 1) // MOE_ROWS * MOE_ROWS
    pad_end = jnp.cumsum(padded)
    pad_start = pad_end - padded
    dest = pad_start[flat_e] + rank
    n_rows = n_asg + N_EXPERTS * MOE_ROWS
    n_blocks = n_rows // MOE_ROWS
    row_tok = jnp.zeros((n_rows,), jnp.int32).at[dest].set(flat_tok)
    blk_start = jnp.arange(n_blocks, dtype=jnp.int32) * MOE_ROWS
    blk_e = jnp.minimum(jnp.searchsorted(pad_end, blk_start, side='right'), N_EXPERTS - 1).astype(jnp.int32)
    n_active = (pad_end[-1:] // MOE_ROWS).astype(jnp.int32)
    xb = jnp.take(h2, row_tok, axis=0)
    yb = _experts(xb, blk_e, n_active, w_gate, w_up, w_down)
    dest2 = dest.reshape(N_TOK, 2)
    y0 = jnp.take(yb, dest2[:, 0], axis=0)
    y1 = jnp.take(yb, dest2[:, 1], axis=0)
    return _combine(x, y0, y1, route, mod)


def _final_kernel(x_ref, w_ref, o_ref):
    o_ref[...] = _rms(x_ref[...], w_ref[...])


def _final_norm(x, w, tile0, n_rows):
    return pl.pallas_call(
        _final_kernel,
        out_shape=jax.ShapeDtypeStruct((n_rows, D_MODEL), F32),
        grid=(n_rows // TM,),
        in_specs=[pl.BlockSpec((TM, D_MODEL), lambda t: (t + tile0, 0)),
                  pl.BlockSpec((1, D_MODEL), lambda t: (0, 0))],
        out_specs=pl.BlockSpec((TM, D_MODEL), lambda t: (t, 0)),
        compiler_params=_params(("parallel",)),
        name="final_norm",
    )(x, w.reshape(1, D_MODEL))


def _hgrn_kernel(*refs, seq_len, layer, context):
    if context:
        (q_ref, ff_ref, fb_ref, v_ref, g_ref, lbl_ref, nw_ref,
         o_ref, sfin_ref, of_s, ob_s, sf_s, sb_s) = refs
    else:
        (q_ref, ff_ref, fb_ref, v_ref, g_ref, lbl_ref, nw_ref, s0_ref, _,
         o_ref, of_s, ob_s, sf_s, sb_s) = refs
    c = CHUNK_A
    n = seq_len // c

    def lower_bound(d):
        z = lbl_ref[d]
        e = jnp.exp(z - jnp.max(z, axis=0, keepdims=True))
        return jnp.sum(e[:layer + 1], axis=0, keepdims=True) / jnp.sum(e, axis=0, keepdims=True)

    lb_f = lower_bound(0)
    lb_b = lower_bound(1)
    row = lax.broadcasted_iota(jnp.int32, (c, c), 0)
    col = lax.broadcasted_iota(jnp.int32, (c, c), 1)
    causal = row >= col
    tri_f = causal.astype(F32)
    tri_b = (row <= col).astype(F32)

    if context:
        sf_s[...] = jnp.zeros_like(sf_s)
        sb_s[...] = jnp.zeros_like(sb_s)
    else:
        sf_s[...] = s0_ref[0, 0, 0].T
        sb_s[...] = s0_ref[0, 1, 0].T

    def chunk(r0, f_ref, lb, tri, mask, last, st_ref, out_s):
        rows = pl.ds(r0, c)
        f = lb + (1.0 - lb) * _sigmoid(f_ref[rows, :])
        b = _tri_matmul(tri, jnp.log(f))
        b_last = b[last:last + 1, :]
        q_in = _silu(q_ref[rows, :]) * jnp.exp(b)
        k = 1.0 - f
        v = v_ref[rows, :]
        a = jnp.where(mask, _dot_nt(q_in, k * jnp.exp(-b)), 0.0)
        st = st_ref[...]
        out_s[rows, :] = _dot(a, v) + _dot_nt(q_in, st)
        st_ref[...] = st * jnp.exp(b_last) + _dot_tn(v, k * jnp.exp(b_last - b))

    def body(i, carry):
        chunk(pl.multiple_of(i * c, c), ff_ref, lb_f, tri_f, causal, c - 1, sf_s, of_s)
        chunk(pl.multiple_of((n - 1 - i) * c, c), fb_ref, lb_b, tri_b, row <= col, 0, sb_s, ob_s)
        return carry

    lax.fori_loop(0, n, body, 0)

    if context:
        sfin_ref[0, 0, 0] = sf_s[...].T
        sfin_ref[0, 1, 0] = sb_s[...].T

    nw = nw_ref[...]

    def epilogue(j, carry):
        rows = pl.ds(pl.multiple_of(j * TM, TM), TM)
        o = of_s[rows, :] + ob_s[rows, :]
        o_ref[rows, :] = _rms(o, nw) * _silu(g_ref[rows, :])
        return carry

    lax.fori_loop(0, seq_len // TM, epilogue, 0)


def _hgrn(proj, lb_logits, norm_w, layer, context, s0=None, o_prev=None):
    seq_len = SEQ if context else DEC_SEQ
    bsz = BATCH if context else DEC_BATCH
    row0 = 0 if context else N_PROMPT // seq_len

    def col(k):
        return pl.BlockSpec((seq_len, LANE), lambda b, h: (row0 + b, k * H_A + h))

    in_specs = [col(0), col(1), col(2), col(3), col(4),
                pl.BlockSpec((2, DEPTH + 1, LANE), lambda b, h: (0, 0, h)),
                pl.BlockSpec((1, LANE), lambda b, h: (0, 0))]
    args = [proj, proj, proj, proj, proj, lb_logits, norm_w.reshape(1, LANE)]
    o_shape = jax.ShapeDtypeStruct((N_TOK, W_A), F32)
    o_spec = pl.BlockSpec((seq_len, LANE), lambda b, h: (row0 + b, h))
    scratch = [pltpu.VMEM((seq_len, LANE), F32), pltpu.VMEM((seq_len, LANE), F32),
               pltpu.VMEM((LANE, LANE), F32), pltpu.VMEM((LANE, LANE), F32)]
    kern = functools.partial(_hgrn_kernel, seq_len=seq_len, layer=layer, context=context)
    if context:
        return pl.pallas_call(
            kern,
            out_shape=(o_shape, jax.ShapeDtypeStruct((bsz, 2, H_A, DK_A, DK_A), F32)),
            grid=(bsz, H_A),
            in_specs=in_specs,
            out_specs=(o_spec, pl.BlockSpec((1, 2, 1, DK_A, DK_A), lambda b, h: (b, 0, h, 0, 0))),
            scratch_shapes=scratch,
            compiler_params=_params(("parallel", "parallel")),
            name="hgrn_ctx",
        )(*args)
    in_specs += [pl.BlockSpec((1, 2, 1, DK_A, DK_A), lambda b, h: (b, 0, h, 0, 0)),
                 pl.BlockSpec(memory_space=pl.ANY)]
    args += [s0, o_prev]
    return pl.pallas_call(
        kern,
        out_shape=o_shape,
        grid=(bsz, H_A),
        in_specs=in_specs,
        out_specs=o_spec,
        scratch_shapes=scratch,
        input_output_aliases={len(args) - 1: 0},
        compiler_params=_params(("parallel", "parallel")),
        name="hgrn_dec",
    )(*args)


ATT_TQ = 256
COL_QD = 5 * W_A // LANE
COL_KD = COL_QD + QK_B // LANE
COL_VD = COL_KD + QK_B // LANE


def _attn_kernel(*refs, seq_len, layer, context):
    if context:
        q_ref, k_ref, v_ref, lam_ref, nw_ref, o_ref, k_s, v_s = refs
    else:
        q_ref, k_ref, v_ref, ck_ref, cv_ref, lam_ref, nw_ref, _, o_ref, k_s, v_s = refs

    @pl.when(pl.program_id(2) == 0)
    def _():
        k_s[0:seq_len, :] = k_ref[...].astype(BF16)
        v_s[0:seq_len, :] = v_ref[...].astype(BF16)
        if not context:
            k_s[seq_len:, :] = ck_ref[0].astype(BF16)
            v_s[seq_len:, :] = cv_ref[0].astype(BF16)

    lam_init = 0.8 - 0.6 * math.exp(-0.3 * layer)
    lp = lam_ref[...]
    lam = (jnp.exp(jnp.sum(lp[0:1] * lp[1:2], axis=1, keepdims=True))
           - jnp.exp(jnp.sum(lp[2:3] * lp[3:4], axis=1, keepdims=True)) + lam_init)

    q = q_ref[...] * (DH_B ** -0.5)
    lane = lax.broadcasted_iota(jnp.int32, q.shape, 1)
    k = k_s[...]

    def softmax_map(first):
        s = _dot_nt(jnp.where((lane < DH_B) == first, q, 0.0), k)
        p = jnp.exp(s - jnp.max(s, axis=1, keepdims=True))
        return p, jnp.sum(p, axis=1, keepdims=True)

    p0, l0 = softmax_map(True)
    p1, l1 = softmax_map(False)
    a = p0 - (lam * l0 / l1) * p1
    o = jnp.dot(a.astype(BF16), v_s[...], preferred_element_type=F32) / l0
    o_ref[...] = _rms(o, nw_ref[...]) * (1.0 - lam_init)


def _attn(proj, lam_p, norm_w, layer, context, cache_k=None, cache_v=None, o_prev=None):
    seq_len = SEQ if context else DEC_SEQ
    bsz = BATCH if context else DEC_BATCH
    row0 = 0 if context else N_PROMPT // seq_len
    nq = seq_len // ATT_TQ
    tile0 = row0 * nq
    t_k = seq_len if context else seq_len + PAST_LEN
    in_specs = [pl.BlockSpec((ATT_TQ, LANE), lambda b, h, i: (tile0 + b * nq + i, COL_QD + h)),
                pl.BlockSpec((seq_len, LANE), lambda b, h, i: (row0 + b, COL_KD + h)),
                pl.BlockSpec((seq_len, LANE), lambda b, h, i: (row0 + b, COL_VD + h))]
    args = [proj, proj, proj]
    if not context:
        in_specs += [pl.BlockSpec((1, PAST_LEN, LANE), lambda b, h, i: (b, 0, h)),
                     pl.BlockSpec((1, PAST_LEN, LANE), lambda b, h, i: (b, 0, h))]
        args += [cache_k, cache_v]
    in_specs += [pl.BlockSpec((4, DH_B), lambda b, h, i: (0, 0)),
                 pl.BlockSpec((1, LANE), lambda b, h, i: (0, 0))]
    args += [lam_p, norm_w.reshape(1, LANE)]
    aliases = {}
    if not context:
        in_specs += [pl.BlockSpec(memory_space=pl.ANY)]
        args += [o_prev]
        aliases = {len(args) - 1: 0}
    return pl.pallas_call(
        functools.partial(_attn_kernel, seq_len=seq_len, layer=layer, context=context),
        out_shape=jax.ShapeDtypeStruct((N_TOK, W_B), F32),
        grid=(bsz, H_B, nq),
        in_specs=in_specs,
        out_specs=pl.BlockSpec((ATT_TQ, LANE), lambda b, h, i: (tile0 + b * nq + i, h)),
        scratch_shapes=[pltpu.VMEM((t_k, LANE), BF16), pltpu.VMEM((t_k, LANE), BF16)],
        input_output_aliases=aliases,
        compiler_params=_params(("parallel", "parallel", "arbitrary")),
        name="attn_ctx" if context else "attn_dec",
    )(*args)


COL_GATES = 4 * W_C // LANE


def _gdn_gates_kernel(g_ref, alog_ref, dtb_ref, col_ref, row_ref):
    c = GDN_CHUNK
    raw = g_ref[...]
    z = raw + dtb_ref[...]
    softplus = jnp.maximum(z, 0.0) + jnp.log(1.0 + jnp.exp(-jnp.abs(z)))
    g = -jnp.exp(alog_ref[...]) * softplus
    row = lax.broadcasted_iota(jnp.int32, (c, c), 0)
    col = lax.broadcasted_iota(jnp.int32, (c, c), 1)
    lane = lax.broadcasted_iota(jnp.int32, (c, LANE), 1)
    prefix = _tri_matmul((row >= col).astype(F32), g)
    suffix = _tri_matmul((row <= col).astype(F32), g)
    out = jnp.where(lane < H_C, prefix, jnp.where(lane < 2 * H_C, suffix, _sigmoid(raw)))
    col_ref[...] = out
    row_ref[0] = out.T


def _gdn_gates(proj, alog_lane, dtb_lane):
    c = GDN_CHUNK
    n = N_TOK // c
    return pl.pallas_call(
        _gdn_gates_kernel,
        out_shape=(jax.ShapeDtypeStruct((N_TOK, LANE), F32), jax.ShapeDtypeStruct((n, LANE, c), F32)),
        grid=(n,),
        in_specs=[pl.BlockSpec((c, LANE), lambda i: (i, COL_GATES)),
                  pl.BlockSpec((1, LANE), lambda i: (0, 0)),
                  pl.BlockSpec((1, LANE), lambda i: (0, 0))],
        out_specs=(pl.BlockSpec((c, LANE), lambda i: (i, 0)),
                   pl.BlockSpec((1, LANE, c), lambda i: (i, 0, 0))),
        compiler_params=_params(("parallel",)),
        name="gdn_gates",
    )(proj, alog_lane, dtb_lane)


def _unit_tri_inverse(m, row, col):
    c = m.shape[0]
    eye = (row == col).astype(F32)
    a = jnp.where((row // 16) == (col // 16), m, 0.0)
    a2 = _dot(a, a)
    a4 = _dot(a2, a2)
    a8 = _dot(a4, a4)
    t = eye - a
    t = t + _dot(t, a2)
    t = t + _dot(t, a4)
    t = t + _dot(t, a8)
    blk = 32
    while blk <= c:
        off = ((row // blk) == (col // blk)) & ((row // (blk // 2)) != (col // (blk // 2)))
        t = t - _dot(_dot(t, jnp.where(off, m, 0.0)), t)
        blk *= 2
    return t


def _gdn_kernel(*refs, seq_len, context):
    if context:
        (q_ref, k_ref, v_ref, go_ref, cwq_ref, cwk_ref, cwv_ref, gcol_ref, grow_ref, nw_ref,
         o_ref, sfin_ref, qn_s, kn_s, vn_s, of_s, ob_s, sf_s, sb_s) = refs
    else:
        (q_ref, k_ref, v_ref, go_ref, cwq_ref, cwk_ref, cwv_ref, gcol_ref, grow_ref, nw_ref, s0_ref, _,
         o_ref, qn_s, kn_s, vn_s, of_s, ob_s, sf_s, sb_s) = refs
    c = GDN_CHUNK
    n = seq_len // c
    head = pl.program_id(1)
    row = lax.broadcasted_iota(jnp.int32, (c, c), 0)
    col = lax.broadcasted_iota(jnp.int32, (c, c), 1)
    lane = lax.broadcasted_iota(jnp.int32, (c, LANE), 1)
    rowi = lax.broadcasted_iota(jnp.int32, (c, LANE), 0)

    def prep(i, carry):
        r0 = pl.multiple_of(i * c, c)
        rows = pl.ds(r0, c)

        def conv(x_ref, w_ref):
            cur = x_ref[rows, :]
            before = x_ref[pl.ds(pl.multiple_of(jnp.maximum(r0 - 8, 0), 8), 8), :]
            after = x_ref[pl.ds(pl.multiple_of(jnp.minimum(r0 + c, seq_len - 8), 8), 8), :]
            prev_row = jnp.where(i > 0, before[7:8, :], 0.0)
            next_row = jnp.where(i < n - 1, after[0:1, :], 0.0)
            xm1 = jnp.where(rowi == 0, prev_row, pltpu.roll(cur, 1, 0))
            xp1 = jnp.where(rowi == c - 1, next_row, pltpu.roll(cur, c - 1, 0))
            w = w_ref[...]
            return _silu(xm1 * w[0:1, :] + cur * w[1:2, :] + xp1 * w[2:3, :])

        q = conv(q_ref, cwq_ref)
        k = conv(k_ref, cwk_ref)
        qn_s[rows, :] = q * lax.rsqrt(jnp.sum(q * q, axis=1, keepdims=True) + EPS) * (DK_C ** -0.5)
        kn_s[rows, :] = k * lax.rsqrt(jnp.sum(k * k, axis=1, keepdims=True) + EPS)
        vn_s[rows, :] = conv(v_ref, cwv_ref)
        return carry

    lax.fori_loop(0, n, prep, 0)

    if context:
        sf_s[...] = jnp.zeros_like(sf_s)
        sb_s[...] = jnp.zeros_like(sb_s)
    else:
        sf_s[...] = s0_ref[0, 0, 0]
        sb_s[...] = s0_ref[0, 1, 0]

    def chunk(ci, d, s_ref, out_s):
        forward = d == 0
        rows = pl.ds(pl.multiple_of(ci * c, c), c)
        gates = gcol_ref[rows, :]
        gc = jnp.sum(jnp.where(lane == d * H_C + head, gates, 0.0), axis=1, keepdims=True)
        beta = jnp.sum(jnp.where(lane == (2 + d) * H_C + head, gates, 0.0), axis=1, keepdims=True)
        gc_row = grow_ref[ci, pl.ds(d * H_C + head, 1), :]
        gc_last = gc_row[:, c - 1:c] if forward else gc_row[:, 0:1]
        incl = (row >= col) if forward else (row <= col)
        strict = (row > col) if forward else (row < col)
        decay = jnp.where(incl, jnp.exp(gc - gc_row), 0.0)
        qn = qn_s[rows, :]
        kn = kn_s[rows, :]
        vn = vn_s[rows, :]
        m = jnp.where(strict, _dot_nt(kn, kn) * beta * decay, 0.0)
        qk = _dot_nt(qn, kn) * decay
        t = _unit_tri_inverse(m, row, col)
        e = jnp.exp(gc)
        uw = _dot(t, jnp.concatenate([vn * beta, kn * (beta * e)], axis=1))
        s = s_ref[...]
        v_new = uw[:, :LANE] - _dot(uw[:, LANE:], s)
        out_s[rows, :] = _dot(qn * e, s) + _dot(qk, v_new)
        s_ref[...] = s * jnp.exp(gc_last) + _dot_tn(kn * jnp.exp(gc_last - gc), v_new)

    def body(i, carry):
        chunk(i, 0, sf_s, of_s)
        chunk(n - 1 - i, 1, sb_s, ob_s)
        return carry

    lax.fori_loop(0, n, body, 0)

    if context:
        sfin_ref[0, 0, 0] = sf_s[...]
        sfin_ref[0, 1, 0] = sb_s[...]

    nw = nw_ref[...]

    def epilogue(j, carry):
        rows = pl.ds(pl.multiple_of(j * TM, TM), TM)
        o = of_s[rows, :] + ob_s[rows, :]
        o_ref[rows, :] = _rms(o, nw) * _silu(go_ref[rows, :])
        return carry

    lax.fori_loop(0, seq_len // TM, epilogue, 0)


def _gdn(proj, conv_w, gcol, grow, norm_w, context, s0=None, o_prev=None):
    seq_len = SEQ if context else DEC_SEQ
    bsz = BATCH if context else DEC_BATCH
    row0 = 0 if context else N_PROMPT // seq_len
    nc = seq_len // GDN_CHUNK

    def col(k):
        return pl.BlockSpec((seq_len, LANE), lambda b, h: (row0 + b, k * H_C + h))

    def cw(k):
        return pl.BlockSpec((3, LANE), lambda b, h: (0, k * H_C + h))

    in_specs = [col(0), col(1), col(2), col(3), cw(0), cw(1), cw(2),
                pl.BlockSpec((seq_len, LANE), lambda b, h: (row0 + b, 0)),
                pl.BlockSpec((nc, LANE, GDN_CHUNK), lambda b, h: (row0 + b, 0, 0)),
                pl.BlockSpec((1, LANE), lambda b, h: (0, 0))]
    args = [proj, proj, proj, proj, conv_w, conv_w, conv_w, gcol, grow, norm_w.reshape(1, LANE)]
    o_shape = jax.ShapeDtypeStruct((N_TOK, W_C), F32)
    o_spec = pl.BlockSpec((seq_len, LANE), lambda b, h: (row0 + b, h))
    seq_buf = pltpu.VMEM((seq_len, LANE), F32)
    state_buf = pltpu.VMEM((DK_C, DK_C), F32)
    scratch = [seq_buf, seq_buf, seq_buf, seq_buf, seq_buf, state_buf, state_buf]
    kern = functools.partial(_gdn_kernel, seq_len=seq_len, context=context)
    state_spec = pl.BlockSpec((1, 2, 1, DK_C, DK_C), lambda b, h: (b, 0, h, 0, 0))
    if context:
        return pl.pallas_call(
            kern,
            out_shape=(o_shape, jax.ShapeDtypeStruct((bsz, 2, H_C, DK_C, DK_C), F32)),
            grid=(bsz, H_C),
            in_specs=in_specs,
            out_specs=(o_spec, state_spec),
            scratch_shapes=scratch,
            compiler_params=_params(("parallel", "parallel")),
            name="gdn_ctx",
        )(*args)
    in_specs += [state_spec, pl.BlockSpec(memory_space=pl.ANY)]
    args += [s0, o_prev]
    return pl.pallas_call(
        kern,
        out_shape=o_shape,
        grid=(bsz, H_C),
        in_specs=in_specs,
        out_specs=o_spec,
        scratch_shapes=scratch,
        input_output_aliases={len(args) - 1: 0},
        compiler_params=_params(("parallel", "parallel")),
        name="gdn_dec",
    )(*args)


def kernel(x_prompt, x_sample, c, cache_diff_k, cache_diff_v, state_hgrn, state_gdn, c_ctx,
           ada_w, ada_b, norm_w, final_norm_w, w_in_ab, hgrn_lb_logits, hgrn_norm_w,
           diff_lambda, diff_norm_w, w_in_c, gdn_conv_w, gdn_a_log, gdn_dt_bias, gdn_norm_w,
           w_out, moe_router_group, moe_router_expert, moe_w_gate, moe_w_up, moe_w_down):
    x = jnp.concatenate([x_prompt.reshape(N_PROMPT, D_MODEL), x_sample.reshape(N_SAMPLE, D_MODEL)], axis=0)
    cvec = jnp.concatenate([c_ctx[None, :], c, jnp.zeros((MOD_ROWS - N_MOD, D_MODEL), F32)], axis=0)
    mod_all = _adaln(cvec, ada_w, ada_b)[:, :N_MOD].reshape(DEPTH, N_MOD, 6, D_MODEL)
    mod_all = jnp.pad(mod_all, ((0, 0), (0, 0), (0, MOD_ROWS - 6), (0, 0)))
    cos, sin = _rope_tables()
    new_k, new_v, new_hgrn, new_gdn = [], [], [], []
    for l in range(DEPTH):
        i = l // 2
        mod = mod_all[l]
        if l % 2 == 0:
            proj = _inproj(x, mod, norm_w[l, 0], w_in_ab[i].astype(BF16),
                           rope=(cos, sin, (COL_QD * LANE, COL_VD * LANE)))
            o_h, s_h = _hgrn(proj, hgrn_lb_logits, hgrn_norm_w[i], l, True)
            o_h = _hgrn(proj, hgrn_lb_logits, hgrn_norm_w[i], l, False, s0=state_hgrn[:, i], o_prev=o_h)
            o_d = _attn(proj, diff_lambda[i], diff_norm_w[i], l, True)
            o_d = _attn(proj, diff_lambda[i], diff_norm_w[i], l, False,
                        cache_k=cache_diff_k[:, i].reshape(DEC_BATCH, PAST_LEN, QK_B),
                        cache_v=cache_diff_v[:, i].reshape(DEC_BATCH, PAST_LEN, W_B), o_prev=o_d)
            parts = (o_h, o_d)
            new_k.append(proj[:N_PROMPT, COL_KD * LANE:COL_VD * LANE].reshape(BATCH, SEQ, H_B, 2, DH_B))
            new_v.append(proj[:N_PROMPT, COL_VD * LANE:].reshape(BATCH, SEQ, H_B, DV_B))
            new_hgrn.append(s_h)
        else:
            w_c = jnp.pad(w_in_c[i], ((0, 0), (0, (COL_GATES + 1) * LANE - IN_C))).astype(BF16)
            proj = _inproj(x, mod, norm_w[l, 0], w_c)
            pad = jnp.zeros((LANE - 2 * H_C,), F32)
            alog_lane = jnp.concatenate([gdn_a_log[i, 0], gdn_a_log[i, 1], pad]).reshape(1, LANE)
            dtb_lane = jnp.concatenate([gdn_dt_bias[i, 0], gdn_dt_bias[i, 1], pad]).reshape(1, LANE)
            gcol, grow = _gdn_gates(proj, alog_lane, dtb_lane)
            o_c, s_c = _gdn(proj, gdn_conv_w[i], gcol, grow, gdn_norm_w[i], True)
            o_c = _gdn(proj, gdn_conv_w[i], gcol, grow, gdn_norm_w[i], False, s0=state_gdn[:, i], o_prev=o_c)
            parts = (o_c,)
            new_gdn.append(s_c)
        w_router = jnp.concatenate(
            [moe_router_group[l], moe_router_expert[l],
             jnp.zeros((D_MODEL, LANE - N_GROUPS - N_EXPERTS), F32)], axis=1)
        x, h2, route = _outproj(parts, w_out[l].astype(BF16), x, mod, norm_w[l, 1], w_router)
        x = _moe(x, h2, route, mod, moe_w_gate[l], moe_w_up[l], moe_w_down[l])
    y_prompt = _final_norm(x, final_norm_w, 0, N_PROMPT).reshape(BATCH, SEQ, D_MODEL)
    y_sample = _final_norm(x, final_norm_w, PROMPT_TILES, N_SAMPLE).reshape(DEC_BATCH, DEC_SEQ, D_MODEL)
    return (y_prompt, y_sample, jnp.stack(new_k, axis=1), jnp.stack(new_v, axis=1),
            jnp.stack(new_hgrn, axis=1), jnp.stack(new_gdn, axis=1))
```

```python
import functools
import math

import jax
import jax.numpy as jnp
from jax import lax
from jax.experimental import pallas as pl
from jax.experimental.pallas import tpu as pltpu

F32 = jnp.float32
BF16 = jnp.bfloat16
HIGHEST = lax.Precision.HIGHEST

D_MODEL = 1024
BATCH = 16
SEQ = 256
DEPTH = 2
DEC_BATCH = 4
DEC_SEQ = 4096
PAST_LEN = 256
GRID_W = 64
H_A = 4
DK_A = 128
W_A = 512
CHUNK_A = 32
H_B = 4
DH_B = 64
DV_B = 128
QK_B = 512
W_B = 512
ROPE_BASE = 10000.0
H_C = 8
DK_C = 128
W_C = 1024
N_GROUPS = 4
E_PER_GROUP = 8
N_EXPERTS = 32
D_EXPERT = 512
EPS = 1e-6
IN_AB = 5 * W_A + 2 * QK_B + W_B
IN_C = 4 * W_C + 4 * H_C

LANE = 128
N_PROMPT = BATCH * SEQ
N_SAMPLE = DEC_BATCH * DEC_SEQ
N_TOK = N_PROMPT + N_SAMPLE
TM = 256
N_TILES = N_TOK // TM
PROMPT_TILES = N_PROMPT // TM
TILES_PER_SAMPLE = DEC_SEQ // TM
N_MOD = 1 + DEC_BATCH
MOD_ROWS = 8
MOE_ROWS = 256
GDN_CHUNK = 128
VMEM_LIMIT = 56 * 1024 * 1024


def _mod_index(t):
    return jnp.where(t < PROMPT_TILES, 0, 1 + (t - PROMPT_TILES) // TILES_PER_SAMPLE)


def _sigmoid(x):
    return 1.0 / (1.0 + jnp.exp(-x))


def _silu(x):
    return x * _sigmoid(x)


def _rms(x, w):
    return x * lax.rsqrt(jnp.mean(x * x, axis=-1, keepdims=True) + EPS) * w


def _dot(a, b):
    return jnp.dot(a.astype(BF16), b.astype(BF16), preferred_element_type=F32)


def _dot_nt(a, b):
    return lax.dot_general(a.astype(BF16), b.astype(BF16), (((1,), (1,)), ((), ())),
                           preferred_element_type=F32)


def _dot_tn(a, b):
    return lax.dot_general(a.astype(BF16), b.astype(BF16), (((0,), (0,)), ((), ())),
                           preferred_element_type=F32)


def _dot_f32(a, b):
    return jnp.dot(a, b, precision=HIGHEST, preferred_element_type=F32)


def _tri_matmul(tri, x):
    hi = x.astype(BF16)
    rem = x - hi.astype(F32)
    mid = rem.astype(BF16)
    lo = (rem - mid.astype(F32)).astype(BF16)
    n = x.shape[1]
    r = jnp.dot(tri.astype(BF16), jnp.concatenate([hi, mid, lo], axis=1), preferred_element_type=F32)
    return r[:, :n] + r[:, n:2 * n] + r[:, 2 * n:]


def _params(sem):
    return pltpu.CompilerParams(dimension_semantics=sem, vmem_limit_bytes=VMEM_LIMIT)


def _adaln_kernel(c_ref, w_ref, b_ref, o_ref):
    s = _silu(c_ref[...])
    o_ref[0] = _dot(s, w_ref[0]) + b_ref[0]


def _adaln(cvec, ada_w, ada_b):
    nb = 4
    wb = 6 * D_MODEL // nb
    return pl.pallas_call(
        _adaln_kernel,
        out_shape=jax.ShapeDtypeStruct((DEPTH, MOD_ROWS, 6 * D_MODEL), F32),
        grid=(DEPTH, nb),
        in_specs=[pl.BlockSpec((MOD_ROWS, D_MODEL), lambda l, j: (0, 0)),
                  pl.BlockSpec((1, D_MODEL, wb), lambda l, j: (l, 0, j)),
                  pl.BlockSpec((1, 1, wb), lambda l, j: (l, 0, j))],
        out_specs=pl.BlockSpec((1, MOD_ROWS, wb), lambda l, j: (l, 0, j)),
        compiler_params=_params(("parallel", "parallel")),
        name="adaln",
    )(cvec, ada_w, ada_b.reshape(DEPTH, 1, 6 * D_MODEL))


def _inproj_kernel(*refs, rope_cols):
    if rope_cols is None:
        x_ref, mod_ref, nw_ref, w_ref, o_ref = refs
    else:
        x_ref, mod_ref, nw_ref, w_ref, cos_ref, sin_ref, o_ref = refs
    h = _rms(x_ref[...], nw_ref[...]) * (1.0 + mod_ref[0, 1:2, :]) + mod_ref[0, 0:1, :]
    r = jnp.dot(h.astype(BF16), w_ref[...], preferred_element_type=F32)
    if rope_cols is None:
        o_ref[...] = r
        return
    lo, hi = rope_cols
    o_ref[:, :lo] = r[:, :lo]
    o_ref[:, hi:] = r[:, hi:]
    cos = cos_ref[...]
    sin = sin_ref[...]
    lane = lax.broadcasted_iota(jnp.int32, (TM, LANE), 1)
    upper = (lane & 16) != 0
    for c0 in range(lo, hi, LANE):
        v = r[:, c0:c0 + LANE]
        partner = jnp.where(upper, pltpu.roll(v, 16, 1), pltpu.roll(v, LANE - 16, 1))
        o_ref[:, c0:c0 + LANE] = v * cos + partner * sin


def _inproj(x, mod, norm_w, w_bf16, rope=None):
    p = w_bf16.shape[1]
    in_specs = [pl.BlockSpec((TM, D_MODEL), lambda t: (t, 0)),
                pl.BlockSpec((1, MOD_ROWS, D_MODEL), lambda t: (_mod_index(t), 0, 0)),
                pl.BlockSpec((1, D_MODEL), lambda t: (0, 0)),
                pl.BlockSpec((D_MODEL, p), lambda t: (0, 0))]
    args = [x, mod, norm_w.reshape(1, D_MODEL), w_bf16]
    rope_cols = None
    if rope is not None:
        cos, sin, rope_cols = rope

        def rope_index(t):
            return (jnp.where(t < PROMPT_TILES, 0, 1 + (t - PROMPT_TILES) % TILES_PER_SAMPLE), 0)

        in_specs += [pl.BlockSpec((TM, LANE), rope_index), pl.BlockSpec((TM, LANE), rope_index)]
        args += [cos, sin]
    return pl.pallas_call(
        functools.partial(_inproj_kernel, rope_cols=rope_cols),
        out_shape=jax.ShapeDtypeStruct((N_TOK, p), F32),
        grid=(N_TILES,),
        in_specs=in_specs,
        out_specs=pl.BlockSpec((TM, p), lambda t: (t, 0)),
        compiler_params=_params(("parallel",)),
        name="inproj",
    )(*args)


def _rope_tables():
    lane = jnp.arange(LANE)
    d = lane % DH_B
    use_col = (d // 32) == 1
    j = d % 16
    upper = ((d % 32) // 16) == 1
    inv_freq = ROPE_BASE ** (-j.astype(F32) / 16.0)
    t = jnp.arange(DEC_SEQ)
    row = (t // GRID_W).astype(F32)
    col = (t % GRID_W).astype(F32)
    pos = jnp.where(use_col[None, :], col[:, None], row[:, None])
    ang = pos * inv_freq[None, :]
    cos = jnp.cos(ang)
    sin = jnp.where(upper[None, :], jnp.sin(ang), -jnp.sin(ang))
    cos = jnp.concatenate([jnp.ones((TM, LANE), F32), cos], axis=0)
    sin = jnp.concatenate([jnp.zeros((TM, LANE), F32), sin], axis=0)
    return cos, sin


def _route(logits):
    lane = lax.broadcasted_iota(jnp.int32, logits.shape, 1)
    lanef = lane.astype(F32)
    neg = jnp.float32(-jnp.inf)
    gl = jnp.where(lane < N_GROUPS, logits, neg)
    gmax = jnp.max(gl, axis=1, keepdims=True)
    gsel = jnp.min(jnp.where(gl == gmax, lanef, float(LANE)), axis=1, keepdims=True)
    p_grp = 1.0 / jnp.sum(jnp.exp(gl - gmax), axis=1, keepdims=True)
    lo = float(N_GROUPS) + gsel * float(E_PER_GROUP)
    el = jnp.where((lanef >= lo) & (lanef < lo + float(E_PER_GROUP)), logits, neg)
    v1 = jnp.max(el, axis=1, keepdims=True)
    i1 = jnp.min(jnp.where(el == v1, lanef, float(LANE)), axis=1, keepdims=True)
    el2 = jnp.where(lanef == i1, neg, el)
    v2 = jnp.max(el2, axis=1, keepdims=True)
    i2 = jnp.min(jnp.where(el2 == v2, lanef, float(LANE)), axis=1, keepdims=True)
    t = jnp.exp(v2 - v1)
    w1 = p_grp / (1.0 + t)
    w2 = p_grp * t / (1.0 + t)
    out = jnp.where(lane == 0, i1 - float(N_GROUPS), 0.0)
    out = jnp.where(lane == 1, i2 - float(N_GROUPS), out)
    out = jnp.where(lane == 2, w1, out)
    out = jnp.where(lane == 3, w2, out)
    return out


def _outproj_kernel(*refs, widths):
    n_in = len(widths)
    o_refs = refs[:n_in]
    w_ref, x_ref, mod_ref, nw_ref, wr_ref, xn_ref, h2_ref, rt_ref = refs[n_in:]
    y = None
    c0 = 0
    for o_ref, wd in zip(o_refs, widths):
        part = jnp.dot(o_ref[...].astype(BF16), w_ref[c0:c0 + wd, :], preferred_element_type=F32)
        y = part if y is None else y + part
        c0 += wd
    xn = x_ref[...] + mod_ref[0, 2:3, :] * y
    xn_ref[...] = xn
    h2 = _rms(xn, nw_ref[...]) * (1.0 + mod_ref[0, 4:5, :]) + mod_ref[0, 3:4, :]
    h2_ref[...] = h2
    rt_ref[...] = _route(_dot_f32(h2, wr_ref[...]))


def _outproj(parts, w_bf16, x, mod, norm_w2, w_router):
    widths = tuple(p.shape[1] for p in parts)
    in_specs = [pl.BlockSpec((TM, wd), lambda t: (t, 0)) for wd in widths]
    in_specs += [pl.BlockSpec((D_MODEL, D_MODEL), lambda t: (0, 0)),
                 pl.BlockSpec((TM, D_MODEL), lambda t: (t, 0)),
                 pl.BlockSpec((1, MOD_ROWS, D_MODEL), lambda t: (_mod_index(t), 0, 0)),
                 pl.BlockSpec((1, D_MODEL), lambda t: (0, 0)),
                 pl.BlockSpec((D_MODEL, LANE), lambda t: (0, 0))]
    return pl.pallas_call(
        functools.partial(_outproj_kernel, widths=widths),
        out_shape=(jax.ShapeDtypeStruct((N_TOK, D_MODEL), F32),
                   jax.ShapeDtypeStruct((N_TOK, D_MODEL), F32),
                   jax.ShapeDtypeStruct((N_TOK, LANE), F32)),
        grid=(N_TILES,),
        in_specs=in_specs,
        out_specs=(pl.BlockSpec((TM, D_MODEL), lambda t: (t, 0)),
                   pl.BlockSpec((TM, D_MODEL), lambda t: (t, 0)),
                   pl.BlockSpec((TM, LANE), lambda t: (t, 0))),
        compiler_params=_params(("parallel",)),
        name="outproj",
    )(*parts, w_bf16, x, mod, norm_w2.reshape(1, D_MODEL), w_router)


def _expert_kernel(blk_e_ref, nact_ref, x_ref, wg_ref, wu_ref, wd_ref, y_ref, wg_s, wu_s, wd_s):
    i = pl.program_id(0)
    e = blk_e_ref[i]
    prev = blk_e_ref[jnp.maximum(i - 1, 0)]

    @pl.when((i == 0) | (e != prev))
    def _():
        wg_s[...] = wg_ref[...].astype(BF16)
        wu_s[...] = wu_ref[...].astype(BF16)
        wd_s[...] = wd_ref[...].astype(BF16)

    @pl.when(i < nact_ref[0])
    def _():
        x = x_ref[...].astype(BF16)
        g = jnp.dot(x, wg_s[...], preferred_element_type=F32)
        u = jnp.dot(x, wu_s[...], preferred_element_type=F32)
        y_ref[...] = jnp.dot((_silu(g) * u).astype(BF16), wd_s[...], preferred_element_type=F32)

    @pl.when(i >= nact_ref[0])
    def _():
        y_ref[...] = jnp.zeros_like(y_ref)


def _experts(xb, blk_e, n_active, w_gate, w_up, w_down):
    n_rows = xb.shape[0]
    n_blocks = n_rows // MOE_ROWS
    grid_spec = pltpu.PrefetchScalarGridSpec(
        num_scalar_prefetch=2,
        grid=(n_blocks,),
        in_specs=[pl.BlockSpec((MOE_ROWS, D_MODEL), lambda i, be, na: (i, 0)),
                  pl.BlockSpec((None, D_MODEL, D_EXPERT), lambda i, be, na: (be[i], 0, 0)),
                  pl.BlockSpec((None, D_MODEL, D_EXPERT), lambda i, be, na: (be[i], 0, 0)),
                  pl.BlockSpec((None, D_EXPERT, D_MODEL), lambda i, be, na: (be[i], 0, 0))],
        out_specs=pl.BlockSpec((MOE_ROWS, D_MODEL), lambda i, be, na: (i, 0)),
        scratch_shapes=[pltpu.VMEM((D_MODEL, D_EXPERT), BF16),
                        pltpu.VMEM((D_MODEL, D_EXPERT), BF16),
                        pltpu.VMEM((D_EXPERT, D_MODEL), BF16)])
    return pl.pallas_call(
        _expert_kernel,
        out_shape=jax.ShapeDtypeStruct((n_rows, D_MODEL), F32),
        grid_spec=grid_spec,
        compiler_params=_params(("arbitrary",)),
        name="experts",
    )(blk_e, n_active, xb, w_gate, w_up, w_down)


def _combine_kernel(x_ref, y0_ref, y1_ref, rt_ref, mod_ref, o_ref):
    rt = rt_ref[...]
    y = rt[:, 2:3] * y0_ref[...] + rt[:, 3:4] * y1_ref[...]
    o_ref[...] = x_ref[...] + mod_ref[0, 5:6, :] * y


def _combine(x, y0, y1, route, mod):
    row = pl.BlockSpec((TM, D_MODEL), lambda t: (t, 0))
    return pl.pallas_call(
        _combine_kernel,
        out_shape=jax.ShapeDtypeStruct((N_TOK, D_MODEL), F32),
        grid=(N_TILES,),
        in_specs=[row, row, row,
                  pl.BlockSpec((TM, LANE), lambda t: (t, 0)),
                  pl.BlockSpec((1, MOD_ROWS, D_MODEL), lambda t: (_mod_index(t), 0, 0))],
        out_specs=row,
        compiler_params=_params(("parallel",)),
        name="combine",
    )(x, y0, y1, route, mod)


def _moe(x, h2, route, mod, w_gate, w_up, w_down):
    n_asg = 2 * N_TOK
    flat_e = route[:, :2].astype(jnp.int32).reshape(n_asg)
    flat_tok = jnp.arange(n_asg, dtype=jnp.int32) // 2
    onehot = (flat_e[:, None] == jnp.arange(N_EXPERTS, dtype=jnp.int32)[None, :]).astype(jnp.int32)
    csum = jnp.cumsum(onehot, axis=0)
    rank = jnp.take_along_axis(csum, flat_e[:, None], axis=1)[:, 0] - 1
    counts = csum[-1]
    padded = (counts + MOE_ROWS - 1) // MOE_ROWS * MOE_ROWS
    pad_end = jnp.cumsum(padded)
    pad_start = pad_end - padded
    dest = pad_start[flat_e] + rank
    n_rows = n_asg + N_EXPERTS * MOE_ROWS
    n_blocks = n_rows // MOE_ROWS
    row_tok = jnp.zeros((n_rows,), jnp.int32).at[dest].set(flat_tok)
    blk_start = jnp.arange(n_blocks, dtype=jnp.int32) * MOE_ROWS
    blk_e = jnp.minimum(jnp.searchsorted(pad_end, blk_start, side='right'), N_EXPERTS - 1).astype(jnp.int32)
    n_active = (pad_end[-1:] // MOE_ROWS).astype(jnp.int32)
    xb = jnp.take(h2, row_tok, axis=0)
    yb = _experts(xb, blk_e, n_active, w_gate, w_up, w_down)
    dest2 = dest.reshape(N_TOK, 2)
    y0 = jnp.take(yb, dest2[:, 0], axis=0)
    y1 = jnp.take(yb, dest2[:, 1], axis=0)
    return _combine(x, y0, y1, route, mod)


def _final_kernel(x_ref, w_ref, o_ref):
    o_ref[...] = _rms(x_ref[...], w_ref[...])


def _final_norm(x, w, tile0, n_rows):
    return pl.pallas_call(
        _final_kernel,
        out_shape=jax.ShapeDtypeStruct((n_rows, D_MODEL), F32),
        grid=(n_rows // TM,),
        in_specs=[pl.BlockSpec((TM, D_MODEL), lambda t: (t + tile0, 0)),
                  pl.BlockSpec((1, D_MODEL), lambda t: (0, 0))],
        out_specs=pl.BlockSpec((TM, D_MODEL), lambda t: (t, 0)),
        compiler_params=_params(("parallel",)),
        name="final_norm",
    )(x, w.reshape(1, D_MODEL))


def _hgrn_kernel(*refs, seq_len, layer, context):
    if context:
        (q_ref, ff_ref, fb_ref, v_ref, g_ref, lbl_ref, nw_ref,
         o_ref, sfin_ref, of_s, ob_s, sf_s, sb_s) = refs
    else:
        (q_ref, ff_ref, fb_ref, v_ref, g_ref, lbl_ref, nw_ref, s0_ref, _,
         o_ref, of_s, ob_s, sf_s, sb_s) = refs
    c = CHUNK_A
    n = seq_len // c

    def lower_bound(d):
        z = lbl_ref[d]
        e = jnp.exp(z - jnp.max(z, axis=0, keepdims=True))
        return jnp.sum(e[:layer + 1], axis=0, keepdims=True) / jnp.sum(e, axis=0, keepdims=True)

    lb_f = lower_bound(0)
    lb_b = lower_bound(1)
    row = lax.broadcasted_iota(jnp.int32, (c, c), 0)
    col = lax.broadcasted_iota(jnp.int32, (c, c), 1)
    causal = row >= col
    tri_f = causal.astype(F32)
    tri_b = (row <= col).astype(F32)

    if context:
        sf_s[...] = jnp.zeros_like(sf_s)
        sb_s[...] = jnp.zeros_like(sb_s)
    else:
        sf_s[...] = s0_ref[0, 0, 0].T
        sb_s[...] = s0_ref[0, 1, 0].T

    def chunk(r0, f_ref, lb, tri, mask, last, st_ref, out_s):
        rows = pl.ds(r0, c)
        f = lb + (1.0 - lb) * _sigmoid(f_ref[rows, :])
        b = _tri_matmul(tri, jnp.log(f))
        b_last = b[last:last + 1, :]
        q_in = _silu(q_ref[rows, :]) * jnp.exp(b)
        k = 1.0 - f
        v = v_ref[rows, :]
        a = jnp.where(mask, _dot_nt(q_in, k * jnp.exp(-b)), 0.0)
        st = st_ref[...]
        out_s[rows, :] = _dot(a, v) + _dot_nt(q_in, st)
        st_ref[...] = st * jnp.exp(b_last) + _dot_tn(v, k * jnp.exp(b_last - b))

    def body(i, carry):
        chunk(pl.multiple_of(i * c, c), ff_ref, lb_f, tri_f, causal, c - 1, sf_s, of_s)
        chunk(pl.multiple_of((n - 1 - i) * c, c), fb_ref, lb_b, tri_b, row <= col, 0, sb_s, ob_s)
        return carry

    lax.fori_loop(0, n, body, 0, unroll=4)

    if context:
        sfin_ref[0, 0, 0] = sf_s[...].T
        sfin_ref[0, 1, 0] = sb_s[...].T

    nw = nw_ref[...]

    def epilogue(j, carry):
        rows = pl.ds(pl.multiple_of(j * TM, TM), TM)
        o = of_s[rows, :] + ob_s[rows, :]
        o_ref[rows, :] = _rms(o, nw) * _silu(g_ref[rows, :])
        return carry

    lax.fori_loop(0, seq_len // TM, epilogue, 0)


def _hgrn(proj, lb_logits, norm_w, layer, context, s0=None, o_prev=None):
    seq_len = SEQ if context else DEC_SEQ
    bsz = BATCH if context else DEC_BATCH
    row0 = 0 if context else N_PROMPT // seq_len

    def col(k):
        return pl.BlockSpec((seq_len, LANE), lambda b, h: (row0 + b, k * H_A + h))

    in_specs = [col(0), col(1), col(2), col(3), col(4),
                pl.BlockSpec((2, DEPTH + 1, LANE), lambda b, h: (0, 0, h)),
                pl.BlockSpec((1, LANE), lambda b, h: (0, 0))]
    args = [proj, proj, proj, proj, proj, lb_logits, norm_w.reshape(1, LANE)]
    o_shape = jax.ShapeDtypeStruct((N_TOK, W_A), F32)
    o_spec = pl.BlockSpec((seq_len, LANE), lambda b, h: (row0 + b, h))
    scratch = [pltpu.VMEM((seq_len, LANE), F32), pltpu.VMEM((seq_len, LANE), F32),
               pltpu.VMEM((LANE, LANE), F32), pltpu.VMEM((LANE, LANE), F32)]
    kern = functools.partial(_hgrn_kernel, seq_len=seq_len, layer=layer, context=context)
    if context:
        return pl.pallas_call(
            kern,
            out_shape=(o_shape, jax.ShapeDtypeStruct((bsz, 2, H_A, DK_A, DK_A), F32)),
            grid=(bsz, H_A),
            in_specs=in_specs,
            out_specs=(o_spec, pl.BlockSpec((1, 2, 1, DK_A, DK_A), lambda b, h: (b, 0, h, 0, 0))),
            scratch_shapes=scratch,
            compiler_params=_params(("parallel", "parallel")),
            name="hgrn_ctx",
        )(*args)
    in_specs += [pl.BlockSpec((1, 2, 1, DK_A, DK_A), lambda b, h: (b, 0, h, 0, 0)),
                 pl.BlockSpec(memory_space=pl.ANY)]
    args += [s0, o_prev]
    return pl.pallas_call(
        kern,
        out_shape=o_shape,
        grid=(bsz, H_A),
        in_specs=in_specs,
        out_specs=o_spec,
        scratch_shapes=scratch,
        input_output_aliases={len(args) - 1: 0},
        compiler_params=_params(("parallel", "parallel")),
        name="hgrn_dec",
    )(*args)


ATT_TQ = 256
COL_QD = 5 * W_A // LANE
COL_KD = COL_QD + QK_B // LANE
COL_VD = COL_KD + QK_B // LANE


def _attn_kernel(*refs, seq_len, layer, context):
    if context:
        q_ref, k_ref, v_ref, lam_ref, nw_ref, o_ref, k_s, v_s = refs
    else:
        q_ref, k_ref, v_ref, ck_ref, cv_ref, lam_ref, nw_ref, _, o_ref, k_s, v_s = refs

    @pl.when(pl.program_id(2) == 0)
    def _():
        k_s[0:seq_len, :] = k_ref[...].astype(BF16)
        v_s[0:seq_len, :] = v_ref[...].astype(BF16)
        if not context:
            k_s[seq_len:, :] = ck_ref[0].astype(BF16)
            v_s[seq_len:, :] = cv_ref[0].astype(BF16)

    lam_init = 0.8 - 0.6 * math.exp(-0.3 * layer)
    lp = lam_ref[...]
    lam = (jnp.exp(jnp.sum(lp[0:1] * lp[1:2], axis=1, keepdims=True))
           - jnp.exp(jnp.sum(lp[2:3] * lp[3:4], axis=1, keepdims=True)) + lam_init)

    q = q_ref[...] * (DH_B ** -0.5)
    lane = lax.broadcasted_iota(jnp.int32, q.shape, 1)
    k = k_s[...]

    def softmax_map(first):
        s = _dot_nt(jnp.where((lane < DH_B) == first, q, 0.0), k)
        p = jnp.exp(s - jnp.max(s, axis=1, keepdims=True))
        return p, jnp.sum(p, axis=1, keepdims=True)

    p0, l0 = softmax_map(True)
    p1, l1 = softmax_map(False)
    a = p0 - (lam * l0 / l1) * p1
    o = jnp.dot(a.astype(BF16), v_s[...], preferred_element_type=F32) / l0
    o_ref[...] = _rms(o, nw_ref[...]) * (1.0 - lam_init)


def _attn(proj, lam_p, norm_w, layer, context, cache_k=None, cache_v=None, o_prev=None):
    seq_len = SEQ if context else DEC_SEQ
    bsz = BATCH if context else DEC_BATCH
    row0 = 0 if context else N_PROMPT // seq_len
    nq = seq_len // ATT_TQ
    tile0 = row0 * nq
    t_k = seq_len if context else seq_len + PAST_LEN
    in_specs = [pl.BlockSpec((ATT_TQ, LANE), lambda b, h, i: (tile0 + b * nq + i, COL_QD + h)),
                pl.BlockSpec((seq_len, LANE), lambda b, h, i: (row0 + b, COL_KD + h)),
                pl.BlockSpec((seq_len, LANE), lambda b, h, i: (row0 + b, COL_VD + h))]
    args = [proj, proj, proj]
    if not context:
        in_specs += [pl.BlockSpec((1, PAST_LEN, LANE), lambda b, h, i: (b, 0, h)),
                     pl.BlockSpec((1, PAST_LEN, LANE), lambda b, h, i: (b, 0, h))]
        args += [cache_k, cache_v]
    in_specs += [pl.BlockSpec((4, DH_B), lambda b, h, i: (0, 0)),
                 pl.BlockSpec((1, LANE), lambda b, h, i: (0, 0))]
    args += [lam_p, norm_w.reshape(1, LANE)]
    aliases = {}
    if not context:
        in_specs += [pl.BlockSpec(memory_space=pl.ANY)]
        args += [o_prev]
        aliases = {len(args) - 1: 0}
    return pl.pallas_call(
        functools.partial(_attn_kernel, seq_len=seq_len, layer=layer, context=context),
        out_shape=jax.ShapeDtypeStruct((N_TOK, W_B), F32),
        grid=(bsz, H_B, nq),
        in_specs=in_specs,
        out_specs=pl.BlockSpec((ATT_TQ, LANE), lambda b, h, i: (tile0 + b * nq + i, h)),
        scratch_shapes=[pltpu.VMEM((t_k, LANE), BF16), pltpu.VMEM((t_k, LANE), BF16)],
        input_output_aliases=aliases,
        compiler_params=_params(("parallel", "parallel", "arbitrary")),
        name="attn_ctx" if context else "attn_dec",
    )(*args)


COL_GATES = 4 * W_C // LANE


def _gdn_gates_kernel(g_ref, alog_ref, dtb_ref, col_ref, row_ref):
    c = GDN_CHUNK
    raw = g_ref[...]
    z = raw + dtb_ref[...]
    softplus = jnp.maximum(z, 0.0) + jnp.log(1.0 + jnp.exp(-jnp.abs(z)))
    g = -jnp.exp(alog_ref[...]) * softplus
    row = lax.broadcasted_iota(jnp.int32, (c, c), 0)
    col = lax.broadcasted_iota(jnp.int32, (c, c), 1)
    lane = lax.broadcasted_iota(jnp.int32, (c, LANE), 1)
    prefix = _tri_matmul((row >= col).astype(F32), g)
    suffix = _tri_matmul((row <= col).astype(F32), g)
    out = jnp.where(lane < H_C, prefix, jnp.where(lane < 2 * H_C, suffix, _sigmoid(raw)))
    col_ref[...] = out
    row_ref[0] = out.T


def _gdn_gates(proj, alog_lane, dtb_lane):
    c = GDN_CHUNK
    n = N_TOK // c
    return pl.pallas_call(
        _gdn_gates_kernel,
        out_shape=(jax.ShapeDtypeStruct((N_TOK, LANE), F32), jax.ShapeDtypeStruct((n, LANE, c), F32)),
        grid=(n,),
        in_specs=[pl.BlockSpec((c, LANE), lambda i: (i, COL_GATES)),
                  pl.BlockSpec((1, LANE), lambda i: (0, 0)),
                  pl.BlockSpec((1, LANE), lambda i: (0, 0))],
        out_specs=(pl.BlockSpec((c, LANE), lambda i: (i, 0)),
                   pl.BlockSpec((1, LANE, c), lambda i: (i, 0, 0))),
        compiler_params=_params(("parallel",)),
        name="gdn_gates",
    )(proj, alog_lane, dtb_lane)


def _unit_tri_inverse_pair(m, row, col):
    c = m.shape[0]
    left = lax.broadcasted_iota(jnp.int32, m.shape, 1) < c

    def mm(x, y):
        yb = y.astype(BF16)
        zero = jnp.zeros_like(yb)
        blockdiag = jnp.concatenate([jnp.where(left, yb, zero), jnp.where(left, zero, yb)], axis=0)
        return jnp.dot(x.astype(BF16), blockdiag, preferred_element_type=F32)

    eye = (row == col).astype(F32)
    a = jnp.where((row // 16) == (col // 16), m, 0.0)
    a2 = mm(a, a)
    a4 = mm(a2, a2)
    a8 = mm(a4, a4)
    t = eye - a
    t = t + mm(t, a2)
    t = t + mm(t, a4)
    t = t + mm(t, a8)
    blk = 32
    while blk <= c:
        off = ((row // blk) == (col // blk)) & ((row // (blk // 2)) != (col // (blk // 2)))
        t = t - mm(mm(t, jnp.where(off, m, 0.0)), t)
        blk *= 2
    return t


def _gdn_kernel(*refs, seq_len, context):
    if context:
        (q_ref, k_ref, v_ref, go_ref, cwq_ref, cwk_ref, cwv_ref, gcol_ref, grow_ref, nw_ref,
         o_ref, sfin_ref, u_s, wq_s, qk_s, kd_s, of_s, ob_s, sf_s, sb_s) = refs
    else:
        (q_ref, k_ref, v_ref, go_ref, cwq_ref, cwk_ref, cwv_ref, gcol_ref, grow_ref, nw_ref, s0_ref, _,
         o_ref, u_s, wq_s, qk_s, kd_s, of_s, ob_s, sf_s, sb_s) = refs
    c = GDN_CHUNK
    n = seq_len // c
    head = pl.program_id(1)
    row = lax.broadcasted_iota(jnp.int32, (c, c), 0)
    col = lax.broadcasted_iota(jnp.int32, (c, c), 1)
    row2 = lax.broadcasted_iota(jnp.int32, (c, 2 * c), 0)
    col2 = lax.broadcasted_iota(jnp.int32, (c, 2 * c), 1) & (c - 1)
    lane = lax.broadcasted_iota(jnp.int32, (c, LANE), 1)
    rowi = lax.broadcasted_iota(jnp.int32, (c, LANE), 0)

    def gc_row_last(ci, d):
        gc_row = grow_ref[ci, pl.ds(d * H_C + head, 1), :]
        return gc_row, (gc_row[:, c - 1:c] if d == 0 else gc_row[:, 0:1])

    def prepare(ci, carry):
        r0 = pl.multiple_of(ci * c, c)
        rows = pl.ds(r0, c)

        def conv(x_ref, w_ref):
            cur = x_ref[rows, :]
            before = x_ref[pl.ds(pl.multiple_of(jnp.maximum(r0 - 8, 0), 8), 8), :]
            after = x_ref[pl.ds(pl.multiple_of(jnp.minimum(r0 + c, seq_len - 8), 8), 8), :]
            prev_row = jnp.where(ci > 0, before[7:8, :], 0.0)
            next_row = jnp.where(ci < n - 1, after[0:1, :], 0.0)
            xm1 = jnp.where(rowi == 0, prev_row, pltpu.roll(cur, 1, 0))
            xp1 = jnp.where(rowi == c - 1, next_row, pltpu.roll(cur, c - 1, 0))
            w = w_ref[...]
            return _silu(xm1 * w[0:1, :] + cur * w[1:2, :] + xp1 * w[2:3, :])

        q = conv(q_ref, cwq_ref)
        k = conv(k_ref, cwk_ref)
        vn = conv(v_ref, cwv_ref)
        qn = q * lax.rsqrt(jnp.sum(q * q, axis=1, keepdims=True) + EPS) * (DK_C ** -0.5)
        kn = k * lax.rsqrt(jnp.sum(k * k, axis=1, keepdims=True) + EPS)
        kq = _dot_nt(jnp.concatenate([kn, qn], axis=0), kn)
        gates = gcol_ref[rows, :]
        per_dir = []
        for d in (0, 1):
            gc = jnp.sum(jnp.where(lane == d * H_C + head, gates, 0.0), axis=1, keepdims=True)
            beta = jnp.sum(jnp.where(lane == (2 + d) * H_C + head, gates, 0.0), axis=1, keepdims=True)
            gc_row, gc_last = gc_row_last(ci, d)
            incl = (row >= col) if d == 0 else (row <= col)
            strict = (row > col) if d == 0 else (row < col)
            decay = jnp.where(incl, jnp.exp(gc - gc_row), 0.0)
            m = jnp.where(strict, kq[:c] * beta * decay, 0.0)
            qk_s[d, rows, :] = (kq[c:] * decay).astype(BF16)
            kd_s[d, rows, :] = (kn * jnp.exp(gc_last - gc)).astype(BF16)
            per_dir.append((m, gc, beta))
        t = _unit_tri_inverse_pair(jnp.concatenate([per_dir[0][0], per_dir[1][0]], axis=1), row2, col2)
        base = pl.multiple_of(ci * 2 * c, 2 * c)
        for d in (0, 1):
            _, gc, beta = per_dir[d]
            e = jnp.exp(gc)
            uw = _dot(t[:, d * c:(d + 1) * c], jnp.concatenate([vn * beta, kn * (beta * e)], axis=1))
            u_s[d, rows, :] = uw[:, :LANE]
            wq_s[d, pl.ds(base, c), :] = uw[:, LANE:].astype(BF16)
            wq_s[d, pl.ds(base + c, c), :] = (qn * e).astype(BF16)
        return carry

    lax.fori_loop(0, n, prepare, 0, unroll=2)

    if context:
        sf_s[...] = jnp.zeros_like(sf_s)
        sb_s[...] = jnp.zeros_like(sb_s)
    else:
        sf_s[...] = s0_ref[0, 0, 0]
        sb_s[...] = s0_ref[0, 1, 0]

    def advance(ci, d, s_ref, out_s):
        rows = pl.ds(pl.multiple_of(ci * c, c), c)
        _, gc_last = gc_row_last(ci, d)
        s = s_ref[...]
        ws = jnp.dot(wq_s[d, pl.ds(pl.multiple_of(ci * 2 * c, 2 * c), 2 * c), :], s.astype(BF16),
                     preferred_element_type=F32)
        v_new = (u_s[d, rows, :] - ws[:c]).astype(BF16)
        out_s[rows, :] = ws[c:] + jnp.dot(qk_s[d, rows, :], v_new, preferred_element_type=F32)
        s_ref[...] = s * jnp.exp(gc_last) + lax.dot_general(
            kd_s[d, rows, :], v_new, (((0,), (0,)), ((), ())), preferred_element_type=F32)

    def body(i, carry):
        advance(i, 0, sf_s, of_s)
        advance(n - 1 - i, 1, sb_s, ob_s)
        return carry

    lax.fori_loop(0, n, body, 0)

    if context:
        sfin_ref[0, 0, 0] = sf_s[...]
        sfin_ref[0, 1, 0] = sb_s[...]

    nw = nw_ref[...]

    def epilogue(j, carry):
        rows = pl.ds(pl.multiple_of(j * TM, TM), TM)
        o = of_s[rows, :] + ob_s[rows, :]
        o_ref[rows, :] = _rms(o, nw) * _silu(go_ref[rows, :])
        return carry

    lax.fori_loop(0, seq_len // TM, epilogue, 0)


def _gdn(proj, conv_w, gcol, grow, norm_w, context, s0=None, o_prev=None):
    seq_len = SEQ if context else DEC_SEQ
    bsz = BATCH if context else DEC_BATCH
    row0 = 0 if context else N_PROMPT // seq_len
    nc = seq_len // GDN_CHUNK

    def col(k):
        return pl.BlockSpec((seq_len, LANE), lambda b, h: (row0 + b, k * H_C + h))

    def cw(k):
        return pl.BlockSpec((3, LANE), lambda b, h: (0, k * H_C + h))

    in_specs = [col(0), col(1), col(2), col(3), cw(0), cw(1), cw(2),
                pl.BlockSpec((seq_len, LANE), lambda b, h: (row0 + b, 0)),
                pl.BlockSpec((nc, LANE, GDN_CHUNK), lambda b, h: (row0 + b, 0, 0)),
                pl.BlockSpec((1, LANE), lambda b, h: (0, 0))]
    args = [proj, proj, proj, proj, conv_w, conv_w, conv_w, gcol, grow, norm_w.reshape(1, LANE)]
    o_shape = jax.ShapeDtypeStruct((N_TOK, W_C), F32)
    o_spec = pl.BlockSpec((seq_len, LANE), lambda b, h: (row0 + b, h))
    seq_buf = pltpu.VMEM((seq_len, LANE), F32)
    state_buf = pltpu.VMEM((DK_C, DK_C), F32)
    dir_bf16 = pltpu.VMEM((2, seq_len, LANE), BF16)
    scratch = [pltpu.VMEM((2, seq_len, LANE), F32), pltpu.VMEM((2, 2 * seq_len, LANE), BF16), dir_bf16, dir_bf16,
               seq_buf, seq_buf, state_buf, state_buf]
    kern = functools.partial(_gdn_kernel, seq_len=seq_len, context=context)
    state_spec = pl.BlockSpec((1, 2, 1, DK_C, DK_C), lambda b, h: (b, 0, h, 0, 0))
    if context:
        return pl.pallas_call(
            kern,
            out_shape=(o_shape, jax.ShapeDtypeStruct((bsz, 2, H_C, DK_C, DK_C), F32)),
            grid=(bsz, H_C),
            in_specs=in_specs,
            out_specs=(o_spec, state_spec),
            scratch_shapes=scratch,
            compiler_params=_params(("parallel", "parallel")),
            name="gdn_ctx",
        )(*args)
    in_specs += [state_spec, pl.BlockSpec(memory_space=pl.ANY)]
    args += [s0, o_prev]
    return pl.pallas_call(
        kern,
        out_shape=o_shape,
        grid=(bsz, H_C),
        in_specs=in_specs,
        out_specs=o_spec,
        scratch_shapes=scratch,
        input_output_aliases={len(args) - 1: 0},
        compiler_params=_params(("parallel", "parallel")),
        name="gdn_dec",
    )(*args)


def kernel(x_prompt, x_sample, c, cache_diff_k, cache_diff_v, state_hgrn, state_gdn, c_ctx,
           ada_w, ada_b, norm_w, final_norm_w, w_in_ab, hgrn_lb_logits, hgrn_norm_w,
           diff_lambda, diff_norm_w, w_in_c, gdn_conv_w, gdn_a_log, gdn_dt_bias, gdn_norm_w,
           w_out, moe_router_group, moe_router_expert, moe_w_gate, moe_w_up, moe_w_down):
    x = jnp.concatenate([x_prompt.reshape(N_PROMPT, D_MODEL), x_sample.reshape(N_SAMPLE, D_MODEL)], axis=0)
    cvec = jnp.concatenate([c_ctx[None, :], c, jnp.zeros((MOD_ROWS - N_MOD, D_MODEL), F32)], axis=0)
    mod_all = _adaln(cvec, ada_w, ada_b)[:, :N_MOD].reshape(DEPTH, N_MOD, 6, D_MODEL)
    mod_all = jnp.pad(mod_all, ((0, 0), (0, 0), (0, MOD_ROWS - 6), (0, 0)))
    cos, sin = _rope_tables()
    new_k, new_v, new_hgrn, new_gdn = [], [], [], []
    for l in range(DEPTH):
        i = l // 2
        mod = mod_all[l]
        if l % 2 == 0:
            proj = _inproj(x, mod, norm_w[l, 0], w_in_ab[i].astype(BF16),
                           rope=(cos, sin, (COL_QD * LANE, COL_VD * LANE)))
            o_h, s_h = _hgrn(proj, hgrn_lb_logits, hgrn_norm_w[i], l, True)
            o_h = _hgrn(proj, hgrn_lb_logits, hgrn_norm_w[i], l, False, s0=state_hgrn[:, i], o_prev=o_h)
            o_d = _attn(proj, diff_lambda[i], diff_norm_w[i], l, True)
            o_d = _attn(proj, diff_lambda[i], diff_norm_w[i], l, False,
                        cache_k=cache_diff_k[:, i].reshape(DEC_BATCH, PAST_LEN, QK_B),
                        cache_v=cache_diff_v[:, i].reshape(DEC_BATCH, PAST_LEN, W_B), o_prev=o_d)
            parts = (o_h, o_d)
            new_k.append(proj[:N_PROMPT, COL_KD * LANE:COL_VD * LANE].reshape(BATCH, SEQ, H_B, 2, DH_B))
            new_v.append(proj[:N_PROMPT, COL_VD * LANE:].reshape(BATCH, SEQ, H_B, DV_B))
            new_hgrn.append(s_h)
        else:
            w_c = jnp.pad(w_in_c[i], ((0, 0), (0, (COL_GATES + 1) * LANE - IN_C))).astype(BF16)
            proj = _inproj(x, mod, norm_w[l, 0], w_c)
            pad = jnp.zeros((LANE - 2 * H_C,), F32)
            alog_lane = jnp.concatenate([gdn_a_log[i, 0], gdn_a_log[i, 1], pad]).reshape(1, LANE)
            dtb_lane = jnp.concatenate([gdn_dt_bias[i, 0], gdn_dt_bias[i, 1], pad]).reshape(1, LANE)
            gcol, grow = _gdn_gates(proj, alog_lane, dtb_lane)
            o_c, s_c = _gdn(proj, gdn_conv_w[i], gcol, grow, gdn_norm_w[i], True)
            o_c = _gdn(proj, gdn_conv_w[i], gcol, grow, gdn_norm_w[i], False, s0=state_gdn[:, i], o_prev=o_c)
            parts = (o_c,)
            new_gdn.append(s_c)
        w_router = jnp.concatenate(
            [moe_router_group[l], moe_router_expert[l],
             jnp.zeros((D_MODEL, LANE - N_GROUPS - N_EXPERTS), F32)], axis=1)
        x, h2, route = _outproj(parts, w_out[l].astype(BF16), x, mod, norm_w[l, 1], w_router)
        x = _moe(x, h2, route, mod, moe_w_gate[l], moe_w_up[l], moe_w_down[l])
    y_prompt = _final_norm(x, final_norm_w, 0, N_PROMPT).reshape(BATCH, SEQ, D_MODEL)
    y_sample = _final_norm(x, final_norm_w, PROMPT_TILES, N_SAMPLE).reshape(DEC_BATCH, DEC_SEQ, D_MODEL)
    return (y_prompt, y_sample, jnp.stack(new_k, axis=1), jnp.stack(new_v, axis=1),
            jnp.stack(new_hgrn, axis=1), jnp.stack(new_gdn, axis=1))
```

```python
import functools
import math

import jax
import jax.numpy as jnp
from jax import lax
from jax.experimental import pallas as pl
from jax.experimental.pallas import tpu as pltpu

F32 = jnp.float32
BF16 = jnp.bfloat16
HIGHEST = lax.Precision.HIGHEST

D_MODEL = 1024
BATCH = 16
SEQ = 256
DEPTH = 2
DEC_BATCH = 4
DEC_SEQ = 4096
PAST_LEN = 256
GRID_W = 64
H_A = 4
DK_A = 128
W_A = 512
CHUNK_A = 32
H_B = 4
DH_B = 64
DV_B = 128
QK_B = 512
W_B = 512
ROPE_BASE = 10000.0
H_C = 8
DK_C = 128
W_C = 1024
N_GROUPS = 4
E_PER_GROUP = 8
N_EXPERTS = 32
D_EXPERT = 512
EPS = 1e-6
IN_AB = 5 * W_A + 2 * QK_B + W_B
IN_C = 4 * W_C + 4 * H_C

LANE = 128
N_PROMPT = BATCH * SEQ
N_SAMPLE = DEC_BATCH * DEC_SEQ
N_TOK = N_PROMPT + N_SAMPLE
TM = 256
N_TILES = N_TOK // TM
PROMPT_TILES = N_PROMPT // TM
TILES_PER_SAMPLE = DEC_SEQ // TM
N_MOD = 1 + DEC_BATCH
MOD_ROWS = 8
MOE_ROWS = 256
GDN_CHUNK = 128
VMEM_LIMIT = 56 * 1024 * 1024


def _mod_index(t):
    return jnp.where(t < PROMPT_TILES, 0, 1 + (t - PROMPT_TILES) // TILES_PER_SAMPLE)


def _sigmoid(x):
    return 1.0 / (1.0 + jnp.exp(-x))


def _silu(x):
    return x * _sigmoid(x)


def _rms(x, w):
    return x * lax.rsqrt(jnp.mean(x * x, axis=-1, keepdims=True) + EPS) * w


def _dot(a, b):
    return jnp.dot(a.astype(BF16), b.astype(BF16), preferred_element_type=F32)


def _dot_nt(a, b):
    return lax.dot_general(a.astype(BF16), b.astype(BF16), (((1,), (1,)), ((), ())),
                           preferred_element_type=F32)


def _dot_tn(a, b):
    return lax.dot_general(a.astype(BF16), b.astype(BF16), (((0,), (0,)), ((), ())),
                           preferred_element_type=F32)


def _dot_f32(a, b):
    return jnp.dot(a, b, precision=HIGHEST, preferred_element_type=F32)


def _tri_matmul(tri, x):
    hi = x.astype(BF16)
    rem = x - hi.astype(F32)
    mid = rem.astype(BF16)
    lo = (rem - mid.astype(F32)).astype(BF16)
    n = x.shape[1]
    r = jnp.dot(tri.astype(BF16), jnp.concatenate([hi, mid, lo], axis=1), preferred_element_type=F32)
    return r[:, :n] + r[:, n:2 * n] + r[:, 2 * n:]


def _params(sem):
    return pltpu.CompilerParams(dimension_semantics=sem, vmem_limit_bytes=VMEM_LIMIT)


def _adaln_kernel(c_ref, w_ref, b_ref, o_ref):
    s = _silu(c_ref[...])
    o_ref[0] = _dot(s, w_ref[0]) + b_ref[0]


def _adaln(cvec, ada_w, ada_b):
    nb = 4
    wb = 6 * D_MODEL // nb
    return pl.pallas_call(
        _adaln_kernel,
        out_shape=jax.ShapeDtypeStruct((DEPTH, MOD_ROWS, 6 * D_MODEL), F32),
        grid=(DEPTH, nb),
        in_specs=[pl.BlockSpec((MOD_ROWS, D_MODEL), lambda l, j: (0, 0)),
                  pl.BlockSpec((1, D_MODEL, wb), lambda l, j: (l, 0, j)),
                  pl.BlockSpec((1, 1, wb), lambda l, j: (l, 0, j))],
        out_specs=pl.BlockSpec((1, MOD_ROWS, wb), lambda l, j: (l, 0, j)),
        compiler_params=_params(("parallel", "parallel")),
        name="adaln",
    )(cvec, ada_w, ada_b.reshape(DEPTH, 1, 6 * D_MODEL))


def _inproj_kernel(*refs, rope_cols):
    if rope_cols is None:
        x_ref, mod_ref, nw_ref, w_ref, o_ref = refs
    else:
        x_ref, mod_ref, nw_ref, w_ref, cos_ref, sin_ref, o_ref = refs
    h = _rms(x_ref[...], nw_ref[...]) * (1.0 + mod_ref[0, 1:2, :]) + mod_ref[0, 0:1, :]
    r = jnp.dot(h.astype(BF16), w_ref[...], preferred_element_type=F32)
    if rope_cols is None:
        o_ref[...] = r
        return
    lo, hi = rope_cols
    o_ref[:, :lo] = r[:, :lo]
    o_ref[:, hi:] = r[:, hi:]
    cos = cos_ref[...]
    sin = sin_ref[...]
    lane = lax.broadcasted_iota(jnp.int32, (TM, LANE), 1)
    upper = (lane & 16) != 0
    for c0 in range(lo, hi, LANE):
        v = r[:, c0:c0 + LANE]
        partner = jnp.where(upper, pltpu.roll(v, 16, 1), pltpu.roll(v, LANE - 16, 1))
        o_ref[:, c0:c0 + LANE] = v * cos + partner * sin


def _inproj(x, mod, norm_w, w_bf16, rope=None):
    p = w_bf16.shape[1]
    in_specs = [pl.BlockSpec((TM, D_MODEL), lambda t: (t, 0)),
                pl.BlockSpec((1, MOD_ROWS, D_MODEL), lambda t: (_mod_index(t), 0, 0)),
                pl.BlockSpec((1, D_MODEL), lambda t: (0, 0)),
                pl.BlockSpec((D_MODEL, p), lambda t: (0, 0))]
    args = [x, mod, norm_w.reshape(1, D_MODEL), w_bf16]
    rope_cols = None
    if rope is not None:
        cos, sin, rope_cols = rope

        def rope_index(t):
            return (jnp.where(t < PROMPT_TILES, 0, 1 + (t - PROMPT_TILES) % TILES_PER_SAMPLE), 0)

        in_specs += [pl.BlockSpec((TM, LANE), rope_index), pl.BlockSpec((TM, LANE), rope_index)]
        args += [cos, sin]
    return pl.pallas_call(
        functools.partial(_inproj_kernel, rope_cols=rope_cols),
        out_shape=jax.ShapeDtypeStruct((N_TOK, p), F32),
        grid=(N_TILES,),
        in_specs=in_specs,
        out_specs=pl.BlockSpec((TM, p), lambda t: (t, 0)),
        compiler_params=_params(("parallel",)),
        name="inproj",
    )(*args)


def _rope_tables():
    lane = jnp.arange(LANE)
    d = lane % DH_B
    use_col = (d // 32) == 1
    j = d % 16
    upper = ((d % 32) // 16) == 1
    inv_freq = ROPE_BASE ** (-j.astype(F32) / 16.0)
    t = jnp.arange(DEC_SEQ)
    row = (t // GRID_W).astype(F32)
    col = (t % GRID_W).astype(F32)
    pos = jnp.where(use_col[None, :], col[:, None], row[:, None])
    ang = pos * inv_freq[None, :]
    cos = jnp.cos(ang)
    sin = jnp.where(upper[None, :], jnp.sin(ang), -jnp.sin(ang))
    cos = jnp.concatenate([jnp.ones((TM, LANE), F32), cos], axis=0)
    sin = jnp.concatenate([jnp.zeros((TM, LANE), F32), sin], axis=0)
    return cos, sin


def _route(logits):
    lane = lax.broadcasted_iota(jnp.int32, logits.shape, 1)
    lanef = lane.astype(F32)
    neg = jnp.float32(-jnp.inf)
    gl = jnp.where(lane < N_GROUPS, logits, neg)
    gmax = jnp.max(gl, axis=1, keepdims=True)
    gsel = jnp.min(jnp.where(gl == gmax, lanef, float(LANE)), axis=1, keepdims=True)
    p_grp = 1.0 / jnp.sum(jnp.exp(gl - gmax), axis=1, keepdims=True)
    lo = float(N_GROUPS) + gsel * float(E_PER_GROUP)
    el = jnp.where((lanef >= lo) & (lanef < lo + float(E_PER_GROUP)), logits, neg)
    v1 = jnp.max(el, axis=1, keepdims=True)
    i1 = jnp.min(jnp.where(el == v1, lanef, float(LANE)), axis=1, keepdims=True)
    el2 = jnp.where(lanef == i1, neg, el)
    v2 = jnp.max(el2, axis=1, keepdims=True)
    i2 = jnp.min(jnp.where(el2 == v2, lanef, float(LANE)), axis=1, keepdims=True)
    t = jnp.exp(v2 - v1)
    w1 = p_grp / (1.0 + t)
    w2 = p_grp * t / (1.0 + t)
    out = jnp.where(lane == 0, i1 - float(N_GROUPS), 0.0)
    out = jnp.where(lane == 1, i2 - float(N_GROUPS), out)
    out = jnp.where(lane == 2, w1, out)
    out = jnp.where(lane == 3, w2, out)
    return out


def _outproj_kernel(*refs, widths):
    n_in = len(widths)
    w_ref, x_ref, mod_ref, nw_ref, wr_ref, xn_ref, h2_ref, rt_ref = refs[2 * n_in:]
    is_prompt = pl.program_id(0) < PROMPT_TILES
    y = None
    c0 = 0
    for k, wd in enumerate(widths):
        o = jnp.where(is_prompt, refs[2 * k][...], refs[2 * k + 1][...])
        part = jnp.dot(o.astype(BF16), w_ref[c0:c0 + wd, :], preferred_element_type=F32)
        y = part if y is None else y + part
        c0 += wd
    xn = x_ref[...] + mod_ref[0, 2:3, :] * y
    xn_ref[...] = xn
    h2 = _rms(xn, nw_ref[...]) * (1.0 + mod_ref[0, 4:5, :]) + mod_ref[0, 3:4, :]
    h2_ref[...] = h2
    rt_ref[...] = _route(_dot_f32(h2, wr_ref[...]))


def _outproj(parts, w_bf16, x, mod, norm_w2, w_router):
    widths = tuple(p[0].shape[1] for p in parts)
    in_specs = []
    for wd in widths:
        in_specs += [pl.BlockSpec((TM, wd), lambda t: (jnp.minimum(t, PROMPT_TILES - 1), 0)),
                     pl.BlockSpec((TM, wd), lambda t: (jnp.maximum(t - PROMPT_TILES, 0), 0))]
    parts = [a for p in parts for a in p]
    in_specs += [pl.BlockSpec((D_MODEL, D_MODEL), lambda t: (0, 0)),
                 pl.BlockSpec((TM, D_MODEL), lambda t: (t, 0)),
                 pl.BlockSpec((1, MOD_ROWS, D_MODEL), lambda t: (_mod_index(t), 0, 0)),
                 pl.BlockSpec((1, D_MODEL), lambda t: (0, 0)),
                 pl.BlockSpec((D_MODEL, LANE), lambda t: (0, 0))]
    return pl.pallas_call(
        functools.partial(_outproj_kernel, widths=widths),
        out_shape=(jax.ShapeDtypeStruct((N_TOK, D_MODEL), F32),
                   jax.ShapeDtypeStruct((N_TOK, D_MODEL), F32),
                   jax.ShapeDtypeStruct((N_TOK, LANE), F32)),
        grid=(N_TILES,),
        in_specs=in_specs,
        out_specs=(pl.BlockSpec((TM, D_MODEL), lambda t: (t, 0)),
                   pl.BlockSpec((TM, D_MODEL), lambda t: (t, 0)),
                   pl.BlockSpec((TM, LANE), lambda t: (t, 0))),
        compiler_params=_params(("parallel",)),
        name="outproj",
    )(*parts, w_bf16, x, mod, norm_w2.reshape(1, D_MODEL), w_router)


def _expert_kernel(blk_e_ref, nact_ref, x_ref, wg_ref, wu_ref, wd_ref, y_ref, wg_s, wu_s, wd_s):
    i = pl.program_id(0)
    e = blk_e_ref[i]
    prev = blk_e_ref[jnp.maximum(i - 1, 0)]

    @pl.when((i == 0) | (e != prev))
    def _():
        wg_s[...] = wg_ref[...].astype(BF16)
        wu_s[...] = wu_ref[...].astype(BF16)
        wd_s[...] = wd_ref[...].astype(BF16)

    @pl.when(i < nact_ref[0])
    def _():
        x = x_ref[...].astype(BF16)
        g = jnp.dot(x, wg_s[...], preferred_element_type=F32)
        u = jnp.dot(x, wu_s[...], preferred_element_type=F32)
        y_ref[...] = jnp.dot((_silu(g) * u).astype(BF16), wd_s[...], preferred_element_type=F32)

    @pl.when(i >= nact_ref[0])
    def _():
        y_ref[...] = jnp.zeros_like(y_ref)


def _experts(xb, blk_e, n_active, w_gate, w_up, w_down, layer):
    n_rows = xb.shape[0]
    n_blocks = n_rows // MOE_ROWS
    grid_spec = pltpu.PrefetchScalarGridSpec(
        num_scalar_prefetch=2,
        grid=(n_blocks,),
        in_specs=[pl.BlockSpec((MOE_ROWS, D_MODEL), lambda i, be, na: (i, 0)),
                  pl.BlockSpec((None, None, D_MODEL, D_EXPERT), lambda i, be, na: (layer, be[i], 0, 0)),
                  pl.BlockSpec((None, None, D_MODEL, D_EXPERT), lambda i, be, na: (layer, be[i], 0, 0)),
                  pl.BlockSpec((None, None, D_EXPERT, D_MODEL), lambda i, be, na: (layer, be[i], 0, 0))],
        out_specs=pl.BlockSpec((MOE_ROWS, D_MODEL), lambda i, be, na: (i, 0)),
        scratch_shapes=[pltpu.VMEM((D_MODEL, D_EXPERT), BF16),
                        pltpu.VMEM((D_MODEL, D_EXPERT), BF16),
                        pltpu.VMEM((D_EXPERT, D_MODEL), BF16)])
    return pl.pallas_call(
        _expert_kernel,
        out_shape=jax.ShapeDtypeStruct((n_rows, D_MODEL), F32),
        grid_spec=grid_spec,
        compiler_params=_params(("arbitrary",)),
        name="experts",
    )(blk_e, n_active, xb, w_gate, w_up, w_down)


def _combine_kernel(x_ref, y0_ref, y1_ref, rt_ref, mod_ref, o_ref):
    rt = rt_ref[...]
    y = rt[:, 2:3] * y0_ref[...] + rt[:, 3:4] * y1_ref[...]
    o_ref[...] = x_ref[...] + mod_ref[0, 5:6, :] * y


def _combine(x, y0, y1, route, mod):
    row = pl.BlockSpec((TM, D_MODEL), lambda t: (t, 0))
    return pl.pallas_call(
        _combine_kernel,
        out_shape=jax.ShapeDtypeStruct((N_TOK, D_MODEL), F32),
        grid=(N_TILES,),
        in_specs=[row, row, row,
                  pl.BlockSpec((TM, LANE), lambda t: (t, 0)),
                  pl.BlockSpec((1, MOD_ROWS, D_MODEL), lambda t: (_mod_index(t), 0, 0))],
        out_specs=row,
        compiler_params=_params(("parallel",)),
        name="combine",
    )(x, y0, y1, route, mod)


def _take_rows(a, idx):
    return a.at[idx].get(mode="promise_in_bounds")


def _moe(x, h2, route, mod, w_gate, w_up, w_down, layer):
    n_asg = 2 * N_TOK
    flat_e = route[:, :2].astype(jnp.int32).reshape(n_asg)
    flat_tok = jnp.arange(n_asg, dtype=jnp.int32) // 2
    onehot = (flat_e[:, None] == jnp.arange(N_EXPERTS, dtype=jnp.int32)[None, :]).astype(jnp.int32)
    csum = jnp.cumsum(onehot, axis=0)
    rank = jnp.take_along_axis(csum, flat_e[:, None], axis=1)[:, 0] - 1
    counts = csum[-1]
    padded = (counts + MOE_ROWS - 1) // MOE_ROWS * MOE_ROWS
    pad_end = jnp.cumsum(padded)
    pad_start = pad_end - padded
    dest = pad_start[flat_e] + rank
    n_rows = n_asg + N_EXPERTS * MOE_ROWS
    n_blocks = n_rows // MOE_ROWS
    row_tok = jnp.zeros((n_rows,), jnp.int32).at[dest].set(
        flat_tok, unique_indices=True, mode="promise_in_bounds")
    blk_start = jnp.arange(n_blocks, dtype=jnp.int32) * MOE_ROWS
    blk_e = jnp.sum((blk_start[:, None] >= pad_end[None, :]).astype(jnp.int32), axis=1)
    blk_e = jnp.minimum(blk_e, N_EXPERTS - 1)
    n_active = (pad_end[-1:] // MOE_ROWS).astype(jnp.int32)
    xb = _take_rows(h2, row_tok)
    yb = _experts(xb, blk_e, n_active, w_gate, w_up, w_down, layer)
    dest2 = dest.reshape(N_TOK, 2)
    return _combine(x, _take_rows(yb, dest2[:, 0]), _take_rows(yb, dest2[:, 1]), route, mod)


def _final_kernel(x_ref, w_ref, o_ref):
    o_ref[...] = _rms(x_ref[...], w_ref[...])


def _final_norm(x, w, tile0, n_rows):
    return pl.pallas_call(
        _final_kernel,
        out_shape=jax.ShapeDtypeStruct((n_rows, D_MODEL), F32),
        grid=(n_rows // TM,),
        in_specs=[pl.BlockSpec((TM, D_MODEL), lambda t: (t + tile0, 0)),
                  pl.BlockSpec((1, D_MODEL), lambda t: (0, 0))],
        out_specs=pl.BlockSpec((TM, D_MODEL), lambda t: (t, 0)),
        compiler_params=_params(("parallel",)),
        name="final_norm",
    )(x, w.reshape(1, D_MODEL))


def _hgrn_kernel(*refs, seq_len, layer, context):
    if context:
        (q_ref, ff_ref, fb_ref, v_ref, g_ref, lbl_ref, nw_ref,
         o_ref, sfin_ref, of_s, ob_s, sf_s, sb_s) = refs
    else:
        (q_ref, ff_ref, fb_ref, v_ref, g_ref, lbl_ref, nw_ref, s0_ref,
         o_ref, of_s, ob_s, sf_s, sb_s) = refs
    c = CHUNK_A
    n = seq_len // c

    def lower_bound(d):
        z = lbl_ref[d]
        e = jnp.exp(z - jnp.max(z, axis=0, keepdims=True))
        return jnp.sum(e[:layer + 1], axis=0, keepdims=True) / jnp.sum(e, axis=0, keepdims=True)

    lb_f = lower_bound(0)
    lb_b = lower_bound(1)
    row = lax.broadcasted_iota(jnp.int32, (c, c), 0)
    col = lax.broadcasted_iota(jnp.int32, (c, c), 1)
    causal = row >= col
    tri_f = causal.astype(F32)
    tri_b = (row <= col).astype(F32)

    if context:
        sf_s[...] = jnp.zeros_like(sf_s)
        sb_s[...] = jnp.zeros_like(sb_s)
    else:
        sf_s[...] = s0_ref[0, 0, 0].T
        sb_s[...] = s0_ref[0, 1, 0].T

    group = 4
    dirs = ((ff_ref, lb_f, tri_f, causal, c - 1, sf_s, of_s),
            (fb_ref, lb_b, tri_b, row <= col, 0, sb_s, ob_s))

    def body(i, carry):
        items = []
        for g in range(group):
            j = i * group + g
            items.append((0, pl.ds(pl.multiple_of(j * c, c), c)))
            items.append((1, pl.ds(pl.multiple_of((n - 1 - j) * c, c), c)))
        fs = [dirs[d][1] + (1.0 - dirs[d][1]) * _sigmoid(dirs[d][0][rows, :]) for d, rows in items]
        bs = [_tri_matmul(dirs[d][2], jnp.log(f)) for (d, _), f in zip(items, fs)]
        b_lasts = [b[dirs[d][4]:dirs[d][4] + 1, :] for (d, _), b in zip(items, bs)]
        q_ins = [_silu(q_ref[rows, :]) * jnp.exp(b) for (_, rows), b in zip(items, bs)]
        a_s = [jnp.where(dirs[d][3], _dot_nt(q_in, (1.0 - f) * jnp.exp(-b)), 0.0)
               for (d, _), q_in, f, b in zip(items, q_ins, fs, bs)]
        vs = [v_ref[rows, :] for _, rows in items]
        o_intra = [_dot(a, v) for a, v in zip(a_s, vs)]
        u_ts = [_dot_tn(v, (1.0 - f) * jnp.exp(b_last - b)) for v, f, b, b_last in zip(vs, fs, bs, b_lasts)]
        for d in (0, 1):
            st_ref, out_s = dirs[d][5], dirs[d][6]
            st = st_ref[...]
            for k, (dk, rows) in enumerate(items):
                if dk == d:
                    out_s[rows, :] = o_intra[k] + _dot_nt(q_ins[k], st)
                    st = st * jnp.exp(b_lasts[k]) + u_ts[k]
            st_ref[...] = st
        return carry

    lax.fori_loop(0, n // group, body, 0)

    if context:
        sfin_ref[0, 0, 0] = sf_s[...].T
        sfin_ref[0, 1, 0] = sb_s[...].T

    nw = nw_ref[...]

    def epilogue(j, carry):
        rows = pl.ds(pl.multiple_of(j * TM, TM), TM)
        o = of_s[rows, :] + ob_s[rows, :]
        o_ref[rows, :] = _rms(o, nw) * _silu(g_ref[rows, :])
        return carry

    lax.fori_loop(0, seq_len // TM, epilogue, 0)


def _hgrn(proj, lb_logits, norm_w, layer, context, s0=None):
    seq_len = SEQ if context else DEC_SEQ
    bsz = BATCH if context else DEC_BATCH
    row0 = 0 if context else N_PROMPT // seq_len

    def col(k):
        return pl.BlockSpec((seq_len, LANE), lambda b, h: (row0 + b, k * H_A + h))

    in_specs = [col(0), col(1), col(2), col(3), col(4),
                pl.BlockSpec((2, DEPTH + 1, LANE), lambda b, h: (0, 0, h)),
                pl.BlockSpec((1, LANE), lambda b, h: (0, 0))]
    args = [proj, proj, proj, proj, proj, lb_logits, norm_w.reshape(1, LANE)]
    o_shape = jax.ShapeDtypeStruct((bsz * seq_len, W_A), F32)
    o_spec = pl.BlockSpec((seq_len, LANE), lambda b, h: (b, h))
    scratch = [pltpu.VMEM((seq_len, LANE), F32), pltpu.VMEM((seq_len, LANE), F32),
               pltpu.VMEM((LANE, LANE), F32), pltpu.VMEM((LANE, LANE), F32)]
    kern = functools.partial(_hgrn_kernel, seq_len=seq_len, layer=layer, context=context)
    if context:
        return pl.pallas_call(
            kern,
            out_shape=(o_shape, jax.ShapeDtypeStruct((bsz, 2, H_A, DK_A, DK_A), F32)),
            grid=(bsz, H_A),
            in_specs=in_specs,
            out_specs=(o_spec, pl.BlockSpec((1, 2, 1, DK_A, DK_A), lambda b, h: (b, 0, h, 0, 0))),
            scratch_shapes=scratch,
            compiler_params=_params(("parallel", "parallel")),
            name="hgrn_ctx",
        )(*args)
    in_specs += [pl.BlockSpec((1, 2, 1, DK_A, DK_A), lambda b, h: (b, 0, h, 0, 0))]
    args += [s0]
    return pl.pallas_call(
        kern,
        out_shape=o_shape,
        grid=(bsz, H_A),
        in_specs=in_specs,
        out_specs=o_spec,
        scratch_shapes=scratch,
        compiler_params=_params(("parallel", "parallel")),
        name="hgrn_dec",
    )(*args)


ATT_TQ = 256
COL_QD = 5 * W_A // LANE
COL_KD = COL_QD + QK_B // LANE
COL_VD = COL_KD + QK_B // LANE


def _attn_kernel(*refs, seq_len, layer, context):
    if context:
        q_ref, k_ref, v_ref, lam_ref, nw_ref, o_ref, k_s, v_s = refs
    else:
        q_ref, k_ref, v_ref, ck_ref, cv_ref, lam_ref, nw_ref, o_ref, k_s, v_s = refs

    @pl.when(pl.program_id(2) == 0)
    def _():
        k_s[0:seq_len, :] = k_ref[...].astype(BF16)
        v_s[0:seq_len, :] = v_ref[...].astype(BF16)
        if not context:
            k_s[seq_len:, :] = ck_ref[0].astype(BF16)
            v_s[seq_len:, :] = cv_ref[0].astype(BF16)

    lam_init = 0.8 - 0.6 * math.exp(-0.3 * layer)
    lp = lam_ref[...]
    lam = (jnp.exp(jnp.sum(lp[0:1] * lp[1:2], axis=1, keepdims=True))
           - jnp.exp(jnp.sum(lp[2:3] * lp[3:4], axis=1, keepdims=True)) + lam_init)

    q = q_ref[...] * (DH_B ** -0.5)
    lane = lax.broadcasted_iota(jnp.int32, q.shape, 1)
    k = k_s[...]

    def softmax_map(first):
        s = _dot_nt(jnp.where((lane < DH_B) == first, q, 0.0), k)
        p = jnp.exp(s - jnp.max(s, axis=1, keepdims=True))
        return p, jnp.sum(p, axis=1, keepdims=True)

    p0, l0 = softmax_map(True)
    p1, l1 = softmax_map(False)
    a = p0 - (lam * l0 / l1) * p1
    o = jnp.dot(a.astype(BF16), v_s[...], preferred_element_type=F32) / l0
    o_ref[...] = _rms(o, nw_ref[...]) * (1.0 - lam_init)


def _attn(proj, lam_p, norm_w, layer, context, cache_k=None, cache_v=None):
    seq_len = SEQ if context else DEC_SEQ
    bsz = BATCH if context else DEC_BATCH
    row0 = 0 if context else N_PROMPT // seq_len
    nq = seq_len // ATT_TQ
    tile0 = row0 * nq
    t_k = seq_len if context else seq_len + PAST_LEN
    in_specs = [pl.BlockSpec((ATT_TQ, LANE), lambda b, h, i: (tile0 + b * nq + i, COL_QD + h)),
                pl.BlockSpec((seq_len, LANE), lambda b, h, i: (row0 + b, COL_KD + h)),
                pl.BlockSpec((seq_len, LANE), lambda b, h, i: (row0 + b, COL_VD + h))]
    args = [proj, proj, proj]
    if not context:
        in_specs += [pl.BlockSpec((1, PAST_LEN, LANE), lambda b, h, i: (b, 0, h)),
                     pl.BlockSpec((1, PAST_LEN, LANE), lambda b, h, i: (b, 0, h))]
        args += [cache_k, cache_v]
    in_specs += [pl.BlockSpec((4, DH_B), lambda b, h, i: (0, 0)),
                 pl.BlockSpec((1, LANE), lambda b, h, i: (0, 0))]
    args += [lam_p, norm_w.reshape(1, LANE)]
    return pl.pallas_call(
        functools.partial(_attn_kernel, seq_len=seq_len, layer=layer, context=context),
        out_shape=jax.ShapeDtypeStruct((bsz * seq_len, W_B), F32),
        grid=(bsz, H_B, nq),
        in_specs=in_specs,
        out_specs=pl.BlockSpec((ATT_TQ, LANE), lambda b, h, i: (b * nq + i, h)),
        scratch_shapes=[pltpu.VMEM((t_k, LANE), BF16), pltpu.VMEM((t_k, LANE), BF16)],
        compiler_params=_params(("parallel", "parallel", "arbitrary")),
        name="attn_ctx" if context else "attn_dec",
    )(*args)


COL_GATES = 4 * W_C // LANE


def _gdn_gates_kernel(g_ref, alog_ref, dtb_ref, col_ref, row_ref):
    c = GDN_CHUNK
    raw = g_ref[...]
    z = raw + dtb_ref[...]
    softplus = jnp.maximum(z, 0.0) + jnp.log(1.0 + jnp.exp(-jnp.abs(z)))
    g = -jnp.exp(alog_ref[...]) * softplus
    row = lax.broadcasted_iota(jnp.int32, (c, c), 0)
    col = lax.broadcasted_iota(jnp.int32, (c, c), 1)
    lane = lax.broadcasted_iota(jnp.int32, (c, LANE), 1)
    prefix = _tri_matmul((row >= col).astype(F32), g)
    suffix = _tri_matmul((row <= col).astype(F32), g)
    out = jnp.where(lane < H_C, prefix, jnp.where(lane < 2 * H_C, suffix, _sigmoid(raw)))
    col_ref[...] = out
    row_ref[0] = out.T


def _gdn_gates(proj, alog_lane, dtb_lane):
    c = GDN_CHUNK
    n = N_TOK // c
    return pl.pallas_call(
        _gdn_gates_kernel,
        out_shape=(jax.ShapeDtypeStruct((N_TOK, LANE), F32), jax.ShapeDtypeStruct((n, LANE, c), F32)),
        grid=(n,),
        in_specs=[pl.BlockSpec((c, LANE), lambda i: (i, COL_GATES)),
                  pl.BlockSpec((1, LANE), lambda i: (0, 0)),
                  pl.BlockSpec((1, LANE), lambda i: (0, 0))],
        out_specs=(pl.BlockSpec((c, LANE), lambda i: (i, 0)),
                   pl.BlockSpec((1, LANE, c), lambda i: (i, 0, 0))),
        compiler_params=_params(("parallel",)),
        name="gdn_gates",
    )(proj, alog_lane, dtb_lane)


def _unit_tri_inverse_pairs(ms, row, col):
    c = ms[0].shape[0]
    left = lax.broadcasted_iota(jnp.int32, ms[0].shape, 1) < c

    def mm(xs, ys):
        out = []
        for x, y in zip(xs, ys):
            yb = y.astype(BF16)
            zero = jnp.zeros_like(yb)
            blockdiag = jnp.concatenate([jnp.where(left, yb, zero), jnp.where(left, zero, yb)], axis=0)
            out.append(jnp.dot(x.astype(BF16), blockdiag, preferred_element_type=F32))
        return out

    def add(xs, ys):
        return [x + y for x, y in zip(xs, ys)]

    eye = (row == col).astype(F32)
    a = [jnp.where((row // 16) == (col // 16), m, 0.0) for m in ms]
    a2 = mm(a, a)
    a4 = mm(a2, a2)
    a8 = mm(a4, a4)
    t = [eye - x for x in a]
    t = add(t, mm(t, a2))
    t = add(t, mm(t, a4))
    t = add(t, mm(t, a8))
    blk = 32
    while blk <= c:
        off = ((row // blk) == (col // blk)) & ((row // (blk // 2)) != (col // (blk // 2)))
        corr = mm(mm(t, [jnp.where(off, m, 0.0) for m in ms]), t)
        t = [x - y for x, y in zip(t, corr)]
        blk *= 2
    return t


def _gdn_kernel(*refs, seq_len, context):
    if context:
        (q_ref, k_ref, v_ref, go_ref, cwq_ref, cwk_ref, cwv_ref, gcol_ref, grow_ref, nw_ref,
         o_ref, sfin_ref, u_s, wq_s, qk_s, kd_s, of_s, ob_s, sf_s, sb_s) = refs
    else:
        (q_ref, k_ref, v_ref, go_ref, cwq_ref, cwk_ref, cwv_ref, gcol_ref, grow_ref, nw_ref, s0_ref,
         o_ref, u_s, wq_s, qk_s, kd_s, of_s, ob_s, sf_s, sb_s) = refs
    c = GDN_CHUNK
    n = seq_len // c
    head = pl.program_id(1)
    row = lax.broadcasted_iota(jnp.int32, (c, c), 0)
    col = lax.broadcasted_iota(jnp.int32, (c, c), 1)
    row2 = lax.broadcasted_iota(jnp.int32, (c, 2 * c), 0)
    col2 = lax.broadcasted_iota(jnp.int32, (c, 2 * c), 1) & (c - 1)
    lane = lax.broadcasted_iota(jnp.int32, (c, LANE), 1)
    rowi = lax.broadcasted_iota(jnp.int32, (c, LANE), 0)

    def gc_row_last(ci, d):
        gc_row = grow_ref[ci, pl.ds(d * H_C + head, 1), :]
        return gc_row, (gc_row[:, c - 1:c] if d == 0 else gc_row[:, 0:1])

    def chunk_inputs(ci):
        r0 = pl.multiple_of(ci * c, c)
        rows = pl.ds(r0, c)

        def conv(x_ref, w_ref):
            cur = x_ref[rows, :]
            before = x_ref[pl.ds(pl.multiple_of(jnp.maximum(r0 - 8, 0), 8), 8), :]
            after = x_ref[pl.ds(pl.multiple_of(jnp.minimum(r0 + c, seq_len - 8), 8), 8), :]
            prev_row = jnp.where(ci > 0, before[7:8, :], 0.0)
            next_row = jnp.where(ci < n - 1, after[0:1, :], 0.0)
            xm1 = jnp.where(rowi == 0, prev_row, pltpu.roll(cur, 1, 0))
            xp1 = jnp.where(rowi == c - 1, next_row, pltpu.roll(cur, c - 1, 0))
            w = w_ref[...]
            return _silu(xm1 * w[0:1, :] + cur * w[1:2, :] + xp1 * w[2:3, :])

        q = conv(q_ref, cwq_ref)
        k = conv(k_ref, cwk_ref)
        vn = conv(v_ref, cwv_ref)
        qn = q * lax.rsqrt(jnp.sum(q * q, axis=1, keepdims=True) + EPS) * (DK_C ** -0.5)
        kn = k * lax.rsqrt(jnp.sum(k * k, axis=1, keepdims=True) + EPS)
        kq = _dot_nt(jnp.concatenate([kn, qn], axis=0), kn)
        gates = gcol_ref[rows, :]
        per_dir = []
        ms = []
        for d in (0, 1):
            gc = jnp.sum(jnp.where(lane == d * H_C + head, gates, 0.0), axis=1, keepdims=True)
            beta = jnp.sum(jnp.where(lane == (2 + d) * H_C + head, gates, 0.0), axis=1, keepdims=True)
            gc_row, gc_last = gc_row_last(ci, d)
            incl = (row >= col) if d == 0 else (row <= col)
            strict = (row > col) if d == 0 else (row < col)
            decay = jnp.where(incl, jnp.exp(gc - gc_row), 0.0)
            m = jnp.where(strict, kq[:c] * beta * decay, 0.0)
            qk_s[d, rows, :] = (kq[c:] * decay).astype(BF16)
            kd_s[d, rows, :] = (kn * jnp.exp(gc_last - gc)).astype(BF16)
            per_dir.append((gc, beta))
            ms.append(m)
        return jnp.concatenate(ms, axis=1), (rows, qn, kn, vn, per_dir)

    def chunk_outputs(ci, t, rest):
        rows, qn, kn, vn, per_dir = rest
        base = pl.multiple_of(ci * 2 * c, 2 * c)
        for d in (0, 1):
            gc, beta = per_dir[d]
            e = jnp.exp(gc)
            uw = _dot(t[:, d * c:(d + 1) * c], jnp.concatenate([vn * beta, kn * (beta * e)], axis=1))
            u_s[d, rows, :] = uw[:, :LANE]
            wq_s[d, pl.ds(base, c), :] = uw[:, LANE:].astype(BF16)
            wq_s[d, pl.ds(base + c, c), :] = (qn * e).astype(BF16)

    group = min(4, n)

    def prepare(j, carry):
        cis = [j * group + g for g in range(group)]
        staged = [chunk_inputs(ci) for ci in cis]
        ts = _unit_tri_inverse_pairs([s[0] for s in staged], row2, col2)
        for ci, t, s in zip(cis, ts, staged):
            chunk_outputs(ci, t, s[1])
        return carry

    lax.fori_loop(0, n // group, prepare, 0)

    if context:
        sf_s[...] = jnp.zeros_like(sf_s)
        sb_s[...] = jnp.zeros_like(sb_s)
    else:
        sf_s[...] = s0_ref[0, 0, 0]
        sb_s[...] = s0_ref[0, 1, 0]

    def advance(ci, d, s_ref, out_s):
        rows = pl.ds(pl.multiple_of(ci * c, c), c)
        _, gc_last = gc_row_last(ci, d)
        s = s_ref[...]
        ws = jnp.dot(wq_s[d, pl.ds(pl.multiple_of(ci * 2 * c, 2 * c), 2 * c), :], s.astype(BF16),
                     preferred_element_type=F32)
        v_new = (u_s[d, rows, :] - ws[:c]).astype(BF16)
        out_s[rows, :] = ws[c:] + jnp.dot(qk_s[d, rows, :], v_new, preferred_element_type=F32)
        s_ref[...] = s * jnp.exp(gc_last) + lax.dot_general(
            kd_s[d, rows, :], v_new, (((0,), (0,)), ((), ())), preferred_element_type=F32)

    def body(i, carry):
        advance(i, 0, sf_s, of_s)
        advance(n - 1 - i, 1, sb_s, ob_s)
        return carry

    lax.fori_loop(0, n, body, 0)

    if context:
        sfin_ref[0, 0, 0] = sf_s[...]
        sfin_ref[0, 1, 0] = sb_s[...]

    nw = nw_ref[...]

    def epilogue(j, carry):
        rows = pl.ds(pl.multiple_of(j * TM, TM), TM)
        o = of_s[rows, :] + ob_s[rows, :]
        o_ref[rows, :] = _rms(o, nw) * _silu(go_ref[rows, :])
        return carry

    lax.fori_loop(0, seq_len // TM, epilogue, 0)


def _gdn(proj, conv_w, gcol, grow, norm_w, context, s0=None):
    seq_len = SEQ if context else DEC_SEQ
    bsz = BATCH if context else DEC_BATCH
    row0 = 0 if context else N_PROMPT // seq_len
    nc = seq_len // GDN_CHUNK

    def col(k):
        return pl.BlockSpec((seq_len, LANE), lambda b, h: (row0 + b, k * H_C + h))

    def cw(k):
        return pl.BlockSpec((3, LANE), lambda b, h: (0, k * H_C + h))

    in_specs = [col(0), col(1), col(2), col(3), cw(0), cw(1), cw(2),
                pl.BlockSpec((seq_len, LANE), lambda b, h: (row0 + b, 0)),
                pl.BlockSpec((nc, LANE, GDN_CHUNK), lambda b, h: (row0 + b, 0, 0)),
                pl.BlockSpec((1, LANE), lambda b, h: (0, 0))]
    args = [proj, proj, proj, proj, conv_w, conv_w, conv_w, gcol, grow, norm_w.reshape(1, LANE)]
    o_shape = jax.ShapeDtypeStruct((bsz * seq_len, W_C), F32)
    o_spec = pl.BlockSpec((seq_len, LANE), lambda b, h: (b, h))
    seq_buf = pltpu.VMEM((seq_len, LANE), F32)
    state_buf = pltpu.VMEM((DK_C, DK_C), F32)
    dir_bf16 = pltpu.VMEM((2, seq_len, LANE), BF16)
    scratch = [pltpu.VMEM((2, seq_len, LANE), F32), pltpu.VMEM((2, 2 * seq_len, LANE), BF16), dir_bf16, dir_bf16,
               seq_buf, seq_buf, state_buf, state_buf]
    kern = functools.partial(_gdn_kernel, seq_len=seq_len, context=context)
    state_spec = pl.BlockSpec((1, 2, 1, DK_C, DK_C), lambda b, h: (b, 0, h, 0, 0))
    if context:
        return pl.pallas_call(
            kern,
            out_shape=(o_shape, jax.ShapeDtypeStruct((bsz, 2, H_C, DK_C, DK_C), F32)),
            grid=(bsz, H_C),
            in_specs=in_specs,
            out_specs=(o_spec, state_spec),
            scratch_shapes=scratch,
            compiler_params=_params(("parallel", "parallel")),
            name="gdn_ctx",
        )(*args)
    in_specs += [state_spec]
    args += [s0]
    return pl.pallas_call(
        kern,
        out_shape=o_shape,
        grid=(bsz, H_C),
        in_specs=in_specs,
        out_specs=o_spec,
        scratch_shapes=scratch,
        compiler_params=_params(("parallel", "parallel")),
        name="gdn_dec",
    )(*args)


def kernel(x_prompt, x_sample, c, cache_diff_k, cache_diff_v, state_hgrn, state_gdn, c_ctx,
           ada_w, ada_b, norm_w, final_norm_w, w_in_ab, hgrn_lb_logits, hgrn_norm_w,
           diff_lambda, diff_norm_w, w_in_c, gdn_conv_w, gdn_a_log, gdn_dt_bias, gdn_norm_w,
           w_out, moe_router_group, moe_router_expert, moe_w_gate, moe_w_up, moe_w_down):
    x = jnp.concatenate([x_prompt.reshape(N_PROMPT, D_MODEL), x_sample.reshape(N_SAMPLE, D_MODEL)], axis=0)
    cvec = jnp.concatenate([c_ctx[None, :], c, jnp.zeros((MOD_ROWS - N_MOD, D_MODEL), F32)], axis=0)
    mod_all = _adaln(cvec, ada_w, ada_b)[:, :N_MOD].reshape(DEPTH, N_MOD, 6, D_MODEL)
    mod_all = jnp.pad(mod_all, ((0, 0), (0, 0), (0, MOD_ROWS - 6), (0, 0)))
    cos, sin = _rope_tables()
    new_k, new_v, new_hgrn, new_gdn = [], [], [], []
    for l in range(DEPTH):
        i = l // 2
        mod = mod_all[l]
        if l % 2 == 0:
            proj = _inproj(x, mod, norm_w[l, 0], w_in_ab[i].astype(BF16),
                           rope=(cos, sin, (COL_QD * LANE, COL_VD * LANE)))
            oh_ctx, s_h = _hgrn(proj, hgrn_lb_logits, hgrn_norm_w[i], l, True)
            oh_dec = _hgrn(proj, hgrn_lb_logits, hgrn_norm_w[i], l, False, s0=state_hgrn[:, i])
            od_ctx = _attn(proj, diff_lambda[i], diff_norm_w[i], l, True)
            od_dec = _attn(proj, diff_lambda[i], diff_norm_w[i], l, False,
                           cache_k=cache_diff_k[:, i].reshape(DEC_BATCH, PAST_LEN, QK_B),
                           cache_v=cache_diff_v[:, i].reshape(DEC_BATCH, PAST_LEN, W_B))
            parts = ((oh_ctx, oh_dec), (od_ctx, od_dec))
            new_k.append(proj[:N_PROMPT, COL_KD * LANE:COL_VD * LANE].reshape(BATCH, SEQ, H_B, 2, DH_B))
            new_v.append(proj[:N_PROMPT, COL_VD * LANE:].reshape(BATCH, SEQ, H_B, DV_B))
            new_hgrn.append(s_h)
        else:
            w_c = jnp.pad(w_in_c[i], ((0, 0), (0, (COL_GATES + 1) * LANE - IN_C))).astype(BF16)
            proj = _inproj(x, mod, norm_w[l, 0], w_c)
            pad = jnp.zeros((LANE - 2 * H_C,), F32)
            alog_lane = jnp.concatenate([gdn_a_log[i, 0], gdn_a_log[i, 1], pad]).reshape(1, LANE)
            dtb_lane = jnp.concatenate([gdn_dt_bias[i, 0], gdn_dt_bias[i, 1], pad]).reshape(1, LANE)
            gcol, grow = _gdn_gates(proj, alog_lane, dtb_lane)
            oc_ctx, s_c = _gdn(proj, gdn_conv_w[i], gcol, grow, gdn_norm_w[i], True)
            oc_dec = _gdn(proj, gdn_conv_w[i], gcol, grow, gdn_norm_w[i], False, s0=state_gdn[:, i])
            parts = ((oc_ctx, oc_dec),)
            new_gdn.append(s_c)
        w_router = jnp.concatenate(
            [moe_router_group[l], moe_router_expert[l],
             jnp.zeros((D_MODEL, LANE - N_GROUPS - N_EXPERTS), F32)], axis=1)
        x, h2, route = _outproj(parts, w_out[l].astype(BF16), x, mod, norm_w[l, 1], w_router)
        x = _moe(x, h2, route, mod, moe_w_gate, moe_w_up, moe_w_down, l)
    y_prompt = _final_norm(x, final_norm_w, 0, N_PROMPT).reshape(BATCH, SEQ, D_MODEL)
    y_sample = _final_norm(x, final_norm_w, PROMPT_TILES, N_SAMPLE).reshape(DEC_BATCH, DEC_SEQ, D_MODEL)
    return (y_prompt, y_sample, jnp.stack(new_k, axis=1), jnp.stack(new_v, axis=1),
            jnp.stack(new_hgrn, axis=1), jnp.stack(new_gdn, axis=1))
```

```python
import functools
import math

import jax
import jax.numpy as jnp
from jax import lax
from jax.experimental import pallas as pl
from jax.experimental.pallas import tpu as pltpu

F32 = jnp.float32
BF16 = jnp.bfloat16
HIGHEST = lax.Precision.HIGHEST

D_MODEL = 1024
BATCH = 16
SEQ = 256
DEPTH = 2
DEC_BATCH = 4
DEC_SEQ = 4096
PAST_LEN = 256
GRID_W = 64
H_A = 4
DK_A = 128
W_A = 512
CHUNK_A = 32
H_B = 4
DH_B = 64
DV_B = 128
QK_B = 512
W_B = 512
ROPE_BASE = 10000.0
H_C = 8
DK_C = 128
W_C = 1024
N_GROUPS = 4
E_PER_GROUP = 8
N_EXPERTS = 32
D_EXPERT = 512
EPS = 1e-6
IN_AB = 5 * W_A + 2 * QK_B + W_B
IN_C = 4 * W_C + 4 * H_C

LANE = 128
N_PROMPT = BATCH * SEQ
N_SAMPLE = DEC_BATCH * DEC_SEQ
N_TOK = N_PROMPT + N_SAMPLE
TM = 256
N_TILES = N_TOK // TM
PROMPT_TILES = N_PROMPT // TM
TILES_PER_SAMPLE = DEC_SEQ // TM
N_MOD = 1 + DEC_BATCH
MOD_ROWS = 8
MOE_ROWS = 256
ROW_TILE = D_MODEL // LANE
GDN_CHUNK = 128
VMEM_LIMIT = 56 * 1024 * 1024


def _mod_index(t):
    return jnp.where(t < PROMPT_TILES, 0, 1 + (t - PROMPT_TILES) // TILES_PER_SAMPLE)


def _sigmoid(x):
    return 1.0 / (1.0 + jnp.exp(-x))


def _silu(x):
    return x * _sigmoid(x)


def _rms(x, w):
    return x * lax.rsqrt(jnp.mean(x * x, axis=-1, keepdims=True) + EPS) * w


def _dot(a, b):
    return jnp.dot(a.astype(BF16), b.astype(BF16), preferred_element_type=F32)


def _dot_nt(a, b):
    return lax.dot_general(a.astype(BF16), b.astype(BF16), (((1,), (1,)), ((), ())),
                           preferred_element_type=F32)


def _dot_tn(a, b):
    return lax.dot_general(a.astype(BF16), b.astype(BF16), (((0,), (0,)), ((), ())),
                           preferred_element_type=F32)


def _dot_f32(a, b):
    return jnp.dot(a, b, precision=HIGHEST, preferred_element_type=F32)


def _tri_matmul(tri, x):
    hi = x.astype(BF16)
    rem = x - hi.astype(F32)
    mid = rem.astype(BF16)
    lo = (rem - mid.astype(F32)).astype(BF16)
    n = x.shape[1]
    r = jnp.dot(tri.astype(BF16), jnp.concatenate([hi, mid, lo], axis=1), preferred_element_type=F32)
    return r[:, :n] + r[:, n:2 * n] + r[:, 2 * n:]


def _params(sem):
    return pltpu.CompilerParams(dimension_semantics=sem, vmem_limit_bytes=VMEM_LIMIT)


def _adaln_kernel(c_ref, w_ref, b_ref, o_ref):
    s = _silu(c_ref[...])
    o_ref[0] = _dot(s, w_ref[0]) + b_ref[0]


def _adaln(cvec, ada_w, ada_b):
    nb = 4
    wb = 6 * D_MODEL // nb
    return pl.pallas_call(
        _adaln_kernel,
        out_shape=jax.ShapeDtypeStruct((DEPTH, MOD_ROWS, 6 * D_MODEL), F32),
        grid=(DEPTH, nb),
        in_specs=[pl.BlockSpec((MOD_ROWS, D_MODEL), lambda l, j: (0, 0)),
                  pl.BlockSpec((1, D_MODEL, wb), lambda l, j: (l, 0, j)),
                  pl.BlockSpec((1, 1, wb), lambda l, j: (l, 0, j))],
        out_specs=pl.BlockSpec((1, MOD_ROWS, wb), lambda l, j: (l, 0, j)),
        compiler_params=_params(("parallel", "parallel")),
        name="adaln",
    )(cvec, ada_w, ada_b.reshape(DEPTH, 1, 6 * D_MODEL))


def _inproj_kernel(*refs, rope_cols):
    if rope_cols is None:
        x_ref, mod_ref, nw_ref, w_ref, o_ref = refs
    else:
        x_ref, mod_ref, nw_ref, w_ref, cos_ref, sin_ref, o_ref = refs
    h = _rms(x_ref[...], nw_ref[...]) * (1.0 + mod_ref[0, 1:2, :]) + mod_ref[0, 0:1, :]
    r = jnp.dot(h.astype(BF16), w_ref[...], preferred_element_type=F32)
    if rope_cols is None:
        o_ref[...] = r
        return
    lo, hi = rope_cols
    o_ref[:, :lo] = r[:, :lo]
    o_ref[:, hi:] = r[:, hi:]
    cos = cos_ref[...]
    sin = sin_ref[...]
    lane = lax.broadcasted_iota(jnp.int32, (TM, LANE), 1)
    upper = (lane & 16) != 0
    for c0 in range(lo, hi, LANE):
        v = r[:, c0:c0 + LANE]
        partner = jnp.where(upper, pltpu.roll(v, 16, 1), pltpu.roll(v, LANE - 16, 1))
        o_ref[:, c0:c0 + LANE] = v * cos + partner * sin


def _inproj(x, mod, norm_w, w_bf16, rope=None):
    p = w_bf16.shape[1]
    in_specs = [pl.BlockSpec((TM, D_MODEL), lambda t: (t, 0)),
                pl.BlockSpec((1, MOD_ROWS, D_MODEL), lambda t: (_mod_index(t), 0, 0)),
                pl.BlockSpec((1, D_MODEL), lambda t: (0, 0)),
                pl.BlockSpec((D_MODEL, p), lambda t: (0, 0))]
    args = [x, mod, norm_w.reshape(1, D_MODEL), w_bf16]
    rope_cols = None
    if rope is not None:
        cos, sin, rope_cols = rope

        def rope_index(t):
            return (jnp.where(t < PROMPT_TILES, 0, 1 + (t - PROMPT_TILES) % TILES_PER_SAMPLE), 0)

        in_specs += [pl.BlockSpec((TM, LANE), rope_index), pl.BlockSpec((TM, LANE), rope_index)]
        args += [cos, sin]
    return pl.pallas_call(
        functools.partial(_inproj_kernel, rope_cols=rope_cols),
        out_shape=jax.ShapeDtypeStruct((N_TOK, p), F32),
        grid=(N_TILES,),
        in_specs=in_specs,
        out_specs=pl.BlockSpec((TM, p), lambda t: (t, 0)),
        compiler_params=_params(("parallel",)),
        name="inproj",
    )(*args)


def _rope_tables():
    lane = jnp.arange(LANE)
    d = lane % DH_B
    use_col = (d // 32) == 1
    j = d % 16
    upper = ((d % 32) // 16) == 1
    inv_freq = ROPE_BASE ** (-j.astype(F32) / 16.0)
    t = jnp.arange(DEC_SEQ)
    row = (t // GRID_W).astype(F32)
    col = (t % GRID_W).astype(F32)
    pos = jnp.where(use_col[None, :], col[:, None], row[:, None])
    ang = pos * inv_freq[None, :]
    cos = jnp.cos(ang)
    sin = jnp.where(upper[None, :], jnp.sin(ang), -jnp.sin(ang))
    cos = jnp.concatenate([jnp.ones((TM, LANE), F32), cos], axis=0)
    sin = jnp.concatenate([jnp.zeros((TM, LANE), F32), sin], axis=0)
    return cos, sin


def _route(logits):
    lane = lax.broadcasted_iota(jnp.int32, logits.shape, 1)
    lanef = lane.astype(F32)
    neg = jnp.float32(-jnp.inf)
    gl = jnp.where(lane < N_GROUPS, logits, neg)
    gmax = jnp.max(gl, axis=1, keepdims=True)
    gsel = jnp.min(jnp.where(gl == gmax, lanef, float(LANE)), axis=1, keepdims=True)
    p_grp = 1.0 / jnp.sum(jnp.exp(gl - gmax), axis=1, keepdims=True)
    lo = float(N_GROUPS) + gsel * float(E_PER_GROUP)
    el = jnp.where((lanef >= lo) & (lanef < lo + float(E_PER_GROUP)), logits, neg)
    v1 = jnp.max(el, axis=1, keepdims=True)
    i1 = jnp.min(jnp.where(el == v1, lanef, float(LANE)), axis=1, keepdims=True)
    el2 = jnp.where(lanef == i1, neg, el)
    v2 = jnp.max(el2, axis=1, keepdims=True)
    i2 = jnp.min(jnp.where(el2 == v2, lanef, float(LANE)), axis=1, keepdims=True)
    t = jnp.exp(v2 - v1)
    w1 = p_grp / (1.0 + t)
    w2 = p_grp * t / (1.0 + t)
    out = jnp.where(lane == 0, i1 - float(N_GROUPS), 0.0)
    out = jnp.where(lane == 1, i2 - float(N_GROUPS), out)
    out = jnp.where(lane == 2, w1, out)
    out = jnp.where(lane == 3, w2, out)
    return out


def _outproj_kernel(*refs, widths):
    n_in = len(widths)
    w_ref, x_ref, mod_ref, nw_ref, wr_ref, xn_ref, h2_ref, rt_ref = refs[2 * n_in:]
    is_prompt = pl.program_id(0) < PROMPT_TILES
    y = None
    c0 = 0
    for k, wd in enumerate(widths):
        o = jnp.where(is_prompt, refs[2 * k][...], refs[2 * k + 1][...])
        part = jnp.dot(o.astype(BF16), w_ref[c0:c0 + wd, :], preferred_element_type=F32)
        y = part if y is None else y + part
        c0 += wd
    xn = x_ref[...] + mod_ref[0, 2:3, :] * y
    xn_ref[...] = xn
    h2 = _rms(xn, nw_ref[...]) * (1.0 + mod_ref[0, 4:5, :]) + mod_ref[0, 3:4, :]
    for s in range(ROW_TILE):
        h2_ref[pl.ds(s, TM, stride=ROW_TILE), :] = h2[:, s * LANE:(s + 1) * LANE]
    rt_ref[...] = _route(_dot_f32(h2, wr_ref[...]))


def _outproj(parts, w_bf16, x, mod, norm_w2, w_router):
    widths = tuple(p[0].shape[1] for p in parts)
    in_specs = []
    for wd in widths:
        in_specs += [pl.BlockSpec((TM, wd), lambda t: (jnp.minimum(t, PROMPT_TILES - 1), 0)),
                     pl.BlockSpec((TM, wd), lambda t: (jnp.maximum(t - PROMPT_TILES, 0), 0))]
    parts = [a for p in parts for a in p]
    in_specs += [pl.BlockSpec((D_MODEL, D_MODEL), lambda t: (0, 0)),
                 pl.BlockSpec((TM, D_MODEL), lambda t: (t, 0)),
                 pl.BlockSpec((1, MOD_ROWS, D_MODEL), lambda t: (_mod_index(t), 0, 0)),
                 pl.BlockSpec((1, D_MODEL), lambda t: (0, 0)),
                 pl.BlockSpec((D_MODEL, LANE), lambda t: (0, 0))]
    return pl.pallas_call(
        functools.partial(_outproj_kernel, widths=widths),
        out_shape=(jax.ShapeDtypeStruct((N_TOK, D_MODEL), F32),
                   jax.ShapeDtypeStruct((N_TOK * ROW_TILE, LANE), F32),
                   jax.ShapeDtypeStruct((N_TOK, LANE), F32)),
        grid=(N_TILES,),
        in_specs=in_specs,
        out_specs=(pl.BlockSpec((TM, D_MODEL), lambda t: (t, 0)),
                   pl.BlockSpec((TM * ROW_TILE, LANE), lambda t: (t, 0)),
                   pl.BlockSpec((TM, LANE), lambda t: (t, 0))),
        compiler_params=_params(("parallel",)),
        name="outproj",
    )(*parts, w_bf16, x, mod, norm_w2.reshape(1, D_MODEL), w_router)


def _expert_kernel(blk_e_ref, nact_ref, rowtok_ref, h2_hbm, wg_ref, wu_ref, wd_ref, y_ref,
                   x_buf, sems, wg_s, wu_s, wd_s):
    i = pl.program_id(0)
    n_act = nact_ref[0]
    e = blk_e_ref[i]
    prev = blk_e_ref[jnp.maximum(i - 1, 0)]

    def row_copy(blk, r, slot):
        tok = rowtok_ref[blk * MOE_ROWS + r]
        return pltpu.make_async_copy(
            h2_hbm.at[pl.ds(pl.multiple_of(tok * ROW_TILE, ROW_TILE), ROW_TILE), :],
            x_buf.at[slot, pl.ds(pl.multiple_of(r * ROW_TILE, ROW_TILE), ROW_TILE), :],
            sems.at[slot])

    @pl.when(i == 0)
    def _():
        def start_row(r, carry):
            row_copy(0, r, 0).start()
            return carry
        lax.fori_loop(0, MOE_ROWS, start_row, 0)

    @pl.when(i + 1 < n_act)
    def _():
        for r in range(MOE_ROWS):
            row_copy(i + 1, r, (i + 1) % 2).start()

    @pl.when((i == 0) | (e != prev))
    def _():
        wg_s[...] = wg_ref[...].astype(BF16)
        wu_s[...] = wu_ref[...].astype(BF16)
        wd_s[...] = wd_ref[...].astype(BF16)

    @pl.when(i < n_act)
    def _():
        slot = i % 2
        pltpu.make_async_copy(h2_hbm.at[pl.ds(0, MOE_ROWS * ROW_TILE), :], x_buf.at[slot], sems.at[slot]).wait()
        x = jnp.concatenate([x_buf[slot, pl.ds(s, MOE_ROWS, stride=ROW_TILE), :] for s in range(ROW_TILE)],
                            axis=1).astype(BF16)
        g = jnp.dot(x, wg_s[...], preferred_element_type=F32)
        u = jnp.dot(x, wu_s[...], preferred_element_type=F32)
        y_ref[...] = jnp.dot((_silu(g) * u).astype(BF16), wd_s[...], preferred_element_type=F32)

    @pl.when(i >= nact_ref[0])
    def _():
        y_ref[...] = jnp.zeros_like(y_ref)


def _experts(h2_tiles, row_tok, blk_e, n_active, w_gate, w_up, w_down, layer):
    n_rows = row_tok.shape[0]
    n_blocks = n_rows // MOE_ROWS
    grid_spec = pltpu.PrefetchScalarGridSpec(
        num_scalar_prefetch=3,
        grid=(n_blocks,),
        in_specs=[pl.BlockSpec(memory_space=pl.ANY),
                  pl.BlockSpec((None, None, D_MODEL, D_EXPERT), lambda i, be, na, rt: (layer, be[i], 0, 0)),
                  pl.BlockSpec((None, None, D_MODEL, D_EXPERT), lambda i, be, na, rt: (layer, be[i], 0, 0)),
                  pl.BlockSpec((None, None, D_EXPERT, D_MODEL), lambda i, be, na, rt: (layer, be[i], 0, 0))],
        out_specs=pl.BlockSpec((MOE_ROWS, D_MODEL), lambda i, be, na, rt: (i, 0)),
        scratch_shapes=[pltpu.VMEM((2, MOE_ROWS * ROW_TILE, LANE), F32),
                        pltpu.SemaphoreType.DMA((2,)),
                        pltpu.VMEM((D_MODEL, D_EXPERT), BF16),
                        pltpu.VMEM((D_MODEL, D_EXPERT), BF16),
                        pltpu.VMEM((D_EXPERT, D_MODEL), BF16)])
    return pl.pallas_call(
        _expert_kernel,
        out_shape=jax.ShapeDtypeStruct((n_rows, D_MODEL), F32),
        grid_spec=grid_spec,
        compiler_params=_params(("arbitrary",)),
        name="experts",
    )(blk_e, n_active, row_tok, h2_tiles, w_gate, w_up, w_down)


def _combine_kernel(x_ref, y0_ref, y1_ref, rt_ref, mod_ref, o_ref):
    rt = rt_ref[...]
    y = rt[:, 2:3] * y0_ref[...] + rt[:, 3:4] * y1_ref[...]
    o_ref[...] = x_ref[...] + mod_ref[0, 5:6, :] * y


def _combine(x, y0, y1, route, mod):
    row = pl.BlockSpec((TM, D_MODEL), lambda t: (t, 0))
    return pl.pallas_call(
        _combine_kernel,
        out_shape=jax.ShapeDtypeStruct((N_TOK, D_MODEL), F32),
        grid=(N_TILES,),
        in_specs=[row, row, row,
                  pl.BlockSpec((TM, LANE), lambda t: (t, 0)),
                  pl.BlockSpec((1, MOD_ROWS, D_MODEL), lambda t: (_mod_index(t), 0, 0))],
        out_specs=row,
        compiler_params=_params(("parallel",)),
        name="combine",
    )(x, y0, y1, route, mod)


def _take_rows(a, idx):
    return a.at[idx].get(mode="promise_in_bounds")


def _moe(x, h2, route, mod, w_gate, w_up, w_down, layer):
    n_asg = 2 * N_TOK
    flat_e = route[:, :2].astype(jnp.int32).reshape(n_asg)
    flat_tok = jnp.arange(n_asg, dtype=jnp.int32) // 2
    onehot = (flat_e[:, None] == jnp.arange(N_EXPERTS, dtype=jnp.int32)[None, :]).astype(jnp.int32)
    csum = jnp.cumsum(onehot, axis=0)
    rank = jnp.take_along_axis(csum, flat_e[:, None], axis=1)[:, 0] - 1
    counts = csum[-1]
    padded = (counts + MOE_ROWS - 1) // MOE_ROWS * MOE_ROWS
    pad_end = jnp.cumsum(padded)
    pad_start = pad_end - padded
    dest = pad_start[flat_e] + rank
    n_rows = n_asg + N_EXPERTS * MOE_ROWS
    n_blocks = n_rows // MOE_ROWS
    row_tok = (jnp.arange(n_rows, dtype=jnp.int32) % N_TOK).at[dest].set(
        flat_tok, unique_indices=True, mode="promise_in_bounds")
    blk_start = jnp.arange(n_blocks, dtype=jnp.int32) * MOE_ROWS
    blk_e = jnp.sum((blk_start[:, None] >= pad_end[None, :]).astype(jnp.int32), axis=1)
    blk_e = jnp.minimum(blk_e, N_EXPERTS - 1)
    n_active = (pad_end[-1:] // MOE_ROWS).astype(jnp.int32)
    yb = _experts(h2, row_tok, blk_e, n_active, w_gate, w_up, w_down, layer)
    dest2 = dest.reshape(N_TOK, 2)
    return _combine(x, _take_rows(yb, dest2[:, 0]), _take_rows(yb, dest2[:, 1]), route, mod)


def _final_kernel(x_ref, w_ref, o_ref):
    o_ref[...] = _rms(x_ref[...], w_ref[...])


def _final_norm(x, w, tile0, n_rows):
    return pl.pallas_call(
        _final_kernel,
        out_shape=jax.ShapeDtypeStruct((n_rows, D_MODEL), F32),
        grid=(n_rows // TM,),
        in_specs=[pl.BlockSpec((TM, D_MODEL), lambda t: (t + tile0, 0)),
                  pl.BlockSpec((1, D_MODEL), lambda t: (0, 0))],
        out_specs=pl.BlockSpec((TM, D_MODEL), lambda t: (t, 0)),
        compiler_params=_params(("parallel",)),
        name="final_norm",
    )(x, w.reshape(1, D_MODEL))


def _hgrn_kernel(*refs, seq_len, layer, context):
    if context:
        (q_ref, ff_ref, fb_ref, v_ref, g_ref, lbl_ref, nw_ref,
         o_ref, sfin_ref, of_s, ob_s, sf_s, sb_s) = refs
    else:
        (q_ref, ff_ref, fb_ref, v_ref, g_ref, lbl_ref, nw_ref, s0_ref,
         o_ref, of_s, ob_s, sf_s, sb_s) = refs
    c = CHUNK_A
    n = seq_len // c

    def lower_bound(d):
        z = lbl_ref[d]
        e = jnp.exp(z - jnp.max(z, axis=0, keepdims=True))
        return jnp.sum(e[:layer + 1], axis=0, keepdims=True) / jnp.sum(e, axis=0, keepdims=True)

    lb_f = lower_bound(0)
    lb_b = lower_bound(1)
    row = lax.broadcasted_iota(jnp.int32, (c, c), 0)
    col = lax.broadcasted_iota(jnp.int32, (c, c), 1)
    causal = row >= col
    tri_f = causal.astype(F32)
    tri_b = (row <= col).astype(F32)

    if context:
        sf_s[...] = jnp.zeros_like(sf_s)
        sb_s[...] = jnp.zeros_like(sb_s)
    else:
        sf_s[...] = s0_ref[0, 0, 0].T
        sb_s[...] = s0_ref[0, 1, 0].T

    group = 4
    dirs = ((ff_ref, lb_f, tri_f, causal, c - 1, sf_s, of_s),
            (fb_ref, lb_b, tri_b, row <= col, 0, sb_s, ob_s))

    def body(i, carry):
        items = []
        for g in range(group):
            j = i * group + g
            items.append((0, pl.ds(pl.multiple_of(j * c, c), c)))
            items.append((1, pl.ds(pl.multiple_of((n - 1 - j) * c, c), c)))
        fs = [dirs[d][1] + (1.0 - dirs[d][1]) * _sigmoid(dirs[d][0][rows, :]) for d, rows in items]
        bs = [_tri_matmul(dirs[d][2], jnp.log(f)) for (d, _), f in zip(items, fs)]
        b_lasts = [b[dirs[d][4]:dirs[d][4] + 1, :] for (d, _), b in zip(items, bs)]
        q_ins = [_silu(q_ref[rows, :]) * jnp.exp(b) for (_, rows), b in zip(items, bs)]
        a_s = [jnp.where(dirs[d][3], _dot_nt(q_in, (1.0 - f) * jnp.exp(-b)), 0.0)
               for (d, _), q_in, f, b in zip(items, q_ins, fs, bs)]
        vs = [v_ref[rows, :] for _, rows in items]
        o_intra = [_dot(a, v) for a, v in zip(a_s, vs)]
        u_ts = [_dot_tn(v, (1.0 - f) * jnp.exp(b_last - b)) for v, f, b, b_last in zip(vs, fs, bs, b_lasts)]
        for d in (0, 1):
            st_ref, out_s = dirs[d][5], dirs[d][6]
            st = st_ref[...]
            for k, (dk, rows) in enumerate(items):
                if dk == d:
                    out_s[rows, :] = o_intra[k] + _dot_nt(q_ins[k], st)
                    st = st * jnp.exp(b_lasts[k]) + u_ts[k]
            st_ref[...] = st
        return carry

    lax.fori_loop(0, n // group, body, 0)

    if context:
        sfin_ref[0, 0, 0] = sf_s[...].T
        sfin_ref[0, 1, 0] = sb_s[...].T

    nw = nw_ref[...]

    def epilogue(j, carry):
        rows = pl.ds(pl.multiple_of(j * TM, TM), TM)
        o = of_s[rows, :] + ob_s[rows, :]
        o_ref[rows, :] = _rms(o, nw) * _silu(g_ref[rows, :])
        return carry

    lax.fori_loop(0, seq_len // TM, epilogue, 0)


def _hgrn(proj, lb_logits, norm_w, layer, context, s0=None):
    seq_len = SEQ if context else DEC_SEQ
    bsz = BATCH if context else DEC_BATCH
    row0 = 0 if context else N_PROMPT // seq_len

    def col(k):
        return pl.BlockSpec((seq_len, LANE), lambda b, h: (row0 + b, k * H_A + h))

    in_specs = [col(0), col(1), col(2), col(3), col(4),
                pl.BlockSpec((2, DEPTH + 1, LANE), lambda b, h: (0, 0, h)),
                pl.BlockSpec((1, LANE), lambda b, h: (0, 0))]
    args = [proj, proj, proj, proj, proj, lb_logits, norm_w.reshape(1, LANE)]
    o_shape = jax.ShapeDtypeStruct((bsz * seq_len, W_A), F32)
    o_spec = pl.BlockSpec((seq_len, LANE), lambda b, h: (b, h))
    scratch = [pltpu.VMEM((seq_len, LANE), F32), pltpu.VMEM((seq_len, LANE), F32),
               pltpu.VMEM((LANE, LANE), F32), pltpu.VMEM((LANE, LANE), F32)]
    kern = functools.partial(_hgrn_kernel, seq_len=seq_len, layer=layer, context=context)
    if context:
        return pl.pallas_call(
            kern,
            out_shape=(o_shape, jax.ShapeDtypeStruct((bsz, 2, H_A, DK_A, DK_A), F32)),
            grid=(bsz, H_A),
            in_specs=in_specs,
            out_specs=(o_spec, pl.BlockSpec((1, 2, 1, DK_A, DK_A), lambda b, h: (b, 0, h, 0, 0))),
            scratch_shapes=scratch,
            compiler_params=_params(("parallel", "parallel")),
            name="hgrn_ctx",
        )(*args)
    in_specs += [pl.BlockSpec((1, 2, 1, DK_A, DK_A), lambda b, h: (b, 0, h, 0, 0))]
    args += [s0]
    return pl.pallas_call(
        kern,
        out_shape=o_shape,
        grid=(bsz, H_A),
        in_specs=in_specs,
        out_specs=o_spec,
        scratch_shapes=scratch,
        compiler_params=_params(("parallel", "parallel")),
        name="hgrn_dec",
    )(*args)


ATT_TQ = 256
COL_QD = 5 * W_A // LANE
COL_KD = COL_QD + QK_B // LANE
COL_VD = COL_KD + QK_B // LANE


def _attn_kernel(*refs, seq_len, layer, context):
    if context:
        q_ref, k_ref, v_ref, lam_ref, nw_ref, o_ref, k_s, v_s = refs
    else:
        q_ref, k_ref, v_ref, ck_ref, cv_ref, lam_ref, nw_ref, o_ref, k_s, v_s = refs

    @pl.when(pl.program_id(2) == 0)
    def _():
        k_s[0:seq_len, :] = k_ref[...].astype(BF16)
        v_s[0:seq_len, :] = v_ref[...].astype(BF16)
        if not context:
            k_s[seq_len:, :] = ck_ref[0].astype(BF16)
            v_s[seq_len:, :] = cv_ref[0].astype(BF16)

    lam_init = 0.8 - 0.6 * math.exp(-0.3 * layer)
    lp = lam_ref[...]
    lam = (jnp.exp(jnp.sum(lp[0:1] * lp[1:2], axis=1, keepdims=True))
           - jnp.exp(jnp.sum(lp[2:3] * lp[3:4], axis=1, keepdims=True)) + lam_init)

    q = q_ref[...] * (DH_B ** -0.5)
    lane = lax.broadcasted_iota(jnp.int32, q.shape, 1)
    k = k_s[...]

    def softmax_map(first):
        s = _dot_nt(jnp.where((lane < DH_B) == first, q, 0.0), k)
        p = jnp.exp(s - jnp.max(s, axis=1, keepdims=True))
        return p, jnp.sum(p, axis=1, keepdims=True)

    p0, l0 = softmax_map(True)
    p1, l1 = softmax_map(False)
    a = p0 - (lam * l0 / l1) * p1
    o = jnp.dot(a.astype(BF16), v_s[...], preferred_element_type=F32) / l0
    o_ref[...] = _rms(o, nw_ref[...]) * (1.0 - lam_init)


def _attn(proj, lam_p, norm_w, layer, context, cache_k=None, cache_v=None):
    seq_len = SEQ if context else DEC_SEQ
    bsz = BATCH if context else DEC_BATCH
    row0 = 0 if context else N_PROMPT // seq_len
    nq = seq_len // ATT_TQ
    tile0 = row0 * nq
    t_k = seq_len if context else seq_len + PAST_LEN
    in_specs = [pl.BlockSpec((ATT_TQ, LANE), lambda b, h, i: (tile0 + b * nq + i, COL_QD + h)),
                pl.BlockSpec((seq_len, LANE), lambda b, h, i: (row0 + b, COL_KD + h)),
                pl.BlockSpec((seq_len, LANE), lambda b, h, i: (row0 + b, COL_VD + h))]
    args = [proj, proj, proj]
    if not context:
        in_specs += [pl.BlockSpec((1, PAST_LEN, LANE), lambda b, h, i: (b, 0, h)),
                     pl.BlockSpec((1, PAST_LEN, LANE), lambda b, h, i: (b, 0, h))]
        args += [cache_k, cache_v]
    in_specs += [pl.BlockSpec((4, DH_B), lambda b, h, i: (0, 0)),
                 pl.BlockSpec((1, LANE), lambda b, h, i: (0, 0))]
    args += [lam_p, norm_w.reshape(1, LANE)]
    return pl.pallas_call(
        functools.partial(_attn_kernel, seq_len=seq_len, layer=layer, context=context),
        out_shape=jax.ShapeDtypeStruct((bsz * seq_len, W_B), F32),
        grid=(bsz, H_B, nq),
        in_specs=in_specs,
        out_specs=pl.BlockSpec((ATT_TQ, LANE), lambda b, h, i: (b * nq + i, h)),
        scratch_shapes=[pltpu.VMEM((t_k, LANE), BF16), pltpu.VMEM((t_k, LANE), BF16)],
        compiler_params=_params(("parallel", "parallel", "arbitrary")),
        name="attn_ctx" if context else "attn_dec",
    )(*args)


COL_GATES = 4 * W_C // LANE


def _gdn_gates_kernel(g_ref, alog_ref, dtb_ref, col_ref, row_ref):
    c = GDN_CHUNK
    raw = g_ref[...]
    z = raw + dtb_ref[...]
    softplus = jnp.maximum(z, 0.0) + jnp.log(1.0 + jnp.exp(-jnp.abs(z)))
    g = -jnp.exp(alog_ref[...]) * softplus
    row = lax.broadcasted_iota(jnp.int32, (c, c), 0)
    col = lax.broadcasted_iota(jnp.int32, (c, c), 1)
    lane = lax.broadcasted_iota(jnp.int32, (c, LANE), 1)
    prefix = _tri_matmul((row >= col).astype(F32), g)
    suffix = _tri_matmul((row <= col).astype(F32), g)
    out = jnp.where(lane < H_C, prefix, jnp.where(lane < 2 * H_C, suffix, _sigmoid(raw)))
    col_ref[...] = out
    row_ref[0] = out.T


def _gdn_gates(proj, alog_lane, dtb_lane):
    c = GDN_CHUNK
    n = N_TOK // c
    return pl.pallas_call(
        _gdn_gates_kernel,
        out_shape=(jax.ShapeDtypeStruct((N_TOK, LANE), F32), jax.ShapeDtypeStruct((n, LANE, c), F32)),
        grid=(n,),
        in_specs=[pl.BlockSpec((c, LANE), lambda i: (i, COL_GATES)),
                  pl.BlockSpec((1, LANE), lambda i: (0, 0)),
                  pl.BlockSpec((1, LANE), lambda i: (0, 0))],
        out_specs=(pl.BlockSpec((c, LANE), lambda i: (i, 0)),
                   pl.BlockSpec((1, LANE, c), lambda i: (i, 0, 0))),
        compiler_params=_params(("parallel",)),
        name="gdn_gates",
    )(proj, alog_lane, dtb_lane)


def _unit_tri_inverse_pairs(ms, row, col):
    c = ms[0].shape[0]
    left = lax.broadcasted_iota(jnp.int32, ms[0].shape, 1) < c

    def mm(xs, ys):
        out = []
        for x, y in zip(xs, ys):
            yb = y.astype(BF16)
            zero = jnp.zeros_like(yb)
            blockdiag = jnp.concatenate([jnp.where(left, yb, zero), jnp.where(left, zero, yb)], axis=0)
            out.append(jnp.dot(x.astype(BF16), blockdiag, preferred_element_type=F32))
        return out

    def add(xs, ys):
        return [x + y for x, y in zip(xs, ys)]

    eye = (row == col).astype(F32)
    a = [jnp.where((row // 16) == (col // 16), m, 0.0) for m in ms]
    a2 = mm(a, a)
    a4 = mm(a2, a2)
    a8 = mm(a4, a4)
    t = [eye - x for x in a]
    t = add(t, mm(t, a2))
    t = add(t, mm(t, a4))
    t = add(t, mm(t, a8))
    blk = 32
    while blk <= c:
        off = ((row // blk) == (col // blk)) & ((row // (blk // 2)) != (col // (blk // 2)))
        corr = mm(mm(t, [jnp.where(off, m, 0.0) for m in ms]), t)
        t = [x - y for x, y in zip(t, corr)]
        blk *= 2
    return t


def _gdn_kernel(*refs, seq_len, context):
    if context:
        (q_ref, k_ref, v_ref, go_ref, cwq_ref, cwk_ref, cwv_ref, gcol_ref, grow_ref, nw_ref,
         o_ref, sfin_ref, u_s, wq_s, qk_s, kd_s, of_s, ob_s, sf_s, sb_s) = refs
    else:
        (q_ref, k_ref, v_ref, go_ref, cwq_ref, cwk_ref, cwv_ref, gcol_ref, grow_ref, nw_ref, s0_ref,
         o_ref, u_s, wq_s, qk_s, kd_s, of_s, ob_s, sf_s, sb_s) = refs
    c = GDN_CHUNK
    n = seq_len // c
    head = pl.program_id(1)
    row = lax.broadcasted_iota(jnp.int32, (c, c), 0)
    col = lax.broadcasted_iota(jnp.int32, (c, c), 1)
    row2 = lax.broadcasted_iota(jnp.int32, (c, 2 * c), 0)
    col2 = lax.broadcasted_iota(jnp.int32, (c, 2 * c), 1) & (c - 1)
    lane = lax.broadcasted_iota(jnp.int32, (c, LANE), 1)
    rowi = lax.broadcasted_iota(jnp.int32, (c, LANE), 0)

    def gc_row_last(ci, d):
        gc_row = grow_ref[ci, pl.ds(d * H_C + head, 1), :]
        return gc_row, (gc_row[:, c - 1:c] if d == 0 else gc_row[:, 0:1])

    def chunk_inputs(ci):
        r0 = pl.multiple_of(ci * c, c)
        rows = pl.ds(r0, c)

        def conv(x_ref, w_ref):
            cur = x_ref[rows, :]
            before = x_ref[pl.ds(pl.multiple_of(jnp.maximum(r0 - 8, 0), 8), 8), :]
            after = x_ref[pl.ds(pl.multiple_of(jnp.minimum(r0 + c, seq_len - 8), 8), 8), :]
            prev_row = jnp.where(ci > 0, before[7:8, :], 0.0)
            next_row = jnp.where(ci < n - 1, after[0:1, :], 0.0)
            xm1 = jnp.where(rowi == 0, prev_row, pltpu.roll(cur, 1, 0))
            xp1 = jnp.where(rowi == c - 1, next_row, pltpu.roll(cur, c - 1, 0))
            w = w_ref[...]
            return _silu(xm1 * w[0:1, :] + cur * w[1:2, :] + xp1 * w[2:3, :])

        q = conv(q_ref, cwq_ref)
        k = conv(k_ref, cwk_ref)
        vn = conv(v_ref, cwv_ref)
        qn = q * lax.rsqrt(jnp.sum(q * q, axis=1, keepdims=True) + EPS) * (DK_C ** -0.5)
        kn = k * lax.rsqrt(jnp.sum(k * k, axis=1, keepdims=True) + EPS)
        kq = _dot_nt(jnp.concatenate([kn, qn], axis=0), kn)
        gates = gcol_ref[rows, :]
        per_dir = []
        ms = []
        for d in (0, 1):
            gc = jnp.sum(jnp.where(lane == d * H_C + head, gates, 0.0), axis=1, keepdims=True)
            beta = jnp.sum(jnp.where(lane == (2 + d) * H_C + head, gates, 0.0), axis=1, keepdims=True)
            gc_row, gc_last = gc_row_last(ci, d)
            incl = (row >= col) if d == 0 else (row <= col)
            strict = (row > col) if d == 0 else (row < col)
            decay = jnp.where(incl, jnp.exp(gc - gc_row), 0.0)
            m = jnp.where(strict, kq[:c] * beta * decay, 0.0)
            qk_s[d, rows, :] = (kq[c:] * decay).astype(BF16)
            kd_s[d, rows, :] = (kn * jnp.exp(gc_last - gc)).astype(BF16)
            per_dir.append((gc, beta))
            ms.append(m)
        return jnp.concatenate(ms, axis=1), (rows, qn, kn, vn, per_dir)

    def chunk_outputs(ci, t, rest):
        rows, qn, kn, vn, per_dir = rest
        base = pl.multiple_of(ci * 2 * c, 2 * c)
        for d in (0, 1):
            gc, beta = per_dir[d]
            e = jnp.exp(gc)
            uw = _dot(t[:, d * c:(d + 1) * c], jnp.concatenate([vn * beta, kn * (beta * e)], axis=1))
            u_s[d, rows, :] = uw[:, :LANE]
            wq_s[d, pl.ds(base, c), :] = uw[:, LANE:].astype(BF16)
            wq_s[d, pl.ds(base + c, c), :] = (qn * e).astype(BF16)

    group = min(4, n)

    def prepare(j, carry):
        cis = [j * group + g for g in range(group)]
        staged = [chunk_inputs(ci) for ci in cis]
        ts = _unit_tri_inverse_pairs([s[0] for s in staged], row2, col2)
        for ci, t, s in zip(cis, ts, staged):
            chunk_outputs(ci, t, s[1])
        return carry

    lax.fori_loop(0, n // group, prepare, 0)

    if context:
        sf_s[...] = jnp.zeros_like(sf_s)
        sb_s[...] = jnp.zeros_like(sb_s)
    else:
        sf_s[...] = s0_ref[0, 0, 0]
        sb_s[...] = s0_ref[0, 1, 0]

    def advance(ci, d, s_ref, out_s):
        rows = pl.ds(pl.multiple_of(ci * c, c), c)
        _, gc_last = gc_row_last(ci, d)
        s = s_ref[...]
        ws = jnp.dot(wq_s[d, pl.ds(pl.multiple_of(ci * 2 * c, 2 * c), 2 * c), :], s.astype(BF16),
                     preferred_element_type=F32)
        v_new = (u_s[d, rows, :] - ws[:c]).astype(BF16)
        out_s[rows, :] = ws[c:] + jnp.dot(qk_s[d, rows, :], v_new, preferred_element_type=F32)
        s_ref[...] = s * jnp.exp(gc_last) + lax.dot_general(
            kd_s[d, rows, :], v_new, (((0,), (0,)), ((), ())), preferred_element_type=F32)

    def body(i, carry):
        advance(i, 0, sf_s, of_s)
        advance(n - 1 - i, 1, sb_s, ob_s)
        return carry

    lax.fori_loop(0, n, body, 0)

    if context:
        sfin_ref[0, 0, 0] = sf_s[...]
        sfin_ref[0, 1, 0] = sb_s[...]

    nw = nw_ref[...]

    def epilogue(j, carry):
        rows = pl.ds(pl.multiple_of(j * TM, TM), TM)
        o = of_s[rows, :] + ob_s[rows, :]
        o_ref[rows, :] = _rms(o, nw) * _silu(go_ref[rows, :])
        return carry

    lax.fori_loop(0, seq_len // TM, epilogue, 0)


def _gdn(proj, conv_w, gcol, grow, norm_w, context, s0=None):
    seq_len = SEQ if context else DEC_SEQ
    bsz = BATCH if context else DEC_BATCH
    row0 = 0 if context else N_PROMPT // seq_len
    nc = seq_len // GDN_CHUNK

    def col(k):
        return pl.BlockSpec((seq_len, LANE), lambda b, h: (row0 + b, k * H_C + h))

    def cw(k):
        return pl.BlockSpec((3, LANE), lambda b, h: (0, k * H_C + h))

    in_specs = [col(0), col(1), col(2), col(3), cw(0), cw(1), cw(2),
                pl.BlockSpec((seq_len, LANE), lambda b, h: (row0 + b, 0)),
                pl.BlockSpec((nc, LANE, GDN_CHUNK), lambda b, h: (row0 + b, 0, 0)),
                pl.BlockSpec((1, LANE), lambda b, h: (0, 0))]
    args = [proj, proj, proj, proj, conv_w, conv_w, conv_w, gcol, grow, norm_w.reshape(1, LANE)]
    o_shape = jax.ShapeDtypeStruct((bsz * seq_len, W_C), F32)
    o_spec = pl.BlockSpec((seq_len, LANE), lambda b, h: (b, h))
    seq_buf = pltpu.VMEM((seq_len, LANE), F32)
    state_buf = pltpu.VMEM((DK_C, DK_C), F32)
    dir_bf16 = pltpu.VMEM((2, seq_len, LANE), BF16)
    scratch = [pltpu.VMEM((2, seq_len, LANE), F32), pltpu.VMEM((2, 2 * seq_len, LANE), BF16), dir_bf16, dir_bf16,
               seq_buf, seq_buf, state_buf, state_buf]
    kern = functools.partial(_gdn_kernel, seq_len=seq_len, context=context)
    state_spec = pl.BlockSpec((1, 2, 1, DK_C, DK_C), lambda b, h: (b, 0, h, 0, 0))
    if context:
        return pl.pallas_call(
            kern,
            out_shape=(o_shape, jax.ShapeDtypeStruct((bsz, 2, H_C, DK_C, DK_C), F32)),
            grid=(bsz, H_C),
            in_specs=in_specs,
            out_specs=(o_spec, state_spec),
            scratch_shapes=scratch,
            compiler_params=_params(("parallel", "parallel")),
            name="gdn_ctx",
        )(*args)
    in_specs += [state_spec]
    args += [s0]
    return pl.pallas_call(
        kern,
        out_shape=o_shape,
        grid=(bsz, H_C),
        in_specs=in_specs,
        out_specs=o_spec,
        scratch_shapes=scratch,
        compiler_params=_params(("parallel", "parallel")),
        name="gdn_dec",
    )(*args)


def kernel(x_prompt, x_sample, c, cache_diff_k, cache_diff_v, state_hgrn, state_gdn, c_ctx,
           ada_w, ada_b, norm_w, final_norm_w, w_in_ab, hgrn_lb_logits, hgrn_norm_w,
           diff_lambda, diff_norm_w, w_in_c, gdn_conv_w, gdn_a_log, gdn_dt_bias, gdn_norm_w,
           w_out, moe_router_group, moe_router_expert, moe_w_gate, moe_w_up, moe_w_down):
    x = jnp.concatenate([x_prompt.reshape(N_PROMPT, D_MODEL), x_sample.reshape(N_SAMPLE, D_MODEL)], axis=0)
    cvec = jnp.concatenate([c_ctx[None, :], c, jnp.zeros((MOD_ROWS - N_MOD, D_MODEL), F32)], axis=0)
    mod_all = _adaln(cvec, ada_w, ada_b)[:, :N_MOD].reshape(DEPTH, N_MOD, 6, D_MODEL)
    mod_all = jnp.pad(mod_all, ((0, 0), (0, 0), (0, MOD_ROWS - 6), (0, 0)))
    cos, sin = _rope_tables()
    new_k, new_v, new_hgrn, new_gdn = [], [], [], []
    for l in range(DEPTH):
        i = l // 2
        mod = mod_all[l]
        if l % 2 == 0:
            proj = _inproj(x, mod, norm_w[l, 0], w_in_ab[i].astype(BF16),
                           rope=(cos, sin, (COL_QD * LANE, COL_VD * LANE)))
            oh_ctx, s_h = _hgrn(proj, hgrn_lb_logits, hgrn_norm_w[i], l, True)
            oh_dec = _hgrn(proj, hgrn_lb_logits, hgrn_norm_w[i], l, False, s0=state_hgrn[:, i])
            od_ctx = _attn(proj, diff_lambda[i], diff_norm_w[i], l, True)
            od_dec = _attn(proj, diff_lambda[i], diff_norm_w[i], l, False,
                           cache_k=cache_diff_k[:, i].reshape(DEC_BATCH, PAST_LEN, QK_B),
                           cache_v=cache_diff_v[:, i].reshape(DEC_BATCH, PAST_LEN, W_B))
            parts = ((oh_ctx, oh_dec), (od_ctx, od_dec))
            new_k.append(proj[:N_PROMPT, COL_KD * LANE:COL_VD * LANE].reshape(BATCH, SEQ, H_B, 2, DH_B))
            new_v.append(proj[:N_PROMPT, COL_VD * LANE:].reshape(BATCH, SEQ, H_B, DV_B))
            new_hgrn.append(s_h)
        else:
            w_c = jnp.pad(w_in_c[i], ((0, 0), (0, (COL_GATES + 1) * LANE - IN_C))).astype(BF16)
            proj = _inproj(x, mod, norm_w[l, 0], w_c)
            pad = jnp.zeros((LANE - 2 * H_C,), F32)
            alog_lane = jnp.concatenate([gdn_a_log[i, 0], gdn_a_log[i, 1], pad]).reshape(1, LANE)
            dtb_lane = jnp.concatenate([gdn_dt_bias[i, 0], gdn_dt_bias[i, 1], pad]).reshape(1, LANE)
            gcol, grow = _gdn_gates(proj, alog_lane, dtb_lane)
            oc_ctx, s_c = _gdn(proj, gdn_conv_w[i], gcol, grow, gdn_norm_w[i], True)
            oc_dec = _gdn(proj, gdn_conv_w[i], gcol, grow, gdn_norm_w[i], False, s0=state_gdn[:, i])
            parts = ((oc_ctx, oc_dec),)
            new_gdn.append(s_c)
        w_router = jnp.concatenate(
            [moe_router_group[l], moe_router_expert[l],
             jnp.zeros((D_MODEL, LANE - N_GROUPS - N_EXPERTS), F32)], axis=1)
        x, h2, route = _outproj(parts, w_out[l].astype(BF16), x, mod, norm_w[l, 1], w_router)
        x = _moe(x, h2, route, mod, moe_w_gate, moe_w_up, moe_w_down, l)
    y_prompt = _final_norm(x, final_norm_w, 0, N_PROMPT).reshape(BATCH, SEQ, D_MODEL)
    y_sample = _final_norm(x, final_norm_w, PROMPT_TILES, N_SAMPLE).reshape(DEC_BATCH, DEC_SEQ, D_MODEL)
    return (y_prompt, y_sample, jnp.stack(new_k, axis=1), jnp.stack(new_v, axis=1),
            jnp.stack(new_hgrn, axis=1), jnp.stack(new_gdn, axis=1))
```

```python
import functools
import math

import jax
import jax.numpy as jnp
from jax import lax
from jax.experimental import pallas as pl
from jax.experimental.pallas import tpu as pltpu

F32 = jnp.float32
BF16 = jnp.bfloat16
HIGHEST = lax.Precision.HIGHEST

D_MODEL = 1024
BATCH = 16
SEQ = 256
DEPTH = 2
DEC_BATCH = 4
DEC_SEQ = 4096
PAST_LEN = 256
GRID_W = 64
H_A = 4
DK_A = 128
W_A = 512
CHUNK_A = 32
H_B = 4
DH_B = 64
DV_B = 128
QK_B = 512
W_B = 512
ROPE_BASE = 10000.0
H_C = 8
DK_C = 128
W_C = 1024
N_GROUPS = 4
E_PER_GROUP = 8
N_EXPERTS = 32
D_EXPERT = 512
EPS = 1e-6
IN_AB = 5 * W_A + 2 * QK_B + W_B
IN_C = 4 * W_C + 4 * H_C

LANE = 128
N_PROMPT = BATCH * SEQ
N_SAMPLE = DEC_BATCH * DEC_SEQ
N_TOK = N_PROMPT + N_SAMPLE
TM = 256
N_TILES = N_TOK // TM
PROMPT_TILES = N_PROMPT // TM
TILES_PER_SAMPLE = DEC_SEQ // TM
N_MOD = 1 + DEC_BATCH
MOD_ROWS = 8
MOE_ROWS = 256
ROW_TILE = D_MODEL // LANE
GDN_CHUNK = 128
VMEM_LIMIT = 56 * 1024 * 1024


def _mod_index(t):
    return jnp.where(t < PROMPT_TILES, 0, 1 + (t - PROMPT_TILES) // TILES_PER_SAMPLE)


def _sigmoid(x):
    return 1.0 / (1.0 + jnp.exp(-x))


def _silu(x):
    return x * _sigmoid(x)


def _rms(x, w):
    return x * lax.rsqrt(jnp.mean(x * x, axis=-1, keepdims=True) + EPS) * w


def _dot(a, b):
    return jnp.dot(a.astype(BF16), b.astype(BF16), preferred_element_type=F32)


def _dot_nt(a, b):
    return lax.dot_general(a.astype(BF16), b.astype(BF16), (((1,), (1,)), ((), ())),
                           preferred_element_type=F32)


def _dot_tn(a, b):
    return lax.dot_general(a.astype(BF16), b.astype(BF16), (((0,), (0,)), ((), ())),
                           preferred_element_type=F32)


def _dot_f32(a, b):
    return jnp.dot(a, b, precision=HIGHEST, preferred_element_type=F32)


def _tri_matmul(tri, x):
    hi = x.astype(BF16)
    rem = x - hi.astype(F32)
    mid = rem.astype(BF16)
    lo = (rem - mid.astype(F32)).astype(BF16)
    n = x.shape[1]
    r = jnp.dot(tri.astype(BF16), jnp.concatenate([hi, mid, lo], axis=1), preferred_element_type=F32)
    return r[:, :n] + r[:, n:2 * n] + r[:, 2 * n:]


def _params(sem):
    return pltpu.CompilerParams(dimension_semantics=sem, vmem_limit_bytes=VMEM_LIMIT)


def _adaln_kernel(c_ref, w_ref, b_ref, o_ref):
    s = _silu(c_ref[...])
    o_ref[0] = _dot(s, w_ref[0]) + b_ref[0]


def _adaln(cvec, ada_w, ada_b):
    nb = 4
    wb = 6 * D_MODEL // nb
    return pl.pallas_call(
        _adaln_kernel,
        out_shape=jax.ShapeDtypeStruct((DEPTH, MOD_ROWS, 6 * D_MODEL), F32),
        grid=(DEPTH, nb),
        in_specs=[pl.BlockSpec((MOD_ROWS, D_MODEL), lambda l, j: (0, 0)),
                  pl.BlockSpec((1, D_MODEL, wb), lambda l, j: (l, 0, j)),
                  pl.BlockSpec((1, 1, wb), lambda l, j: (l, 0, j))],
        out_specs=pl.BlockSpec((1, MOD_ROWS, wb), lambda l, j: (l, 0, j)),
        compiler_params=_params(("parallel", "parallel")),
        name="adaln",
    )(cvec, ada_w, ada_b.reshape(DEPTH, 1, 6 * D_MODEL))


def _inproj_kernel(*refs, rope_cols):
    if rope_cols is None:
        x_ref, mod_ref, nw_ref, w_ref, o_ref = refs
    else:
        x_ref, mod_ref, nw_ref, w_ref, cos_ref, sin_ref, o_ref = refs
    h = _rms(x_ref[...], nw_ref[...]) * (1.0 + mod_ref[0, 1:2, :]) + mod_ref[0, 0:1, :]
    r = jnp.dot(h.astype(BF16), w_ref[...], preferred_element_type=F32)
    if rope_cols is None:
        o_ref[...] = r
        return
    lo, hi = rope_cols
    o_ref[:, :lo] = r[:, :lo]
    o_ref[:, hi:] = r[:, hi:]
    cos = cos_ref[...]
    sin = sin_ref[...]
    lane = lax.broadcasted_iota(jnp.int32, (TM, LANE), 1)
    upper = (lane & 16) != 0
    for c0 in range(lo, hi, LANE):
        v = r[:, c0:c0 + LANE]
        partner = jnp.where(upper, pltpu.roll(v, 16, 1), pltpu.roll(v, LANE - 16, 1))
        o_ref[:, c0:c0 + LANE] = v * cos + partner * sin


def _inproj(x, mod, norm_w, w_bf16, rope=None):
    p = w_bf16.shape[1]
    in_specs = [pl.BlockSpec((TM, D_MODEL), lambda t: (t, 0)),
                pl.BlockSpec((1, MOD_ROWS, D_MODEL), lambda t: (_mod_index(t), 0, 0)),
                pl.BlockSpec((1, D_MODEL), lambda t: (0, 0)),
                pl.BlockSpec((D_MODEL, p), lambda t: (0, 0))]
    args = [x, mod, norm_w.reshape(1, D_MODEL), w_bf16]
    rope_cols = None
    if rope is not None:
        cos, sin, rope_cols = rope

        def rope_index(t):
            return (jnp.where(t < PROMPT_TILES, 0, 1 + (t - PROMPT_TILES) % TILES_PER_SAMPLE), 0)

        in_specs += [pl.BlockSpec((TM, LANE), rope_index), pl.BlockSpec((TM, LANE), rope_index)]
        args += [cos, sin]
    return pl.pallas_call(
        functools.partial(_inproj_kernel, rope_cols=rope_cols),
        out_shape=jax.ShapeDtypeStruct((N_TOK, p), F32),
        grid=(N_TILES,),
        in_specs=in_specs,
        out_specs=pl.BlockSpec((TM, p), lambda t: (t, 0)),
        compiler_params=_params(("parallel",)),
        name="inproj",
    )(*args)


def _rope_tables():
    lane = jnp.arange(LANE)
    d = lane % DH_B
    use_col = (d // 32) == 1
    j = d % 16
    upper = ((d % 32) // 16) == 1
    inv_freq = ROPE_BASE ** (-j.astype(F32) / 16.0)
    t = jnp.arange(DEC_SEQ)
    row = (t // GRID_W).astype(F32)
    col = (t % GRID_W).astype(F32)
    pos = jnp.where(use_col[None, :], col[:, None], row[:, None])
    ang = pos * inv_freq[None, :]
    cos = jnp.cos(ang)
    sin = jnp.where(upper[None, :], jnp.sin(ang), -jnp.sin(ang))
    cos = jnp.concatenate([jnp.ones((TM, LANE), F32), cos], axis=0)
    sin = jnp.concatenate([jnp.zeros((TM, LANE), F32), sin], axis=0)
    return cos, sin


def _route(logits):
    lane = lax.broadcasted_iota(jnp.int32, logits.shape, 1)
    lanef = lane.astype(F32)
    neg = jnp.float32(-jnp.inf)
    gl = jnp.where(lane < N_GROUPS, logits, neg)
    gmax = jnp.max(gl, axis=1, keepdims=True)
    gsel = jnp.min(jnp.where(gl == gmax, lanef, float(LANE)), axis=1, keepdims=True)
    p_grp = 1.0 / jnp.sum(jnp.exp(gl - gmax), axis=1, keepdims=True)
    lo = float(N_GROUPS) + gsel * float(E_PER_GROUP)
    el = jnp.where((lanef >= lo) & (lanef < lo + float(E_PER_GROUP)), logits, neg)
    v1 = jnp.max(el, axis=1, keepdims=True)
    i1 = jnp.min(jnp.where(el == v1, lanef, float(LANE)), axis=1, keepdims=True)
    el2 = jnp.where(lanef == i1, neg, el)
    v2 = jnp.max(el2, axis=1, keepdims=True)
    i2 = jnp.min(jnp.where(el2 == v2, lanef, float(LANE)), axis=1, keepdims=True)
    t = jnp.exp(v2 - v1)
    w1 = p_grp / (1.0 + t)
    w2 = p_grp * t / (1.0 + t)
    out = jnp.where(lane == 0, i1 - float(N_GROUPS), 0.0)
    out = jnp.where(lane == 1, i2 - float(N_GROUPS), out)
    out = jnp.where(lane == 2, w1, out)
    out = jnp.where(lane == 3, w2, out)
    return out


def _outproj_kernel(*refs, widths):
    n_in = len(widths)
    w_ref, x_ref, mod_ref, nw_ref, wr_ref, xn_ref, h2_ref, rt_ref = refs[2 * n_in:]
    is_prompt = pl.program_id(0) < PROMPT_TILES
    y = None
    c0 = 0
    for k, wd in enumerate(widths):
        o = jnp.where(is_prompt, refs[2 * k][...], refs[2 * k + 1][...])
        part = jnp.dot(o.astype(BF16), w_ref[c0:c0 + wd, :], preferred_element_type=F32)
        y = part if y is None else y + part
        c0 += wd
    xn = x_ref[...] + mod_ref[0, 2:3, :] * y
    xn_ref[...] = xn
    h2 = _rms(xn, nw_ref[...]) * (1.0 + mod_ref[0, 4:5, :]) + mod_ref[0, 3:4, :]
    for s in range(ROW_TILE):
        h2_ref[pl.ds(s, TM, stride=ROW_TILE), :] = h2[:, s * LANE:(s + 1) * LANE]
    h_hi = h2.astype(BF16)
    h_lo = (h2 - h_hi.astype(F32)).astype(BF16)
    wr = wr_ref[...]
    hw = jnp.dot(h_hi, wr, preferred_element_type=F32)
    logits = hw[:, :LANE] + hw[:, LANE:] + jnp.dot(h_lo, wr[:, :LANE], preferred_element_type=F32)
    rt_ref[...] = _route(logits)


def _outproj(parts, w_bf16, x, mod, norm_w2, w_router):
    widths = tuple(p[0].shape[1] for p in parts)
    in_specs = []
    for wd in widths:
        in_specs += [pl.BlockSpec((TM, wd), lambda t: (jnp.minimum(t, PROMPT_TILES - 1), 0)),
                     pl.BlockSpec((TM, wd), lambda t: (jnp.maximum(t - PROMPT_TILES, 0), 0))]
    parts = [a for p in parts for a in p]
    in_specs += [pl.BlockSpec((D_MODEL, D_MODEL), lambda t: (0, 0)),
                 pl.BlockSpec((TM, D_MODEL), lambda t: (t, 0)),
                 pl.BlockSpec((1, MOD_ROWS, D_MODEL), lambda t: (_mod_index(t), 0, 0)),
                 pl.BlockSpec((1, D_MODEL), lambda t: (0, 0)),
                 pl.BlockSpec((D_MODEL, 2 * LANE), lambda t: (0, 0))]
    return pl.pallas_call(
        functools.partial(_outproj_kernel, widths=widths),
        out_shape=(jax.ShapeDtypeStruct((N_TOK, D_MODEL), F32),
                   jax.ShapeDtypeStruct((N_TOK * ROW_TILE, LANE), F32),
                   jax.ShapeDtypeStruct((N_TOK, LANE), F32)),
        grid=(N_TILES,),
        in_specs=in_specs,
        out_specs=(pl.BlockSpec((TM, D_MODEL), lambda t: (t, 0)),
                   pl.BlockSpec((TM * ROW_TILE, LANE), lambda t: (t, 0)),
                   pl.BlockSpec((TM, LANE), lambda t: (t, 0))),
        compiler_params=_params(("parallel",)),
        name="outproj",
    )(*parts, w_bf16, x, mod, norm_w2.reshape(1, D_MODEL), w_router)


def _expert_kernel(blk_e_ref, nact_ref, rowtok_ref, h2_hbm, wg_ref, wu_ref, wd_ref, y_ref,
                   x_even, x_odd, sems, wg_s, wu_s, wd_s):
    i = pl.program_id(0)
    n_act = nact_ref[0]
    e = blk_e_ref[i]
    prev = blk_e_ref[jnp.maximum(i - 1, 0)]
    bufs = (x_even, x_odd)

    def row_copy(blk, r, parity):
        tok = rowtok_ref[blk * MOE_ROWS + r]
        return pltpu.make_async_copy(
            h2_hbm.at[pl.ds(pl.multiple_of(tok * ROW_TILE, ROW_TILE), ROW_TILE), :],
            bufs[parity].at[pl.ds(pl.multiple_of(r * ROW_TILE, ROW_TILE), ROW_TILE), :],
            sems.at[parity])

    def wait_block(parity):
        pltpu.make_async_copy(h2_hbm.at[pl.ds(0, MOE_ROWS * ROW_TILE), :], bufs[parity], sems.at[parity]).wait()

    @pl.when(i == 0)
    def _():
        def start_row(r, carry):
            row_copy(0, r, 0).start()
            return carry
        lax.fori_loop(0, MOE_ROWS, start_row, 0)

    @pl.when((i == 0) | (e != prev))
    def _():
        wg_s[...] = wg_ref[...].astype(BF16)
        wu_s[...] = wu_ref[...].astype(BF16)
        wd_s[...] = wd_ref[...].astype(BF16)

    for parity in (0, 1):
        @pl.when((i == n_act) & (i % 2 == parity))
        def _():
            wait_block(parity)

        @pl.when((i < n_act) & (i % 2 == parity))
        def _():
            wait_block(parity)
            for r in range(MOE_ROWS):
                row_copy(i + 1, r, 1 - parity).start()
            x = jnp.concatenate([bufs[parity][pl.ds(s, MOE_ROWS, stride=ROW_TILE), :] for s in range(ROW_TILE)],
                                axis=1).astype(BF16)
            g = jnp.dot(x, wg_s[...], preferred_element_type=F32)
            u = jnp.dot(x, wu_s[...], preferred_element_type=F32)
            y_ref[...] = jnp.dot((_silu(g) * u).astype(BF16), wd_s[...], preferred_element_type=F32)

    @pl.when(i >= nact_ref[0])
    def _():
        y_ref[...] = jnp.zeros_like(y_ref)


def _experts(h2_tiles, row_tok, blk_e, n_active, w_gate, w_up, w_down, layer):
    n_rows = row_tok.shape[0]
    n_blocks = n_rows // MOE_ROWS
    grid_spec = pltpu.PrefetchScalarGridSpec(
        num_scalar_prefetch=3,
        grid=(n_blocks,),
        in_specs=[pl.BlockSpec(memory_space=pl.ANY),
                  pl.BlockSpec((None, None, D_MODEL, D_EXPERT), lambda i, be, na, rt: (layer, be[i], 0, 0)),
                  pl.BlockSpec((None, None, D_MODEL, D_EXPERT), lambda i, be, na, rt: (layer, be[i], 0, 0)),
                  pl.BlockSpec((None, None, D_EXPERT, D_MODEL), lambda i, be, na, rt: (layer, be[i], 0, 0))],
        out_specs=pl.BlockSpec((MOE_ROWS, D_MODEL), lambda i, be, na, rt: (i, 0)),
        scratch_shapes=[pltpu.VMEM((MOE_ROWS * ROW_TILE, LANE), F32),
                        pltpu.VMEM((MOE_ROWS * ROW_TILE, LANE), F32),
                        pltpu.SemaphoreType.DMA((2,)),
                        pltpu.VMEM((D_MODEL, D_EXPERT), BF16),
                        pltpu.VMEM((D_MODEL, D_EXPERT), BF16),
                        pltpu.VMEM((D_EXPERT, D_MODEL), BF16)])
    return pl.pallas_call(
        _expert_kernel,
        out_shape=jax.ShapeDtypeStruct((n_rows, D_MODEL), F32),
        grid_spec=grid_spec,
        compiler_params=_params(("arbitrary",)),
        name="experts",
    )(blk_e, n_active, row_tok, h2_tiles, w_gate, w_up, w_down)


def _combine_kernel(x_ref, y0_ref, y1_ref, rt_ref, mod_ref, o_ref):
    rt = rt_ref[...]
    y = rt[:, 2:3] * y0_ref[...] + rt[:, 3:4] * y1_ref[...]
    o_ref[...] = x_ref[...] + mod_ref[0, 5:6, :] * y


def _combine(x, y0, y1, route, mod):
    row = pl.BlockSpec((TM, D_MODEL), lambda t: (t, 0))
    return pl.pallas_call(
        _combine_kernel,
        out_shape=jax.ShapeDtypeStruct((N_TOK, D_MODEL), F32),
        grid=(N_TILES,),
        in_specs=[row, row, row,
                  pl.BlockSpec((TM, LANE), lambda t: (t, 0)),
                  pl.BlockSpec((1, MOD_ROWS, D_MODEL), lambda t: (_mod_index(t), 0, 0))],
        out_specs=row,
        compiler_params=_params(("parallel",)),
        name="combine",
    )(x, y0, y1, route, mod)


def _take_rows(a, idx):
    return a.at[idx].get(mode="promise_in_bounds")


def _moe(x, h2, route, mod, w_gate, w_up, w_down, layer):
    n_asg = 2 * N_TOK
    flat_e = route[:, :2].astype(jnp.int32).reshape(n_asg)
    flat_tok = jnp.arange(n_asg, dtype=jnp.int32) // 2
    onehot = (flat_e[:, None] == jnp.arange(N_EXPERTS, dtype=jnp.int32)[None, :]).astype(jnp.int32)
    csum = jnp.cumsum(onehot, axis=0)
    rank = jnp.take_along_axis(csum, flat_e[:, None], axis=1)[:, 0] - 1
    counts = csum[-1]
    padded = (counts + MOE_ROWS - 1) // MOE_ROWS * MOE_ROWS
    pad_end = jnp.cumsum(padded)
    pad_start = pad_end - padded
    dest = pad_start[flat_e] + rank
    n_rows = n_asg + (N_EXPERTS + 1) * MOE_ROWS
    n_blocks = n_rows // MOE_ROWS
    row_tok = (jnp.arange(n_rows, dtype=jnp.int32) % N_TOK).at[dest].set(
        flat_tok, unique_indices=True, mode="promise_in_bounds")
    blk_start = jnp.arange(n_blocks, dtype=jnp.int32) * MOE_ROWS
    blk_e = jnp.sum((blk_start[:, None] >= pad_end[None, :]).astype(jnp.int32), axis=1)
    blk_e = jnp.minimum(blk_e, N_EXPERTS - 1)
    n_active = (pad_end[-1:] // MOE_ROWS).astype(jnp.int32)
    yb = _experts(h2, row_tok, blk_e, n_active, w_gate, w_up, w_down, layer)
    dest2 = dest.reshape(N_TOK, 2)
    return _combine(x, _take_rows(yb, dest2[:, 0]), _take_rows(yb, dest2[:, 1]), route, mod)


def _final_kernel(x_ref, w_ref, o_ref):
    o_ref[...] = _rms(x_ref[...], w_ref[...])


def _final_norm(x, w, tile0, n_rows):
    return pl.pallas_call(
        _final_kernel,
        out_shape=jax.ShapeDtypeStruct((n_rows, D_MODEL), F32),
        grid=(n_rows // TM,),
        in_specs=[pl.BlockSpec((TM, D_MODEL), lambda t: (t + tile0, 0)),
                  pl.BlockSpec((1, D_MODEL), lambda t: (0, 0))],
        out_specs=pl.BlockSpec((TM, D_MODEL), lambda t: (t, 0)),
        compiler_params=_params(("parallel",)),
        name="final_norm",
    )(x, w.reshape(1, D_MODEL))


def _hgrn_kernel(*refs, seq_len, layer, context):
    if context:
        (q_ref, ff_ref, fb_ref, v_ref, g_ref, lbl_ref, nw_ref,
         o_ref, sfin_ref, of_s, ob_s, sf_s, sb_s) = refs
    else:
        (q_ref, ff_ref, fb_ref, v_ref, g_ref, lbl_ref, nw_ref, s0_ref,
         o_ref, of_s, ob_s, sf_s, sb_s) = refs
    c = CHUNK_A
    n = seq_len // c

    def lower_bound(d):
        z = lbl_ref[d]
        e = jnp.exp(z - jnp.max(z, axis=0, keepdims=True))
        return jnp.sum(e[:layer + 1], axis=0, keepdims=True) / jnp.sum(e, axis=0, keepdims=True)

    lb_f = lower_bound(0)
    lb_b = lower_bound(1)
    row = lax.broadcasted_iota(jnp.int32, (c, c), 0)
    col = lax.broadcasted_iota(jnp.int32, (c, c), 1)
    causal = row >= col
    tri_f = causal.astype(F32)
    tri_b = (row <= col).astype(F32)

    if context:
        sf_s[...] = jnp.zeros_like(sf_s)
        sb_s[...] = jnp.zeros_like(sb_s)
    else:
        sf_s[...] = s0_ref[0, 0, 0].T
        sb_s[...] = s0_ref[0, 1, 0].T

    group = 4
    dirs = ((ff_ref, lb_f, tri_f, causal, c - 1, sf_s, of_s),
            (fb_ref, lb_b, tri_b, row <= col, 0, sb_s, ob_s))

    def body(i, carry):
        items = []
        for g in range(group):
            j = i * group + g
            items.append((0, pl.ds(pl.multiple_of(j * c, c), c)))
            items.append((1, pl.ds(pl.multiple_of((n - 1 - j) * c, c), c)))
        fs = [dirs[d][1] + (1.0 - dirs[d][1]) * _sigmoid(dirs[d][0][rows, :]) for d, rows in items]
        bs = [_tri_matmul(dirs[d][2], jnp.log(f)) for (d, _), f in zip(items, fs)]
        b_lasts = [b[dirs[d][4]:dirs[d][4] + 1, :] for (d, _), b in zip(items, bs)]
        q_ins = [_silu(q_ref[rows, :]) * jnp.exp(b) for (_, rows), b in zip(items, bs)]
        a_s = [jnp.where(dirs[d][3], _dot_nt(q_in, (1.0 - f) * jnp.exp(-b)), 0.0)
               for (d, _), q_in, f, b in zip(items, q_ins, fs, bs)]
        vs = [v_ref[rows, :] for _, rows in items]
        o_intra = [_dot(a, v) for a, v in zip(a_s, vs)]
        u_ts = [_dot_tn(v, (1.0 - f) * jnp.exp(b_last - b)) for v, f, b, b_last in zip(vs, fs, bs, b_lasts)]
        for d in (0, 1):
            st_ref, out_s = dirs[d][5], dirs[d][6]
            st = st_ref[...]
            for k, (dk, rows) in enumerate(items):
                if dk == d:
                    out_s[rows, :] = o_intra[k] + _dot_nt(q_ins[k], st)
                    st = st * jnp.exp(b_lasts[k]) + u_ts[k]
            st_ref[...] = st
        return carry

    lax.fori_loop(0, n // group, body, 0)

    if context:
        sfin_ref[0, 0, 0] = sf_s[...].T
        sfin_ref[0, 1, 0] = sb_s[...].T

    nw = nw_ref[...]

    def epilogue(j, carry):
        rows = pl.ds(pl.multiple_of(j * TM, TM), TM)
        o = of_s[rows, :] + ob_s[rows, :]
        o_ref[rows, :] = _rms(o, nw) * _silu(g_ref[rows, :])
        return carry

    lax.fori_loop(0, seq_len // TM, epilogue, 0)


def _hgrn(proj, lb_logits, norm_w, layer, context, s0=None):
    seq_len = SEQ if context else DEC_SEQ
    bsz = BATCH if context else DEC_BATCH
    row0 = 0 if context else N_PROMPT // seq_len

    def col(k):
        return pl.BlockSpec((seq_len, LANE), lambda b, h: (row0 + b, k * H_A + h))

    in_specs = [col(0), col(1), col(2), col(3), col(4),
                pl.BlockSpec((2, DEPTH + 1, LANE), lambda b, h: (0, 0, h)),
                pl.BlockSpec((1, LANE), lambda b, h: (0, 0))]
    args = [proj, proj, proj, proj, proj, lb_logits, norm_w.reshape(1, LANE)]
    o_shape = jax.ShapeDtypeStruct((bsz * seq_len, W_A), F32)
    o_spec = pl.BlockSpec((seq_len, LANE), lambda b, h: (b, h))
    scratch = [pltpu.VMEM((seq_len, LANE), F32), pltpu.VMEM((seq_len, LANE), F32),
               pltpu.VMEM((LANE, LANE), F32), pltpu.VMEM((LANE, LANE), F32)]
    kern = functools.partial(_hgrn_kernel, seq_len=seq_len, layer=layer, context=context)
    if context:
        return pl.pallas_call(
            kern,
            out_shape=(o_shape, jax.ShapeDtypeStruct((bsz, 2, H_A, DK_A, DK_A), F32)),
            grid=(bsz, H_A),
            in_specs=in_specs,
            out_specs=(o_spec, pl.BlockSpec((1, 2, 1, DK_A, DK_A), lambda b, h: (b, 0, h, 0, 0))),
            scratch_shapes=scratch,
            compiler_params=_params(("parallel", "parallel")),
            name="hgrn_ctx",
        )(*args)
    in_specs += [pl.BlockSpec((1, 2, 1, DK_A, DK_A), lambda b, h: (b, 0, h, 0, 0))]
    args += [s0]
    return pl.pallas_call(
        kern,
        out_shape=o_shape,
        grid=(bsz, H_A),
        in_specs=in_specs,
        out_specs=o_spec,
        scratch_shapes=scratch,
        compiler_params=_params(("parallel", "parallel")),
        name="hgrn_dec",
    )(*args)


ATT_TQ = 256
COL_QD = 5 * W_A // LANE
COL_KD = COL_QD + QK_B // LANE
COL_VD = COL_KD + QK_B // LANE


def _attn_kernel(*refs, seq_len, layer, context):
    if context:
        q_ref, k_ref, v_ref, lam_ref, nw_ref, o_ref, k_s, v_s = refs
    else:
        q_ref, k_ref, v_ref, ck_ref, cv_ref, lam_ref, nw_ref, o_ref, k_s, v_s = refs

    @pl.when(pl.program_id(2) == 0)
    def _():
        k_s[0:seq_len, :] = k_ref[...].astype(BF16)
        v_s[0:seq_len, :] = v_ref[...].astype(BF16)
        if not context:
            k_s[seq_len:, :] = ck_ref[0].astype(BF16)
            v_s[seq_len:, :] = cv_ref[0].astype(BF16)

    lam_init = 0.8 - 0.6 * math.exp(-0.3 * layer)
    lp = lam_ref[...]
    lam = (jnp.exp(jnp.sum(lp[0:1] * lp[1:2], axis=1, keepdims=True))
           - jnp.exp(jnp.sum(lp[2:3] * lp[3:4], axis=1, keepdims=True)) + lam_init)

    q = q_ref[...] * (DH_B ** -0.5)
    lane = lax.broadcasted_iota(jnp.int32, q.shape, 1)
    k = k_s[...]

    def softmax_map(first):
        s = _dot_nt(jnp.where((lane < DH_B) == first, q, 0.0), k)
        p = jnp.exp(s - jnp.max(s, axis=1, keepdims=True))
        return p, jnp.sum(p, axis=1, keepdims=True)

    p0, l0 = softmax_map(True)
    p1, l1 = softmax_map(False)
    a = p0 - (lam * l0 / l1) * p1
    o = jnp.dot(a.astype(BF16), v_s[...], preferred_element_type=F32) / l0
    o_ref[...] = _rms(o, nw_ref[...]) * (1.0 - lam_init)


def _attn(proj, lam_p, norm_w, layer, context, cache_k=None, cache_v=None):
    seq_len = SEQ if context else DEC_SEQ
    bsz = BATCH if context else DEC_BATCH
    row0 = 0 if context else N_PROMPT // seq_len
    nq = seq_len // ATT_TQ
    tile0 = row0 * nq
    t_k = seq_len if context else seq_len + PAST_LEN
    in_specs = [pl.BlockSpec((ATT_TQ, LANE), lambda b, h, i: (tile0 + b * nq + i, COL_QD + h)),
                pl.BlockSpec((seq_len, LANE), lambda b, h, i: (row0 + b, COL_KD + h)),
                pl.BlockSpec((seq_len, LANE), lambda b, h, i: (row0 + b, COL_VD + h))]
    args = [proj, proj, proj]
    if not context:
        in_specs += [pl.BlockSpec((1, PAST_LEN, LANE), lambda b, h, i: (b, 0, h)),
                     pl.BlockSpec((1, PAST_LEN, LANE), lambda b, h, i: (b, 0, h))]
        args += [cache_k, cache_v]
    in_specs += [pl.BlockSpec((4, DH_B), lambda b, h, i: (0, 0)),
                 pl.BlockSpec((1, LANE), lambda b, h, i: (0, 0))]
    args += [lam_p, norm_w.reshape(1, LANE)]
    return pl.pallas_call(
        functools.partial(_attn_kernel, seq_len=seq_len, layer=layer, context=context),
        out_shape=jax.ShapeDtypeStruct((bsz * seq_len, W_B), F32),
        grid=(bsz, H_B, nq),
        in_specs=in_specs,
        out_specs=pl.BlockSpec((ATT_TQ, LANE), lambda b, h, i: (b * nq + i, h)),
        scratch_shapes=[pltpu.VMEM((t_k, LANE), BF16), pltpu.VMEM((t_k, LANE), BF16)],
        compiler_params=_params(("parallel", "parallel", "arbitrary")),
        name="attn_ctx" if context else "attn_dec",
    )(*args)


COL_GATES = 4 * W_C // LANE


def _gdn_gates_kernel(g_ref, alog_ref, dtb_ref, col_ref, row_ref):
    c = GDN_CHUNK
    raw = g_ref[...]
    z = raw + dtb_ref[...]
    softplus = jnp.maximum(z, 0.0) + jnp.log(1.0 + jnp.exp(-jnp.abs(z)))
    g = -jnp.exp(alog_ref[...]) * softplus
    row = lax.broadcasted_iota(jnp.int32, (c, c), 0)
    col = lax.broadcasted_iota(jnp.int32, (c, c), 1)
    lane = lax.broadcasted_iota(jnp.int32, (c, LANE), 1)
    prefix = _tri_matmul((row >= col).astype(F32), g)
    suffix = _tri_matmul((row <= col).astype(F32), g)
    out = jnp.where(lane < H_C, prefix, jnp.where(lane < 2 * H_C, suffix, _sigmoid(raw)))
    col_ref[...] = out
    row_ref[0] = out.T


def _gdn_gates(proj, alog_lane, dtb_lane):
    c = GDN_CHUNK
    n = N_TOK // c
    return pl.pallas_call(
        _gdn_gates_kernel,
        out_shape=(jax.ShapeDtypeStruct((N_TOK, LANE), F32), jax.ShapeDtypeStruct((n, LANE, c), F32)),
        grid=(n,),
        in_specs=[pl.BlockSpec((c, LANE), lambda i: (i, COL_GATES)),
                  pl.BlockSpec((1, LANE), lambda i: (0, 0)),
                  pl.BlockSpec((1, LANE), lambda i: (0, 0))],
        out_specs=(pl.BlockSpec((c, LANE), lambda i: (i, 0)),
                   pl.BlockSpec((1, LANE, c), lambda i: (i, 0, 0))),
        compiler_params=_params(("parallel",)),
        name="gdn_gates",
    )(proj, alog_lane, dtb_lane)


def _unit_tri_inverse_pairs(ms, row, col):
    c = ms[0].shape[0]
    zero = jnp.zeros((c, c), BF16)

    def mm(xs, ys):
        out = []
        for x, y in zip(xs, ys):
            yb = y.astype(BF16)
            blockdiag = jnp.concatenate([jnp.concatenate([yb[:, :c], zero], axis=1),
                                         jnp.concatenate([zero, yb[:, c:]], axis=1)], axis=0)
            out.append(jnp.dot(x.astype(BF16), blockdiag, preferred_element_type=F32))
        return out

    def add(xs, ys):
        return [x + y for x, y in zip(xs, ys)]

    eye = (row == col).astype(F32)
    a = [jnp.where((row // 16) == (col // 16), m, 0.0) for m in ms]
    a2 = mm(a, a)
    a4 = mm(a2, a2)
    a8 = mm(a4, a4)
    t = [eye - x for x in a]
    t = add(t, mm(t, a2))
    t = add(t, mm(t, a4))
    t = add(t, mm(t, a8))
    blk = 32
    while blk <= c:
        off = ((row // blk) == (col // blk)) & ((row // (blk // 2)) != (col // (blk // 2)))
        corr = mm(mm(t, [jnp.where(off, m, 0.0) for m in ms]), t)
        t = [x - y for x, y in zip(t, corr)]
        blk *= 2
    return t


def _gdn_kernel(*refs, seq_len, context):
    if context:
        (q_ref, k_ref, v_ref, go_ref, cwq_ref, cwk_ref, cwv_ref, gcol_ref, grow_ref, nw_ref,
         o_ref, sfin_ref, su_s, pr_s, of_s, ob_s, sf_s, sb_s) = refs
    else:
        (q_ref, k_ref, v_ref, go_ref, cwq_ref, cwk_ref, cwv_ref, gcol_ref, grow_ref, nw_ref, s0_ref,
         o_ref, su_s, pr_s, of_s, ob_s, sf_s, sb_s) = refs
    c = GDN_CHUNK
    n = seq_len // c
    head = pl.program_id(1)
    row = lax.broadcasted_iota(jnp.int32, (c, c), 0)
    col = lax.broadcasted_iota(jnp.int32, (c, c), 1)
    row2 = lax.broadcasted_iota(jnp.int32, (c, 2 * c), 0)
    col2 = lax.broadcasted_iota(jnp.int32, (c, 2 * c), 1) & (c - 1)
    lane = lax.broadcasted_iota(jnp.int32, (c, LANE), 1)
    rowi = lax.broadcasted_iota(jnp.int32, (c, LANE), 0)

    def gc_row_last(ci, d):
        gc_row = grow_ref[ci, pl.ds(d * H_C + head, 1), :]
        return gc_row, (gc_row[:, c - 1:c] if d == 0 else gc_row[:, 0:1])

    def chunk_inputs(ci):
        r0 = pl.multiple_of(ci * c, c)
        rows = pl.ds(r0, c)

        def conv(x_ref, w_ref):
            cur = x_ref[rows, :]
            before = x_ref[pl.ds(pl.multiple_of(jnp.maximum(r0 - 8, 0), 8), 8), :]
            after = x_ref[pl.ds(pl.multiple_of(jnp.minimum(r0 + c, seq_len - 8), 8), 8), :]
            prev_row = jnp.where(ci > 0, before[7:8, :], 0.0)
            next_row = jnp.where(ci < n - 1, after[0:1, :], 0.0)
            xm1 = jnp.where(rowi == 0, prev_row, pltpu.roll(cur, 1, 0))
            xp1 = jnp.where(rowi == c - 1, next_row, pltpu.roll(cur, c - 1, 0))
            w = w_ref[...]
            return _silu(xm1 * w[0:1, :] + cur * w[1:2, :] + xp1 * w[2:3, :])

        q = conv(q_ref, cwq_ref)
        k = conv(k_ref, cwk_ref)
        vn = conv(v_ref, cwv_ref)
        qn = q * lax.rsqrt(jnp.sum(q * q, axis=1, keepdims=True) + EPS) * (DK_C ** -0.5)
        kn = k * lax.rsqrt(jnp.sum(k * k, axis=1, keepdims=True) + EPS)
        kq = _dot_nt(jnp.concatenate([kn, qn], axis=0), kn)
        gates = gcol_ref[rows, :]
        per_dir = []
        ms = []
        for d in (0, 1):
            gc = jnp.sum(jnp.where(lane == d * H_C + head, gates, 0.0), axis=1, keepdims=True)
            beta = jnp.sum(jnp.where(lane == (2 + d) * H_C + head, gates, 0.0), axis=1, keepdims=True)
            gc_row, gc_last = gc_row_last(ci, d)
            incl = (row >= col) if d == 0 else (row <= col)
            strict = (row > col) if d == 0 else (row < col)
            decay = jnp.where(incl, jnp.exp(gc - gc_row), 0.0)
            m = jnp.where(strict, kq[:c] * beta * decay, 0.0)
            per_dir.append((gc, beta, kq[c:] * decay, (kn * jnp.exp(gc_last - gc)).T))
            ms.append(m)
        return jnp.concatenate(ms, axis=1), (rows, qn, kn, vn, per_dir)

    def chunk_outputs(ci, t, rest):
        rows, qn, kn, vn, per_dir = rest
        base = pl.multiple_of(ci * 2 * c, 2 * c)
        for d in (0, 1):
            gc, beta, qk, kd_t = per_dir[d]
            e = jnp.exp(gc)
            uw = _dot(t[:, d * c:(d + 1) * c], jnp.concatenate([vn * beta, kn * (beta * e)], axis=1))
            cross = _dot(jnp.concatenate([kd_t, qk], axis=0), uw)
            su_s[d, rows, :] = cross[:c, :c]
            (of_s, ob_s)[d][rows, :] = cross[c:, :c]
            pr_s[d, pl.ds(base, c), :] = cross[:c, c:].astype(BF16)
            pr_s[d, pl.ds(base + c, c), :] = (qn * e - cross[c:, c:]).astype(BF16)

    group = min(4, n)

    def prepare(j, carry):
        cis = [j * group + g for g in range(group)]
        staged = [chunk_inputs(ci) for ci in cis]
        ts = _unit_tri_inverse_pairs([s[0] for s in staged], row2, col2)
        for ci, t, s in zip(cis, ts, staged):
            chunk_outputs(ci, t, s[1])
        return carry

    lax.fori_loop(0, n // group, prepare, 0)

    if context:
        sf_s[...] = jnp.zeros_like(sf_s)
        sb_s[...] = jnp.zeros_like(sb_s)
    else:
        sf_s[...] = s0_ref[0, 0, 0]
        sb_s[...] = s0_ref[0, 1, 0]

    def advance(ci, d, s_ref, out_s):
        rows = pl.ds(pl.multiple_of(ci * c, c), c)
        _, gc_last = gc_row_last(ci, d)
        s = s_ref[...]
        ps = jnp.dot(pr_s[d, pl.ds(pl.multiple_of(ci * 2 * c, 2 * c), 2 * c), :], s.astype(BF16),
                     preferred_element_type=F32)
        out_s[rows, :] = out_s[rows, :] + ps[c:]
        s_ref[...] = s * jnp.exp(gc_last) - ps[:c] + su_s[d, rows, :]

    def body(i, carry):
        advance(i, 0, sf_s, of_s)
        advance(n - 1 - i, 1, sb_s, ob_s)
        return carry

    lax.fori_loop(0, n, body, 0)

    if context:
        sfin_ref[0, 0, 0] = sf_s[...]
        sfin_ref[0, 1, 0] = sb_s[...]

    nw = nw_ref[...]

    def epilogue(j, carry):
        rows = pl.ds(pl.multiple_of(j * TM, TM), TM)
        o = of_s[rows, :] + ob_s[rows, :]
        o_ref[rows, :] = _rms(o, nw) * _silu(go_ref[rows, :])
        return carry

    lax.fori_loop(0, seq_len // TM, epilogue, 0)


def _gdn(proj, conv_w, gcol, grow, norm_w, context, s0=None):
    seq_len = SEQ if context else DEC_SEQ
    bsz = BATCH if context else DEC_BATCH
    row0 = 0 if context else N_PROMPT // seq_len
    nc = seq_len // GDN_CHUNK

    def col(k):
        return pl.BlockSpec((seq_len, LANE), lambda b, h: (row0 + b, k * H_C + h))

    def cw(k):
        return pl.BlockSpec((3, LANE), lambda b, h: (0, k * H_C + h))

    in_specs = [col(0), col(1), col(2), col(3), cw(0), cw(1), cw(2),
                pl.BlockSpec((seq_len, LANE), lambda b, h: (row0 + b, 0)),
                pl.BlockSpec((nc, LANE, GDN_CHUNK), lambda b, h: (row0 + b, 0, 0)),
                pl.BlockSpec((1, LANE), lambda b, h: (0, 0))]
    args = [proj, proj, proj, proj, conv_w, conv_w, conv_w, gcol, grow, norm_w.reshape(1, LANE)]
    o_shape = jax.ShapeDtypeStruct((bsz * seq_len, W_C), F32)
    o_spec = pl.BlockSpec((seq_len, LANE), lambda b, h: (b, h))
    seq_buf = pltpu.VMEM((seq_len, LANE), F32)
    state_buf = pltpu.VMEM((DK_C, DK_C), F32)
    scratch = [pltpu.VMEM((2, seq_len, LANE), F32), pltpu.VMEM((2, 2 * seq_len, LANE), BF16),
               seq_buf, seq_buf, state_buf, state_buf]
    kern = functools.partial(_gdn_kernel, seq_len=seq_len, context=context)
    state_spec = pl.BlockSpec((1, 2, 1, DK_C, DK_C), lambda b, h: (b, 0, h, 0, 0))
    if context:
        return pl.pallas_call(
            kern,
            out_shape=(o_shape, jax.ShapeDtypeStruct((bsz, 2, H_C, DK_C, DK_C), F32)),
            grid=(bsz, H_C),
            in_specs=in_specs,
            out_specs=(o_spec, state_spec),
            scratch_shapes=scratch,
            compiler_params=_params(("parallel", "parallel")),
            name="gdn_ctx",
        )(*args)
    in_specs += [state_spec]
    args += [s0]
    return pl.pallas_call(
        kern,
        out_shape=o_shape,
        grid=(bsz, H_C),
        in_specs=in_specs,
        out_specs=o_spec,
        scratch_shapes=scratch,
        compiler_params=_params(("parallel", "parallel")),
        name="gdn_dec",
    )(*args)


def kernel(x_prompt, x_sample, c, cache_diff_k, cache_diff_v, state_hgrn, state_gdn, c_ctx,
           ada_w, ada_b, norm_w, final_norm_w, w_in_ab, hgrn_lb_logits, hgrn_norm_w,
           diff_lambda, diff_norm_w, w_in_c, gdn_conv_w, gdn_a_log, gdn_dt_bias, gdn_norm_w,
           w_out, moe_router_group, moe_router_expert, moe_w_gate, moe_w_up, moe_w_down):
    x = jnp.concatenate([x_prompt.reshape(N_PROMPT, D_MODEL), x_sample.reshape(N_SAMPLE, D_MODEL)], axis=0)
    cvec = jnp.concatenate([c_ctx[None, :], c, jnp.zeros((MOD_ROWS - N_MOD, D_MODEL), F32)], axis=0)
    mod_all = _adaln(cvec, ada_w, ada_b)[:, :N_MOD].reshape(DEPTH, N_MOD, 6, D_MODEL)
    mod_all = jnp.pad(mod_all, ((0, 0), (0, 0), (0, MOD_ROWS - 6), (0, 0)))
    cos, sin = _rope_tables()
    new_k, new_v, new_hgrn, new_gdn = [], [], [], []
    for l in range(DEPTH):
        i = l // 2
        mod = mod_all[l]
        if l % 2 == 0:
            proj = _inproj(x, mod, norm_w[l, 0], w_in_ab[i].astype(BF16),
                           rope=(cos, sin, (COL_QD * LANE, COL_VD * LANE)))
            oh_ctx, s_h = _hgrn(proj, hgrn_lb_logits, hgrn_norm_w[i], l, True)
            oh_dec = _hgrn(proj, hgrn_lb_logits, hgrn_norm_w[i], l, False, s0=state_hgrn[:, i])
            od_ctx = _attn(proj, diff_lambda[i], diff_norm_w[i], l, True)
            od_dec = _attn(proj, diff_lambda[i], diff_norm_w[i], l, False,
                           cache_k=cache_diff_k[:, i].reshape(DEC_BATCH, PAST_LEN, QK_B),
                           cache_v=cache_diff_v[:, i].reshape(DEC_BATCH, PAST_LEN, W_B))
            parts = ((oh_ctx, oh_dec), (od_ctx, od_dec))
            new_k.append(proj[:N_PROMPT, COL_KD * LANE:COL_VD * LANE].reshape(BATCH, SEQ, H_B, 2, DH_B))
            new_v.append(proj[:N_PROMPT, COL_VD * LANE:].reshape(BATCH, SEQ, H_B, DV_B))
            new_hgrn.append(s_h)
        else:
            w_c = jnp.pad(w_in_c[i], ((0, 0), (0, (COL_GATES + 1) * LANE - IN_C))).astype(BF16)
            proj = _inproj(x, mod, norm_w[l, 0], w_c)
            pad = jnp.zeros((LANE - 2 * H_C,), F32)
            alog_lane = jnp.concatenate([gdn_a_log[i, 0], gdn_a_log[i, 1], pad]).reshape(1, LANE)
            dtb_lane = jnp.concatenate([gdn_dt_bias[i, 0], gdn_dt_bias[i, 1], pad]).reshape(1, LANE)
            gcol, grow = _gdn_gates(proj, alog_lane, dtb_lane)
            oc_ctx, s_c = _gdn(proj, gdn_conv_w[i], gcol, grow, gdn_norm_w[i], True)
            oc_dec = _gdn(proj, gdn_conv_w[i], gcol, grow, gdn_norm_w[i], False, s0=state_gdn[:, i])
            parts = ((oc_ctx, oc_dec),)
            new_gdn.append(s_c)
        w_router = jnp.concatenate(
            [moe_router_group[l], moe_router_expert[l],
             jnp.zeros((D_MODEL, LANE - N_GROUPS - N_EXPERTS), F32)], axis=1)
        w_router_hi = w_router.astype(BF16)
        w_router = jnp.concatenate([w_router_hi, (w_router - w_router_hi.astype(F32)).astype(BF16)], axis=1)
        x, h2, route = _outproj(parts, w_out[l].astype(BF16), x, mod, norm_w[l, 1], w_router)
        x = _moe(x, h2, route, mod, moe_w_gate, moe_w_up, moe_w_down, l)
    y_prompt = _final_norm(x, final_norm_w, 0, N_PROMPT).reshape(BATCH, SEQ, D_MODEL)
    y_sample = _final_norm(x, final_norm_w, PROMPT_TILES, N_SAMPLE).reshape(DEC_BATCH, DEC_SEQ, D_MODEL)
    return (y_prompt, y_sample, jnp.stack(new_k, axis=1), jnp.stack(new_v, axis=1),
            jnp.stack(new_hgrn, axis=1), jnp.stack(new_gdn, axis=1))
```

```python
import functools
import math

import jax
import jax.numpy as jnp
from jax import lax
from jax.experimental import pallas as pl
from jax.experimental.pallas import tpu as pltpu

F32 = jnp.float32
BF16 = jnp.bfloat16
HIGHEST = lax.Precision.HIGHEST

D_MODEL = 1024
BATCH = 16
SEQ = 256
DEPTH = 2
DEC_BATCH = 4
DEC_SEQ = 4096
PAST_LEN = 256
GRID_W = 64
H_A = 4
DK_A = 128
W_A = 512
CHUNK_A = 32
H_B = 4
DH_B = 64
DV_B = 128
QK_B = 512
W_B = 512
ROPE_BASE = 10000.0
H_C = 8
DK_C = 128
W_C = 1024
N_GROUPS = 4
E_PER_GROUP = 8
N_EXPERTS = 32
D_EXPERT = 512
EPS = 1e-6
IN_AB = 5 * W_A + 2 * QK_B + W_B
IN_C = 4 * W_C + 4 * H_C

LANE = 128
N_PROMPT = BATCH * SEQ
N_SAMPLE = DEC_BATCH * DEC_SEQ
N_TOK = N_PROMPT + N_SAMPLE
TM = 256
N_TILES = N_TOK // TM
PROMPT_TILES = N_PROMPT // TM
TILES_PER_SAMPLE = DEC_SEQ // TM
N_MOD = 1 + DEC_BATCH
MOD_ROWS = 8
MOE_ROWS = 256
ROW_TILE = D_MODEL // LANE
GDN_CHUNK = 128
VMEM_LIMIT = 56 * 1024 * 1024


def _mod_index(t):
    return jnp.where(t < PROMPT_TILES, 0, 1 + (t - PROMPT_TILES) // TILES_PER_SAMPLE)


def _sigmoid(x):
    return 1.0 / (1.0 + jnp.exp(-x))


def _silu(x):
    return x * _sigmoid(x)


def _rms(x, w):
    return x * lax.rsqrt(jnp.mean(x * x, axis=-1, keepdims=True) + EPS) * w


def _dot(a, b):
    return jnp.dot(a.astype(BF16), b.astype(BF16), preferred_element_type=F32)


def _dot_nt(a, b):
    return lax.dot_general(a.astype(BF16), b.astype(BF16), (((1,), (1,)), ((), ())),
                           preferred_element_type=F32)


def _dot_tn(a, b):
    return lax.dot_general(a.astype(BF16), b.astype(BF16), (((0,), (0,)), ((), ())),
                           preferred_element_type=F32)


def _dot_f32(a, b):
    return jnp.dot(a, b, precision=HIGHEST, preferred_element_type=F32)


def _tri_matmul(tri, x):
    hi = x.astype(BF16)
    rem = x - hi.astype(F32)
    mid = rem.astype(BF16)
    lo = (rem - mid.astype(F32)).astype(BF16)
    n = x.shape[1]
    r = jnp.dot(tri.astype(BF16), jnp.concatenate([hi, mid, lo], axis=1), preferred_element_type=F32)
    return r[:, :n] + r[:, n:2 * n] + r[:, 2 * n:]


def _params(sem):
    return pltpu.CompilerParams(dimension_semantics=sem, vmem_limit_bytes=VMEM_LIMIT)


def _adaln_kernel(c_ref, w_ref, b_ref, o_ref):
    s = _silu(c_ref[...])
    o_ref[0] = _dot(s, w_ref[0]) + b_ref[0]


def _adaln(cvec, ada_w, ada_b):
    nb = 4
    wb = 6 * D_MODEL // nb
    return pl.pallas_call(
        _adaln_kernel,
        out_shape=jax.ShapeDtypeStruct((DEPTH, MOD_ROWS, 6 * D_MODEL), F32),
        grid=(DEPTH, nb),
        in_specs=[pl.BlockSpec((MOD_ROWS, D_MODEL), lambda l, j: (0, 0)),
                  pl.BlockSpec((1, D_MODEL, wb), lambda l, j: (l, 0, j)),
                  pl.BlockSpec((1, 1, wb), lambda l, j: (l, 0, j))],
        out_specs=pl.BlockSpec((1, MOD_ROWS, wb), lambda l, j: (l, 0, j)),
        compiler_params=_params(("parallel", "parallel")),
        name="adaln",
    )(cvec, ada_w, ada_b.reshape(DEPTH, 1, 6 * D_MODEL))


def _row_specs(x, width):
    if isinstance(x, tuple):
        return ([pl.BlockSpec((TM, width), lambda t: (jnp.minimum(t, PROMPT_TILES - 1), 0)),
                 pl.BlockSpec((TM, width), lambda t: (jnp.maximum(t - PROMPT_TILES, 0), 0))], list(x))
    return [pl.BlockSpec((TM, width), lambda t: (t, 0))], [x]


def _read_rows(refs):
    if len(refs) == 2:
        return jnp.where(pl.program_id(0) < PROMPT_TILES, refs[0][...], refs[1][...])
    return refs[0][...]


def _inproj_kernel(*refs, n_x, rope_cols):
    x = _read_rows(refs[:n_x])
    if rope_cols is None:
        mod_ref, nw_ref, w_ref, o_ref = refs[n_x:]
    else:
        mod_ref, nw_ref, w_ref, cos_ref, sin_ref, o_ref = refs[n_x:]
    h = _rms(x, nw_ref[...]) * (1.0 + mod_ref[0, 1:2, :]) + mod_ref[0, 0:1, :]
    r = jnp.dot(h.astype(BF16), w_ref[...], preferred_element_type=F32)
    if rope_cols is None:
        o_ref[...] = r
        return
    lo, hi = rope_cols
    o_ref[:, :lo] = r[:, :lo]
    o_ref[:, hi:] = r[:, hi:]
    cos = cos_ref[...]
    sin = sin_ref[...]
    lane = lax.broadcasted_iota(jnp.int32, (TM, LANE), 1)
    upper = (lane & 16) != 0
    for c0 in range(lo, hi, LANE):
        v = r[:, c0:c0 + LANE]
        partner = jnp.where(upper, pltpu.roll(v, 16, 1), pltpu.roll(v, LANE - 16, 1))
        o_ref[:, c0:c0 + LANE] = v * cos + partner * sin


def _inproj(x, mod, norm_w, w_bf16, rope=None):
    p = w_bf16.shape[1]
    in_specs, args = _row_specs(x, D_MODEL)
    n_x = len(args)
    in_specs += [pl.BlockSpec((1, MOD_ROWS, D_MODEL), lambda t: (_mod_index(t), 0, 0)),
                 pl.BlockSpec((1, D_MODEL), lambda t: (0, 0)),
                 pl.BlockSpec((D_MODEL, p), lambda t: (0, 0))]
    args += [mod, norm_w.reshape(1, D_MODEL), w_bf16]
    rope_cols = None
    if rope is not None:
        cos, sin, rope_cols = rope

        def rope_index(t):
            return (jnp.where(t < PROMPT_TILES, 0, 1 + (t - PROMPT_TILES) % TILES_PER_SAMPLE), 0)

        in_specs += [pl.BlockSpec((TM, LANE), rope_index), pl.BlockSpec((TM, LANE), rope_index)]
        args += [cos, sin]
    return pl.pallas_call(
        functools.partial(_inproj_kernel, n_x=n_x, rope_cols=rope_cols),
        out_shape=jax.ShapeDtypeStruct((N_TOK, p), F32),
        grid=(N_TILES,),
        in_specs=in_specs,
        out_specs=pl.BlockSpec((TM, p), lambda t: (t, 0)),
        compiler_params=_params(("parallel",)),
        name="inproj",
    )(*args)


def _rope_tables():
    lane = jnp.arange(LANE)
    d = lane % DH_B
    use_col = (d // 32) == 1
    j = d % 16
    upper = ((d % 32) // 16) == 1
    inv_freq = ROPE_BASE ** (-j.astype(F32) / 16.0)
    t = jnp.arange(DEC_SEQ)
    row = (t // GRID_W).astype(F32)
    col = (t % GRID_W).astype(F32)
    pos = jnp.where(use_col[None, :], col[:, None], row[:, None])
    ang = pos * inv_freq[None, :]
    cos = jnp.cos(ang)
    sin = jnp.where(upper[None, :], jnp.sin(ang), -jnp.sin(ang))
    cos = jnp.concatenate([jnp.ones((TM, LANE), F32), cos], axis=0)
    sin = jnp.concatenate([jnp.zeros((TM, LANE), F32), sin], axis=0)
    return cos, sin


def _route(logits):
    lane = lax.broadcasted_iota(jnp.int32, logits.shape, 1)
    lanef = lane.astype(F32)
    neg = jnp.float32(-jnp.inf)
    gl = jnp.where(lane < N_GROUPS, logits, neg)
    gmax = jnp.max(gl, axis=1, keepdims=True)
    gsel = jnp.min(jnp.where(gl == gmax, lanef, float(LANE)), axis=1, keepdims=True)
    p_grp = 1.0 / jnp.sum(jnp.exp(gl - gmax), axis=1, keepdims=True)
    lo = float(N_GROUPS) + gsel * float(E_PER_GROUP)
    el = jnp.where((lanef >= lo) & (lanef < lo + float(E_PER_GROUP)), logits, neg)
    v1 = jnp.max(el, axis=1, keepdims=True)
    i1 = jnp.min(jnp.where(el == v1, lanef, float(LANE)), axis=1, keepdims=True)
    el2 = jnp.where(lanef == i1, neg, el)
    v2 = jnp.max(el2, axis=1, keepdims=True)
    i2 = jnp.min(jnp.where(el2 == v2, lanef, float(LANE)), axis=1, keepdims=True)
    t = jnp.exp(v2 - v1)
    w1 = p_grp / (1.0 + t)
    w2 = p_grp * t / (1.0 + t)
    out = jnp.where(lane == 0, i1 - float(N_GROUPS), 0.0)
    out = jnp.where(lane == 1, i2 - float(N_GROUPS), out)
    out = jnp.where(lane == 2, w1, out)
    out = jnp.where(lane == 3, w2, out)
    return out


def _outproj_kernel(*refs, widths, n_x):
    n_in = len(widths)
    x_refs = refs[2 * n_in:2 * n_in + n_x]
    w_ref, mod_ref, nw_ref, wr_ref, xn_ref, h2_ref, rt_ref = refs[2 * n_in + n_x:]
    y = None
    c0 = 0
    for k, wd in enumerate(widths):
        o = _read_rows(refs[2 * k:2 * k + 2])
        part = jnp.dot(o.astype(BF16), w_ref[c0:c0 + wd, :], preferred_element_type=F32)
        y = part if y is None else y + part
        c0 += wd
    xn = _read_rows(x_refs) + mod_ref[0, 2:3, :] * y
    xn_ref[...] = xn
    h2 = _rms(xn, nw_ref[...]) * (1.0 + mod_ref[0, 4:5, :]) + mod_ref[0, 3:4, :]
    for s in range(ROW_TILE):
        h2_ref[pl.ds(s, TM, stride=ROW_TILE), :] = h2[:, s * LANE:(s + 1) * LANE]
    h_hi = h2.astype(BF16)
    h_lo = (h2 - h_hi.astype(F32)).astype(BF16)
    wr = wr_ref[...]
    hw = jnp.dot(h_hi, wr, preferred_element_type=F32)
    logits = hw[:, :LANE] + hw[:, LANE:] + jnp.dot(h_lo, wr[:, :LANE], preferred_element_type=F32)
    rt_ref[...] = _route(logits)


def _outproj(parts, w_bf16, x, mod, norm_w2, w_router):
    widths = tuple(p[0].shape[1] for p in parts)
    in_specs, args = [], []
    for p, wd in zip(parts, widths):
        specs, ops = _row_specs(p, wd)
        in_specs += specs
        args += ops
    specs, ops = _row_specs(x, D_MODEL)
    in_specs += specs
    args += ops
    in_specs += [pl.BlockSpec((D_MODEL, D_MODEL), lambda t: (0, 0)),
                 pl.BlockSpec((1, MOD_ROWS, D_MODEL), lambda t: (_mod_index(t), 0, 0)),
                 pl.BlockSpec((1, D_MODEL), lambda t: (0, 0)),
                 pl.BlockSpec((D_MODEL, 2 * LANE), lambda t: (0, 0))]
    args += [w_bf16, mod, norm_w2.reshape(1, D_MODEL), w_router]
    return pl.pallas_call(
        functools.partial(_outproj_kernel, widths=widths, n_x=len(ops)),
        out_shape=(jax.ShapeDtypeStruct((N_TOK, D_MODEL), F32),
                   jax.ShapeDtypeStruct((N_TOK * ROW_TILE, LANE), F32),
                   jax.ShapeDtypeStruct((N_TOK, LANE), F32)),
        grid=(N_TILES,),
        in_specs=in_specs,
        out_specs=(pl.BlockSpec((TM, D_MODEL), lambda t: (t, 0)),
                   pl.BlockSpec((TM * ROW_TILE, LANE), lambda t: (t, 0)),
                   pl.BlockSpec((TM, LANE), lambda t: (t, 0))),
        compiler_params=_params(("parallel",)),
        name="outproj",
    )(*args)


def _expert_kernel(blk_e_ref, nact_ref, rowtok_ref, h2_hbm, wg_ref, wu_ref, wd_ref, y_ref,
                   x_even, x_odd, sems, wg_s, wu_s, wd_s):
    i = pl.program_id(0)
    n_act = nact_ref[0]
    e = blk_e_ref[i]
    prev = blk_e_ref[jnp.maximum(i - 1, 0)]
    bufs = (x_even, x_odd)

    def row_copy(blk, r, parity):
        tok = rowtok_ref[blk * MOE_ROWS + r]
        return pltpu.make_async_copy(
            h2_hbm.at[pl.ds(pl.multiple_of(tok * ROW_TILE, ROW_TILE), ROW_TILE), :],
            bufs[parity].at[pl.ds(pl.multiple_of(r * ROW_TILE, ROW_TILE), ROW_TILE), :],
            sems.at[parity])

    def wait_block(parity):
        pltpu.make_async_copy(h2_hbm.at[pl.ds(0, MOE_ROWS * ROW_TILE), :], bufs[parity], sems.at[parity]).wait()

    @pl.when(i == 0)
    def _():
        def start_row(r, carry):
            row_copy(0, r, 0).start()
            return carry
        lax.fori_loop(0, MOE_ROWS, start_row, 0)

    @pl.when((i == 0) | (e != prev))
    def _():
        wg_s[...] = wg_ref[...].astype(BF16)
        wu_s[...] = wu_ref[...].astype(BF16)
        wd_s[...] = wd_ref[...].astype(BF16)

    for parity in (0, 1):
        @pl.when((i == n_act) & (i % 2 == parity))
        def _():
            wait_block(parity)

        @pl.when((i < n_act) & (i % 2 == parity))
        def _():
            wait_block(parity)
            for r in range(MOE_ROWS):
                row_copy(i + 1, r, 1 - parity).start(priority=r % 2)
            x = jnp.concatenate([bufs[parity][pl.ds(s, MOE_ROWS, stride=ROW_TILE), :] for s in range(ROW_TILE)],
                                axis=1).astype(BF16)
            g = jnp.dot(x, wg_s[...], preferred_element_type=F32)
            u = jnp.dot(x, wu_s[...], preferred_element_type=F32)
            y_ref[...] = jnp.dot((_silu(g) * u).astype(BF16), wd_s[...], preferred_element_type=F32)

    @pl.when(i >= nact_ref[0])
    def _():
        y_ref[...] = jnp.zeros_like(y_ref)


def _experts(h2_tiles, row_tok, blk_e, n_active, w_gate, w_up, w_down, layer):
    n_rows = row_tok.shape[0]
    n_blocks = n_rows // MOE_ROWS
    grid_spec = pltpu.PrefetchScalarGridSpec(
        num_scalar_prefetch=3,
        grid=(n_blocks,),
        in_specs=[pl.BlockSpec(memory_space=pl.ANY),
                  pl.BlockSpec((None, None, D_MODEL, D_EXPERT), lambda i, be, na, rt: (layer, be[i], 0, 0)),
                  pl.BlockSpec((None, None, D_MODEL, D_EXPERT), lambda i, be, na, rt: (layer, be[i], 0, 0)),
                  pl.BlockSpec((None, None, D_EXPERT, D_MODEL), lambda i, be, na, rt: (layer, be[i], 0, 0))],
        out_specs=pl.BlockSpec((MOE_ROWS, D_MODEL), lambda i, be, na, rt: (i, 0)),
        scratch_shapes=[pltpu.VMEM((MOE_ROWS * ROW_TILE, LANE), F32),
                        pltpu.VMEM((MOE_ROWS * ROW_TILE, LANE), F32),
                        pltpu.SemaphoreType.DMA((2,)),
                        pltpu.VMEM((D_MODEL, D_EXPERT), BF16),
                        pltpu.VMEM((D_MODEL, D_EXPERT), BF16),
                        pltpu.VMEM((D_EXPERT, D_MODEL), BF16)])
    return pl.pallas_call(
        _expert_kernel,
        out_shape=jax.ShapeDtypeStruct((n_rows, D_MODEL), F32),
        grid_spec=grid_spec,
        compiler_params=_params(("arbitrary",)),
        name="experts",
    )(blk_e, n_active, row_tok, h2_tiles, w_gate, w_up, w_down)


def _combine_kernel(x_ref, y0_ref, y1_ref, rt_ref, mod_ref, o_ref):
    rt = rt_ref[...]
    y = rt[:, 2:3] * y0_ref[...] + rt[:, 3:4] * y1_ref[...]
    o_ref[...] = x_ref[...] + mod_ref[0, 5:6, :] * y


def _combine(x, y0, y1, route, mod):
    row = pl.BlockSpec((TM, D_MODEL), lambda t: (t, 0))
    return pl.pallas_call(
        _combine_kernel,
        out_shape=jax.ShapeDtypeStruct((N_TOK, D_MODEL), F32),
        grid=(N_TILES,),
        in_specs=[row, row, row,
                  pl.BlockSpec((TM, LANE), lambda t: (t, 0)),
                  pl.BlockSpec((1, MOD_ROWS, D_MODEL), lambda t: (_mod_index(t), 0, 0))],
        out_specs=row,
        compiler_params=_params(("parallel",)),
        name="combine",
    )(x, y0, y1, route, mod)


def _take_rows(a, idx):
    return a.at[idx].get(mode="promise_in_bounds")


def _moe(x, h2, route, mod, w_gate, w_up, w_down, layer):
    n_asg = 2 * N_TOK
    flat_e = route[:, :2].astype(jnp.int32).reshape(n_asg)
    flat_tok = jnp.arange(n_asg, dtype=jnp.int32) // 2
    onehot = (flat_e[:, None] == jnp.arange(N_EXPERTS, dtype=jnp.int32)[None, :]).astype(jnp.int32)
    csum = jnp.cumsum(onehot, axis=0)
    rank = jnp.take_along_axis(csum, flat_e[:, None], axis=1)[:, 0] - 1
    counts = csum[-1]
    padded = (counts + MOE_ROWS - 1) // MOE_ROWS * MOE_ROWS
    pad_end = jnp.cumsum(padded)
    pad_start = pad_end - padded
    dest = pad_start[flat_e] + rank
    n_rows = n_asg + (N_EXPERTS + 1) * MOE_ROWS
    n_blocks = n_rows // MOE_ROWS
    row_tok = (jnp.arange(n_rows, dtype=jnp.int32) % N_TOK).at[dest].set(
        flat_tok, unique_indices=True, mode="promise_in_bounds")
    blk_start = jnp.arange(n_blocks, dtype=jnp.int32) * MOE_ROWS
    blk_e = jnp.sum((blk_start[:, None] >= pad_end[None, :]).astype(jnp.int32), axis=1)
    blk_e = jnp.minimum(blk_e, N_EXPERTS - 1)
    n_active = (pad_end[-1:] // MOE_ROWS).astype(jnp.int32)
    yb = _experts(h2, row_tok, blk_e, n_active, w_gate, w_up, w_down, layer)
    dest2 = dest.reshape(N_TOK, 2)
    return _combine(x, _take_rows(yb, dest2[:, 0]), _take_rows(yb, dest2[:, 1]), route, mod)


def _final_kernel(x_ref, w_ref, o_ref):
    o_ref[...] = _rms(x_ref[...], w_ref[...])


def _final_norm(x, w, tile0, n_rows):
    return pl.pallas_call(
        _final_kernel,
        out_shape=jax.ShapeDtypeStruct((n_rows, D_MODEL), F32),
        grid=(n_rows // TM,),
        in_specs=[pl.BlockSpec((TM, D_MODEL), lambda t: (t + tile0, 0)),
                  pl.BlockSpec((1, D_MODEL), lambda t: (0, 0))],
        out_specs=pl.BlockSpec((TM, D_MODEL), lambda t: (t, 0)),
        compiler_params=_params(("parallel",)),
        name="final_norm",
    )(x, w.reshape(1, D_MODEL))


def _hgrn_kernel(*refs, seq_len, layer, context):
    if context:
        (q_ref, ff_ref, fb_ref, v_ref, g_ref, lbl_ref, nw_ref,
         o_ref, sfin_ref, of_s, ob_s, sf_s, sb_s) = refs
    else:
        (q_ref, ff_ref, fb_ref, v_ref, g_ref, lbl_ref, nw_ref, s0_ref,
         o_ref, of_s, ob_s, sf_s, sb_s) = refs
    c = CHUNK_A
    n = seq_len // c

    def lower_bound(d):
        z = lbl_ref[d]
        e = jnp.exp(z - jnp.max(z, axis=0, keepdims=True))
        return jnp.sum(e[:layer + 1], axis=0, keepdims=True) / jnp.sum(e, axis=0, keepdims=True)

    lb_f = lower_bound(0)
    lb_b = lower_bound(1)
    row = lax.broadcasted_iota(jnp.int32, (c, c), 0)
    col = lax.broadcasted_iota(jnp.int32, (c, c), 1)
    causal = row >= col
    tri_f = causal.astype(F32)
    tri_b = (row <= col).astype(F32)

    if context:
        sf_s[...] = jnp.zeros_like(sf_s)
        sb_s[...] = jnp.zeros_like(sb_s)
    else:
        sf_s[...] = s0_ref[0, 0, 0].T
        sb_s[...] = s0_ref[0, 1, 0].T

    group = 4
    dirs = ((ff_ref, lb_f, tri_f, causal, c - 1, sf_s, of_s),
            (fb_ref, lb_b, tri_b, row <= col, 0, sb_s, ob_s))

    def body(i, carry):
        items = []
        for g in range(group):
            j = i * group + g
            items.append((0, pl.ds(pl.multiple_of(j * c, c), c)))
            items.append((1, pl.ds(pl.multiple_of((n - 1 - j) * c, c), c)))
        fs = [dirs[d][1] + (1.0 - dirs[d][1]) * _sigmoid(dirs[d][0][rows, :]) for d, rows in items]
        bs = [_tri_matmul(dirs[d][2], jnp.log(f)) for (d, _), f in zip(items, fs)]
        b_lasts = [b[dirs[d][4]:dirs[d][4] + 1, :] for (d, _), b in zip(items, bs)]
        q_ins = [_silu(q_ref[rows, :]) * jnp.exp(b) for (_, rows), b in zip(items, bs)]
        a_s = [jnp.where(dirs[d][3], _dot_nt(q_in, (1.0 - f) * jnp.exp(-b)), 0.0)
               for (d, _), q_in, f, b in zip(items, q_ins, fs, bs)]
        vs = [v_ref[rows, :] for _, rows in items]
        o_intra = [_dot(a, v) for a, v in zip(a_s, vs)]
        u_ts = [_dot_tn(v, (1.0 - f) * jnp.exp(b_last - b)) for v, f, b, b_last in zip(vs, fs, bs, b_lasts)]
        for d in (0, 1):
            st_ref, out_s = dirs[d][5], dirs[d][6]
            st = st_ref[...]
            for k, (dk, rows) in enumerate(items):
                if dk == d:
                    out_s[rows, :] = o_intra[k] + _dot_nt(q_ins[k], st)
                    st = st * jnp.exp(b_lasts[k]) + u_ts[k]
            st_ref[...] = st
        return carry

    lax.fori_loop(0, n // group, body, 0)

    if context:
        sfin_ref[0, 0, 0] = sf_s[...].T
        sfin_ref[0, 1, 0] = sb_s[...].T

    nw = nw_ref[...]

    def epilogue(j, carry):
        rows = pl.ds(pl.multiple_of(j * TM, TM), TM)
        o = of_s[rows, :] + ob_s[rows, :]
        o_ref[rows, :] = _rms(o, nw) * _silu(g_ref[rows, :])
        return carry

    lax.fori_loop(0, seq_len // TM, epilogue, 0)


def _hgrn(proj, lb_logits, norm_w, layer, context, s0=None):
    seq_len = SEQ if context else DEC_SEQ
    bsz = BATCH if context else DEC_BATCH
    row0 = 0 if context else N_PROMPT // seq_len

    def col(k):
        return pl.BlockSpec((seq_len, LANE), lambda b, h: (row0 + b, k * H_A + h))

    in_specs = [col(0), col(1), col(2), col(3), col(4),
                pl.BlockSpec((2, DEPTH + 1, LANE), lambda b, h: (0, 0, h)),
                pl.BlockSpec((1, LANE), lambda b, h: (0, 0))]
    args = [proj, proj, proj, proj, proj, lb_logits, norm_w.reshape(1, LANE)]
    o_shape = jax.ShapeDtypeStruct((bsz * seq_len, W_A), F32)
    o_spec = pl.BlockSpec((seq_len, LANE), lambda b, h: (b, h))
    scratch = [pltpu.VMEM((seq_len, LANE), F32), pltpu.VMEM((seq_len, LANE), F32),
               pltpu.VMEM((LANE, LANE), F32), pltpu.VMEM((LANE, LANE), F32)]
    kern = functools.partial(_hgrn_kernel, seq_len=seq_len, layer=layer, context=context)
    if context:
        return pl.pallas_call(
            kern,
            out_shape=(o_shape, jax.ShapeDtypeStruct((bsz, 2, H_A, DK_A, DK_A), F32)),
            grid=(bsz, H_A),
            in_specs=in_specs,
            out_specs=(o_spec, pl.BlockSpec((1, 2, 1, DK_A, DK_A), lambda b, h: (b, 0, h, 0, 0))),
            scratch_shapes=scratch,
            compiler_params=_params(("parallel", "parallel")),
            name="hgrn_ctx",
        )(*args)
    in_specs += [pl.BlockSpec((1, 2, 1, DK_A, DK_A), lambda b, h: (b, 0, h, 0, 0))]
    args += [s0]
    return pl.pallas_call(
        kern,
        out_shape=o_shape,
        grid=(bsz, H_A),
        in_specs=in_specs,
        out_specs=o_spec,
        scratch_shapes=scratch,
        compiler_params=_params(("parallel", "parallel")),
        name="hgrn_dec",
    )(*args)


ATT_TQ = 256
COL_QD = 5 * W_A // LANE
COL_KD = COL_QD + QK_B // LANE
COL_VD = COL_KD + QK_B // LANE


def _attn_kernel(*refs, seq_len, layer, context):
    if context:
        q_ref, k_ref, v_ref, lam_ref, nw_ref, o_ref, k_s, v_s = refs
    else:
        q_ref, k_ref, v_ref, ck_ref, cv_ref, lam_ref, nw_ref, o_ref, k_s, v_s = refs

    @pl.when(pl.program_id(2) == 0)
    def _():
        k_s[0:seq_len, :] = k_ref[...].astype(BF16)
        v_s[0:seq_len, :] = v_ref[...].astype(BF16)
        if not context:
            k_s[seq_len:, :] = ck_ref[0].astype(BF16)
            v_s[seq_len:, :] = cv_ref[0].astype(BF16)

    lam_init = 0.8 - 0.6 * math.exp(-0.3 * layer)
    lp = lam_ref[...]
    lam = (jnp.exp(jnp.sum(lp[0:1] * lp[1:2], axis=1, keepdims=True))
           - jnp.exp(jnp.sum(lp[2:3] * lp[3:4], axis=1, keepdims=True)) + lam_init)

    q = q_ref[...] * (DH_B ** -0.5)
    lane = lax.broadcasted_iota(jnp.int32, q.shape, 1)
    k = k_s[...]

    def softmax_map(first):
        s = _dot_nt(jnp.where((lane < DH_B) == first, q, 0.0), k)
        p = jnp.exp(s - jnp.max(s, axis=1, keepdims=True))
        return p, jnp.sum(p, axis=1, keepdims=True)

    p0, l0 = softmax_map(True)
    p1, l1 = softmax_map(False)
    a = p0 - (lam * l0 / l1) * p1
    o = jnp.dot(a.astype(BF16), v_s[...], preferred_element_type=F32) / l0
    o_ref[...] = _rms(o, nw_ref[...]) * (1.0 - lam_init)


def _attn(proj, lam_p, norm_w, layer, context, cache_k=None, cache_v=None):
    seq_len = SEQ if context else DEC_SEQ
    bsz = BATCH if context else DEC_BATCH
    row0 = 0 if context else N_PROMPT // seq_len
    nq = seq_len // ATT_TQ
    tile0 = row0 * nq
    t_k = seq_len if context else seq_len + PAST_LEN
    in_specs = [pl.BlockSpec((ATT_TQ, LANE), lambda b, h, i: (tile0 + b * nq + i, COL_QD + h)),
                pl.BlockSpec((seq_len, LANE), lambda b, h, i: (row0 + b, COL_KD + h)),
                pl.BlockSpec((seq_len, LANE), lambda b, h, i: (row0 + b, COL_VD + h))]
    args = [proj, proj, proj]
    if not context:
        in_specs += [pl.BlockSpec((1, PAST_LEN, LANE), lambda b, h, i: (b, 0, h)),
                     pl.BlockSpec((1, PAST_LEN, LANE), lambda b, h, i: (b, 0, h))]
        args += [cache_k, cache_v]
    in_specs += [pl.BlockSpec((4, DH_B), lambda b, h, i: (0, 0)),
                 pl.BlockSpec((1, LANE), lambda b, h, i: (0, 0))]
    args += [lam_p, norm_w.reshape(1, LANE)]
    return pl.pallas_call(
        functools.partial(_attn_kernel, seq_len=seq_len, layer=layer, context=context),
        out_shape=jax.ShapeDtypeStruct((bsz * seq_len, W_B), F32),
        grid=(bsz, H_B, nq),
        in_specs=in_specs,
        out_specs=pl.BlockSpec((ATT_TQ, LANE), lambda b, h, i: (b * nq + i, h)),
        scratch_shapes=[pltpu.VMEM((t_k, LANE), BF16), pltpu.VMEM((t_k, LANE), BF16)],
        compiler_params=_params(("parallel", "parallel", "arbitrary")),
        name="attn_ctx" if context else "attn_dec",
    )(*args)


COL_GATES = 4 * W_C // LANE
GATE_CHUNKS = 4


def _gdn_gates_kernel(g_ref, alog_ref, dtb_ref, col_ref, row_ref):
    c = GDN_CHUNK
    row = lax.broadcasted_iota(jnp.int32, (c, c), 0)
    col = lax.broadcasted_iota(jnp.int32, (c, c), 1)
    lane = lax.broadcasted_iota(jnp.int32, (c, LANE), 1)
    lower = (row >= col).astype(F32)
    upper = (row <= col).astype(F32)
    for j in range(GATE_CHUNKS):
        raw = g_ref[j * c:(j + 1) * c, :]
        z = raw + dtb_ref[...]
        softplus = jnp.maximum(z, 0.0) + jnp.log(1.0 + jnp.exp(-jnp.abs(z)))
        g = -jnp.exp(alog_ref[...]) * softplus
        out = jnp.where(lane < H_C, _tri_matmul(lower, g),
                        jnp.where(lane < 2 * H_C, _tri_matmul(upper, g), _sigmoid(raw)))
        col_ref[j * c:(j + 1) * c, :] = out
        row_ref[j] = out.T


def _gdn_gates(proj, alog_lane, dtb_lane):
    c = GDN_CHUNK
    n = N_TOK // c
    return pl.pallas_call(
        _gdn_gates_kernel,
        out_shape=(jax.ShapeDtypeStruct((N_TOK, LANE), F32), jax.ShapeDtypeStruct((n, LANE, c), F32)),
        grid=(n // GATE_CHUNKS,),
        in_specs=[pl.BlockSpec((GATE_CHUNKS * c, LANE), lambda i: (i, COL_GATES)),
                  pl.BlockSpec((1, LANE), lambda i: (0, 0)),
                  pl.BlockSpec((1, LANE), lambda i: (0, 0))],
        out_specs=(pl.BlockSpec((GATE_CHUNKS * c, LANE), lambda i: (i, 0)),
                   pl.BlockSpec((GATE_CHUNKS, LANE, c), lambda i: (i, 0, 0))),
        compiler_params=_params(("parallel",)),
        name="gdn_gates",
    )(proj, alog_lane, dtb_lane)


def _unit_tri_inverse_pairs(ms, row, col):
    c = ms[0].shape[0]
    zero = jnp.zeros((c, c), BF16)

    def mm(xs, ys):
        out = []
        for x, y in zip(xs, ys):
            yb = y.astype(BF16)
            blockdiag = jnp.concatenate([jnp.concatenate([yb[:, :c], zero], axis=1),
                                         jnp.concatenate([zero, yb[:, c:]], axis=1)], axis=0)
            out.append(jnp.dot(x.astype(BF16), blockdiag, preferred_element_type=F32))
        return out

    def add(xs, ys):
        return [x + y for x, y in zip(xs, ys)]

    eye = (row == col).astype(F32)
    a = [jnp.where((row // 16) == (col // 16), m, 0.0) for m in ms]
    a2 = mm(a, a)
    a4 = mm(a2, a2)
    a8 = mm(a4, a4)
    t = [eye - x for x in a]
    t = add(t, mm(t, a2))
    t = add(t, mm(t, a4))
    t = add(t, mm(t, a8))
    blk = 32
    while blk <= c:
        off = ((row // blk) == (col // blk)) & ((row // (blk // 2)) != (col // (blk // 2)))
        corr = mm(mm(t, [jnp.where(off, m, 0.0) for m in ms]), t)
        t = [x - y for x, y in zip(t, corr)]
        blk *= 2
    return t


def _gdn_kernel(*refs, seq_len, context):
    if context:
        (q_ref, k_ref, v_ref, go_ref, cwq_ref, cwk_ref, cwv_ref, gcol_ref, grow_ref, nw_ref,
         o_ref, sfin_ref, su_s, pr_s, of_s, ob_s, sf_s, sb_s) = refs
    else:
        (q_ref, k_ref, v_ref, go_ref, cwq_ref, cwk_ref, cwv_ref, gcol_ref, grow_ref, nw_ref, s0_ref,
         o_ref, su_s, pr_s, of_s, ob_s, sf_s, sb_s) = refs
    c = GDN_CHUNK
    n = seq_len // c
    head = pl.program_id(1)
    row = lax.broadcasted_iota(jnp.int32, (c, c), 0)
    col = lax.broadcasted_iota(jnp.int32, (c, c), 1)
    row2 = lax.broadcasted_iota(jnp.int32, (c, 2 * c), 0)
    col2 = lax.broadcasted_iota(jnp.int32, (c, 2 * c), 1) & (c - 1)
    lane = lax.broadcasted_iota(jnp.int32, (c, LANE), 1)
    rowi = lax.broadcasted_iota(jnp.int32, (c, LANE), 0)

    def gc_row_last(ci, d):
        gc_row = grow_ref[ci, pl.ds(d * H_C + head, 1), :]
        return gc_row, (gc_row[:, c - 1:c] if d == 0 else gc_row[:, 0:1])

    def chunk_inputs(ci):
        r0 = pl.multiple_of(ci * c, c)
        rows = pl.ds(r0, c)

        def conv(x_ref, w_ref):
            cur = x_ref[rows, :]
            before = x_ref[pl.ds(pl.multiple_of(jnp.maximum(r0 - 8, 0), 8), 8), :]
            after = x_ref[pl.ds(pl.multiple_of(jnp.minimum(r0 + c, seq_len - 8), 8), 8), :]
            prev_row = jnp.where(ci > 0, before[7:8, :], 0.0)
            next_row = jnp.where(ci < n - 1, after[0:1, :], 0.0)
            xm1 = jnp.where(rowi == 0, prev_row, pltpu.roll(cur, 1, 0))
            xp1 = jnp.where(rowi == c - 1, next_row, pltpu.roll(cur, c - 1, 0))
            w = w_ref[...]
            return _silu(xm1 * w[0:1, :] + cur * w[1:2, :] + xp1 * w[2:3, :])

        q = conv(q_ref, cwq_ref)
        k = conv(k_ref, cwk_ref)
        vn = conv(v_ref, cwv_ref)
        qn = q * lax.rsqrt(jnp.sum(q * q, axis=1, keepdims=True) + EPS) * (DK_C ** -0.5)
        kn = k * lax.rsqrt(jnp.sum(k * k, axis=1, keepdims=True) + EPS)
        kq = _dot_nt(jnp.concatenate([kn, qn], axis=0), kn)
        gates = gcol_ref[rows, :]
        per_dir = []
        ms = []
        for d in (0, 1):
            gc = jnp.sum(jnp.where(lane == d * H_C + head, gates, 0.0), axis=1, keepdims=True)
            beta = jnp.sum(jnp.where(lane == (2 + d) * H_C + head, gates, 0.0), axis=1, keepdims=True)
            gc_row, gc_last = gc_row_last(ci, d)
            incl = (row >= col) if d == 0 else (row <= col)
            strict = (row > col) if d == 0 else (row < col)
            decay = jnp.where(incl, jnp.exp(gc - gc_row), 0.0)
            m = jnp.where(strict, kq[:c] * beta * decay, 0.0)
            per_dir.append((gc, beta, kq[c:] * decay, (kn * jnp.exp(gc_last - gc)).T))
            ms.append(m)
        return jnp.concatenate(ms, axis=1), (rows, qn, kn, vn, per_dir)

    def chunk_outputs(ci, t, rest):
        rows, qn, kn, vn, per_dir = rest
        base = pl.multiple_of(ci * 2 * c, 2 * c)
        for d in (0, 1):
            gc, beta, qk, kd_t = per_dir[d]
            e = jnp.exp(gc)
            uw = _dot(t[:, d * c:(d + 1) * c], jnp.concatenate([vn * beta, kn * (beta * e)], axis=1))
            cross = _dot(jnp.concatenate([kd_t, qk], axis=0), uw)
            su_s[d, rows, :] = cross[:c, :c]
            (of_s, ob_s)[d][rows, :] = cross[c:, :c]
            pr_s[d, pl.ds(base, c), :] = cross[:c, c:].astype(BF16)
            pr_s[d, pl.ds(base + c, c), :] = (qn * e - cross[c:, c:]).astype(BF16)

    group = min(8, n)

    def prepare(j, carry):
        cis = [j * group + g for g in range(group)]
        staged = [chunk_inputs(ci) for ci in cis]
        ts = _unit_tri_inverse_pairs([s[0] for s in staged], row2, col2)
        for ci, t, s in zip(cis, ts, staged):
            chunk_outputs(ci, t, s[1])
        return carry

    lax.fori_loop(0, n // group, prepare, 0)

    if context:
        sf_s[...] = jnp.zeros_like(sf_s)
        sb_s[...] = jnp.zeros_like(sb_s)
    else:
        sf_s[...] = s0_ref[0, 0, 0]
        sb_s[...] = s0_ref[0, 1, 0]

    def advance(ci, d, s_ref, out_s):
        rows = pl.ds(pl.multiple_of(ci * c, c), c)
        _, gc_last = gc_row_last(ci, d)
        s = s_ref[...]
        ps = jnp.dot(pr_s[d, pl.ds(pl.multiple_of(ci * 2 * c, 2 * c), 2 * c), :], s.astype(BF16),
                     preferred_element_type=F32)
        out_s[rows, :] = out_s[rows, :] + ps[c:]
        s_ref[...] = s * jnp.exp(gc_last) - ps[:c] + su_s[d, rows, :]

    def body(i, carry):
        advance(i, 0, sf_s, of_s)
        advance(n - 1 - i, 1, sb_s, ob_s)
        return carry

    lax.fori_loop(0, n, body, 0)

    if context:
        sfin_ref[0, 0, 0] = sf_s[...]
        sfin_ref[0, 1, 0] = sb_s[...]

    nw = nw_ref[...]

    def epilogue(j, carry):
        rows = pl.ds(pl.multiple_of(j * TM, TM), TM)
        o = of_s[rows, :] + ob_s[rows, :]
        o_ref[rows, :] = _rms(o, nw) * _silu(go_ref[rows, :])
        return carry

    lax.fori_loop(0, seq_len // TM, epilogue, 0)


def _gdn(proj, conv_w, gcol, grow, norm_w, context, s0=None):
    seq_len = SEQ if context else DEC_SEQ
    bsz = BATCH if context else DEC_BATCH
    row0 = 0 if context else N_PROMPT // seq_len
    nc = seq_len // GDN_CHUNK

    def col(k):
        return pl.BlockSpec((seq_len, LANE), lambda b, h: (row0 + b, k * H_C + h))

    def cw(k):
        return pl.BlockSpec((3, LANE), lambda b, h: (0, k * H_C + h))

    in_specs = [col(0), col(1), col(2), col(3), cw(0), cw(1), cw(2),
                pl.BlockSpec((seq_len, LANE), lambda b, h: (row0 + b, 0)),
                pl.BlockSpec((nc, LANE, GDN_CHUNK), lambda b, h: (row0 + b, 0, 0)),
                pl.BlockSpec((1, LANE), lambda b, h: (0, 0))]
    args = [proj, proj, proj, proj, conv_w, conv_w, conv_w, gcol, grow, norm_w.reshape(1, LANE)]
    o_shape = jax.ShapeDtypeStruct((bsz * seq_len, W_C), F32)
    o_spec = pl.BlockSpec((seq_len, LANE), lambda b, h: (b, h))
    seq_buf = pltpu.VMEM((seq_len, LANE), F32)
    state_buf = pltpu.VMEM((DK_C, DK_C), F32)
    scratch = [pltpu.VMEM((2, seq_len, LANE), F32), pltpu.VMEM((2, 2 * seq_len, LANE), BF16),
               seq_buf, seq_buf, state_buf, state_buf]
    kern = functools.partial(_gdn_kernel, seq_len=seq_len, context=context)
    state_spec = pl.BlockSpec((1, 2, 1, DK_C, DK_C), lambda b, h: (b, 0, h, 0, 0))
    if context:
        return pl.pallas_call(
            kern,
            out_shape=(o_shape, jax.ShapeDtypeStruct((bsz, 2, H_C, DK_C, DK_C), F32)),
            grid=(bsz, H_C),
            in_specs=in_specs,
            out_specs=(o_spec, state_spec),
            scratch_shapes=scratch,
            compiler_params=_params(("parallel", "parallel")),
            name="gdn_ctx",
        )(*args)
    in_specs += [state_spec]
    args += [s0]
    return pl.pallas_call(
        kern,
        out_shape=o_shape,
        grid=(bsz, H_C),
        in_specs=in_specs,
        out_specs=o_spec,
        scratch_shapes=scratch,
        compiler_params=_params(("parallel", "parallel")),
        name="gdn_dec",
    )(*args)


def kernel(x_prompt, x_sample, c, cache_diff_k, cache_diff_v, state_hgrn, state_gdn, c_ctx,
           ada_w, ada_b, norm_w, final_norm_w, w_in_ab, hgrn_lb_logits, hgrn_norm_w,
           diff_lambda, diff_norm_w, w_in_c, gdn_conv_w, gdn_a_log, gdn_dt_bias, gdn_norm_w,
           w_out, moe_router_group, moe_router_expert, moe_w_gate, moe_w_up, moe_w_down):
    x = (x_prompt.reshape(N_PROMPT, D_MODEL), x_sample.reshape(N_SAMPLE, D_MODEL))
    cvec = jnp.concatenate([c_ctx[None, :], c, jnp.zeros((MOD_ROWS - N_MOD, D_MODEL), F32)], axis=0)
    mod_all = _adaln(cvec, ada_w, ada_b)[:, :N_MOD].reshape(DEPTH, N_MOD, 6, D_MODEL)
    mod_all = jnp.pad(mod_all, ((0, 0), (0, 0), (0, MOD_ROWS - 6), (0, 0)))
    cos, sin = _rope_tables()
    new_k, new_v, new_hgrn, new_gdn = [], [], [], []
    for l in range(DEPTH):
        i = l // 2
        mod = mod_all[l]
        if l % 2 == 0:
            proj = _inproj(x, mod, norm_w[l, 0], w_in_ab[i].astype(BF16),
                           rope=(cos, sin, (COL_QD * LANE, COL_VD * LANE)))
            oh_ctx, s_h = _hgrn(proj, hgrn_lb_logits, hgrn_norm_w[i], l, True)
            oh_dec = _hgrn(proj, hgrn_lb_logits, hgrn_norm_w[i], l, False, s0=state_hgrn[:, i])
            od_ctx = _attn(proj, diff_lambda[i], diff_norm_w[i], l, True)
            od_dec = _attn(proj, diff_lambda[i], diff_norm_w[i], l, False,
                           cache_k=cache_diff_k[:, i].reshape(DEC_BATCH, PAST_LEN, QK_B),
                           cache_v=cache_diff_v[:, i].reshape(DEC_BATCH, PAST_LEN, W_B))
            parts = ((oh_ctx, oh_dec), (od_ctx, od_dec))
            new_k.append(proj[:N_PROMPT, COL_KD * LANE:COL_VD * LANE].reshape(BATCH, SEQ, H_B, 2, DH_B))
            new_v.append(proj[:N_PROMPT, COL_VD * LANE:].reshape(BATCH, SEQ, H_B, DV_B))
            new_hgrn.append(s_h)
        else:
            w_c = jnp.pad(w_in_c[i], ((0, 0), (0, (COL_GATES + 1) * LANE - IN_C))).astype(BF16)
            proj = _inproj(x, mod, norm_w[l, 0], w_c)
            pad = jnp.zeros((LANE - 2 * H_C,), F32)
            alog_lane = jnp.concatenate([gdn_a_log[i, 0], gdn_a_log[i, 1], pad]).reshape(1, LANE)
            dtb_lane = jnp.concatenate([gdn_dt_bias[i, 0], gdn_dt_bias[i, 1], pad]).reshape(1, LANE)
            gcol, grow = _gdn_gates(proj, alog_lane, dtb_lane)
            oc_ctx, s_c = _gdn(proj, gdn_conv_w[i], gcol, grow, gdn_norm_w[i], True)
            oc_dec = _gdn(proj, gdn_conv_w[i], gcol, grow, gdn_norm_w[i], False, s0=state_gdn[:, i])
            parts = ((oc_ctx, oc_dec),)
            new_gdn.append(s_c)
        w_router = jnp.concatenate(
            [moe_router_group[l], moe_router_expert[l],
             jnp.zeros((D_MODEL, LANE - N_GROUPS - N_EXPERTS), F32)], axis=1)
        w_router_hi = w_router.astype(BF16)
        w_router = jnp.concatenate([w_router_hi, (w_router - w_router_hi.astype(F32)).astype(BF16)], axis=1)
        x, h2, route = _outproj(parts, w_out[l].astype(BF16), x, mod, norm_w[l, 1], w_router)
        x = _moe(x, h2, route, mod, moe_w_gate, moe_w_up, moe_w_down, l)
    y_prompt = _final_norm(x, final_norm_w, 0, N_PROMPT).reshape(BATCH, SEQ, D_MODEL)
    y_sample = _final_norm(x, final_norm_w, PROMPT_TILES, N_SAMPLE).reshape(DEC_BATCH, DEC_SEQ, D_MODEL)
    return (y_prompt, y_sample, jnp.stack(new_k, axis=1), jnp.stack(new_v, axis=1),
            jnp.stack(new_hgrn, axis=1), jnp.stack(new_gdn, axis=1))
```

```python
import functools
import math

import jax
import jax.numpy as jnp
from jax import lax
from jax.experimental import pallas as pl
from jax.experimental.pallas import tpu as pltpu

F32 = jnp.float32
BF16 = jnp.bfloat16
HIGHEST = lax.Precision.HIGHEST

D_MODEL = 1024
BATCH = 16
SEQ = 256
DEPTH = 2
DEC_BATCH = 4
DEC_SEQ = 4096
PAST_LEN = 256
GRID_W = 64
H_A = 4
DK_A = 128
W_A = 512
CHUNK_A = 32
H_B = 4
DH_B = 64
DV_B = 128
QK_B = 512
W_B = 512
ROPE_BASE = 10000.0
H_C = 8
DK_C = 128
W_C = 1024
N_GROUPS = 4
E_PER_GROUP = 8
N_EXPERTS = 32
D_EXPERT = 512
EPS = 1e-6
IN_AB = 5 * W_A + 2 * QK_B + W_B
IN_C = 4 * W_C + 4 * H_C

LANE = 128
N_PROMPT = BATCH * SEQ
N_SAMPLE = DEC_BATCH * DEC_SEQ
N_TOK = N_PROMPT + N_SAMPLE
TM = 256
N_TILES = N_TOK // TM
PROMPT_TILES = N_PROMPT // TM
TILES_PER_SAMPLE = DEC_SEQ // TM
N_MOD = 1 + DEC_BATCH
MOD_ROWS = 8
MOE_ROWS = 256
MOE_AHEAD = 2
ROW_TILE = D_MODEL // LANE
GDN_CHUNK = 128
VMEM_LIMIT = 56 * 1024 * 1024


def _mod_index(t):
    return jnp.where(t < PROMPT_TILES, 0, 1 + (t - PROMPT_TILES) // TILES_PER_SAMPLE)


def _sigmoid(x):
    return 1.0 / (1.0 + jnp.exp(-x))


def _silu(x):
    return x * _sigmoid(x)


def _rms(x, w):
    return x * lax.rsqrt(jnp.mean(x * x, axis=-1, keepdims=True) + EPS) * w


def _dot(a, b):
    return jnp.dot(a.astype(BF16), b.astype(BF16), preferred_element_type=F32)


def _dot_nt(a, b):
    return lax.dot_general(a.astype(BF16), b.astype(BF16), (((1,), (1,)), ((), ())),
                           preferred_element_type=F32)


def _dot_tn(a, b):
    return lax.dot_general(a.astype(BF16), b.astype(BF16), (((0,), (0,)), ((), ())),
                           preferred_element_type=F32)


def _dot_f32(a, b):
    return jnp.dot(a, b, precision=HIGHEST, preferred_element_type=F32)


def _tri_matmul(tri, x):
    hi = x.astype(BF16)
    rem = x - hi.astype(F32)
    mid = rem.astype(BF16)
    lo = (rem - mid.astype(F32)).astype(BF16)
    n = x.shape[1]
    r = jnp.dot(tri.astype(BF16), jnp.concatenate([hi, mid, lo], axis=1), preferred_element_type=F32)
    return r[:, :n] + r[:, n:2 * n] + r[:, 2 * n:]


def _params(sem):
    return pltpu.CompilerParams(dimension_semantics=sem, vmem_limit_bytes=VMEM_LIMIT)


def _adaln_kernel(c_ref, w_ref, b_ref, o_ref):
    s = _silu(c_ref[...])
    o_ref[0] = _dot(s, w_ref[0]) + b_ref[0]


def _adaln(cvec, ada_w, ada_b):
    nb = 4
    wb = 6 * D_MODEL // nb
    return pl.pallas_call(
        _adaln_kernel,
        out_shape=jax.ShapeDtypeStruct((DEPTH, MOD_ROWS, 6 * D_MODEL), F32),
        grid=(DEPTH, nb),
        in_specs=[pl.BlockSpec((MOD_ROWS, D_MODEL), lambda l, j: (0, 0)),
                  pl.BlockSpec((1, D_MODEL, wb), lambda l, j: (l, 0, j)),
                  pl.BlockSpec((1, 1, wb), lambda l, j: (l, 0, j))],
        out_specs=pl.BlockSpec((1, MOD_ROWS, wb), lambda l, j: (l, 0, j)),
        compiler_params=_params(("parallel", "parallel")),
        name="adaln",
    )(cvec, ada_w, ada_b.reshape(DEPTH, 1, 6 * D_MODEL))


def _row_specs(x, width):
    if isinstance(x, tuple):
        return ([pl.BlockSpec((TM, width), lambda t: (jnp.minimum(t, PROMPT_TILES - 1), 0)),
                 pl.BlockSpec((TM, width), lambda t: (jnp.maximum(t - PROMPT_TILES, 0), 0))], list(x))
    return [pl.BlockSpec((TM, width), lambda t: (t, 0))], [x]


def _read_rows(refs):
    if len(refs) == 2:
        return jnp.where(pl.program_id(0) < PROMPT_TILES, refs[0][...], refs[1][...])
    return refs[0][...]


def _inproj_kernel(*refs, n_x, rope_cols):
    x = _read_rows(refs[:n_x])
    if rope_cols is None:
        mod_ref, nw_ref, w_ref, o_ref = refs[n_x:]
    else:
        mod_ref, nw_ref, w_ref, cos_ref, sin_ref, o_ref = refs[n_x:]
    h = _rms(x, nw_ref[...]) * (1.0 + mod_ref[0, 1:2, :]) + mod_ref[0, 0:1, :]
    r = jnp.dot(h.astype(BF16), w_ref[...], preferred_element_type=F32)
    if rope_cols is None:
        o_ref[...] = r
        return
    lo, hi = rope_cols
    o_ref[:, :lo] = r[:, :lo]
    o_ref[:, hi:] = r[:, hi:]
    cos = cos_ref[...]
    sin = sin_ref[...]
    lane = lax.broadcasted_iota(jnp.int32, (TM, LANE), 1)
    upper = (lane & 16) != 0
    for c0 in range(lo, hi, LANE):
        v = r[:, c0:c0 + LANE]
        partner = jnp.where(upper, pltpu.roll(v, 16, 1), pltpu.roll(v, LANE - 16, 1))
        o_ref[:, c0:c0 + LANE] = v * cos + partner * sin


def _inproj(x, mod, norm_w, w_bf16, rope=None):
    p = w_bf16.shape[1]
    in_specs, args = _row_specs(x, D_MODEL)
    n_x = len(args)
    in_specs += [pl.BlockSpec((1, MOD_ROWS, D_MODEL), lambda t: (_mod_index(t), 0, 0)),
                 pl.BlockSpec((1, D_MODEL), lambda t: (0, 0)),
                 pl.BlockSpec((D_MODEL, p), lambda t: (0, 0))]
    args += [mod, norm_w.reshape(1, D_MODEL), w_bf16]
    rope_cols = None
    if rope is not None:
        cos, sin, rope_cols = rope

        def rope_index(t):
            return (jnp.where(t < PROMPT_TILES, 0, 1 + (t - PROMPT_TILES) % TILES_PER_SAMPLE), 0)

        in_specs += [pl.BlockSpec((TM, LANE), rope_index), pl.BlockSpec((TM, LANE), rope_index)]
        args += [cos, sin]
    return pl.pallas_call(
        functools.partial(_inproj_kernel, n_x=n_x, rope_cols=rope_cols),
        out_shape=jax.ShapeDtypeStruct((N_TOK, p), F32),
        grid=(N_TILES,),
        in_specs=in_specs,
        out_specs=pl.BlockSpec((TM, p), lambda t: (t, 0)),
        compiler_params=_params(("parallel",)),
        name="inproj",
    )(*args)


def _rope_tables():
    lane = jnp.arange(LANE)
    d = lane % DH_B
    use_col = (d // 32) == 1
    j = d % 16
    upper = ((d % 32) // 16) == 1
    inv_freq = ROPE_BASE ** (-j.astype(F32) / 16.0)
    t = jnp.arange(DEC_SEQ)
    row = (t // GRID_W).astype(F32)
    col = (t % GRID_W).astype(F32)
    pos = jnp.where(use_col[None, :], col[:, None], row[:, None])
    ang = pos * inv_freq[None, :]
    cos = jnp.cos(ang)
    sin = jnp.where(upper[None, :], jnp.sin(ang), -jnp.sin(ang))
    cos = jnp.concatenate([jnp.ones((TM, LANE), F32), cos], axis=0)
    sin = jnp.concatenate([jnp.zeros((TM, LANE), F32), sin], axis=0)
    return cos, sin


def _route(logits):
    lane = lax.broadcasted_iota(jnp.int32, logits.shape, 1)
    lanef = lane.astype(F32)
    neg = jnp.float32(-jnp.inf)
    gl = jnp.where(lane < N_GROUPS, logits, neg)
    gmax = jnp.max(gl, axis=1, keepdims=True)
    gsel = jnp.min(jnp.where(gl == gmax, lanef, float(LANE)), axis=1, keepdims=True)
    p_grp = 1.0 / jnp.sum(jnp.exp(gl - gmax), axis=1, keepdims=True)
    lo = float(N_GROUPS) + gsel * float(E_PER_GROUP)
    el = jnp.where((lanef >= lo) & (lanef < lo + float(E_PER_GROUP)), logits, neg)
    v1 = jnp.max(el, axis=1, keepdims=True)
    i1 = jnp.min(jnp.where(el == v1, lanef, float(LANE)), axis=1, keepdims=True)
    el2 = jnp.where(lanef == i1, neg, el)
    v2 = jnp.max(el2, axis=1, keepdims=True)
    i2 = jnp.min(jnp.where(el2 == v2, lanef, float(LANE)), axis=1, keepdims=True)
    t = jnp.exp(v2 - v1)
    w1 = p_grp / (1.0 + t)
    w2 = p_grp * t / (1.0 + t)
    out = jnp.where(lane == 0, i1 - float(N_GROUPS), 0.0)
    out = jnp.where(lane == 1, i2 - float(N_GROUPS), out)
    out = jnp.where(lane == 2, w1, out)
    out = jnp.where(lane == 3, w2, out)
    return out


def _outproj_kernel(*refs, widths, n_x):
    n_in = len(widths)
    x_refs = refs[2 * n_in:2 * n_in + n_x]
    w_ref, mod_ref, nw_ref, wr_ref, xn_ref, h2_ref, rt_ref = refs[2 * n_in + n_x:]
    y = None
    c0 = 0
    for k, wd in enumerate(widths):
        o = _read_rows(refs[2 * k:2 * k + 2])
        part = jnp.dot(o.astype(BF16), w_ref[c0:c0 + wd, :], preferred_element_type=F32)
        y = part if y is None else y + part
        c0 += wd
    xn = _read_rows(x_refs) + mod_ref[0, 2:3, :] * y
    xn_ref[...] = xn
    h2 = _rms(xn, nw_ref[...]) * (1.0 + mod_ref[0, 4:5, :]) + mod_ref[0, 3:4, :]
    for s in range(ROW_TILE):
        h2_ref[pl.ds(s, TM, stride=ROW_TILE), :] = h2[:, s * LANE:(s + 1) * LANE]
    h_hi = h2.astype(BF16)
    h_lo = (h2 - h_hi.astype(F32)).astype(BF16)
    wr = wr_ref[...]
    hw = jnp.dot(h_hi, wr, preferred_element_type=F32)
    logits = hw[:, :LANE] + hw[:, LANE:] + jnp.dot(h_lo, wr[:, :LANE], preferred_element_type=F32)
    rt_ref[...] = _route(logits)


def _outproj(parts, w_bf16, x, mod, norm_w2, w_router):
    widths = tuple(p[0].shape[1] for p in parts)
    in_specs, args = [], []
    for p, wd in zip(parts, widths):
        specs, ops = _row_specs(p, wd)
        in_specs += specs
        args += ops
    specs, ops = _row_specs(x, D_MODEL)
    in_specs += specs
    args += ops
    in_specs += [pl.BlockSpec((D_MODEL, D_MODEL), lambda t: (0, 0)),
                 pl.BlockSpec((1, MOD_ROWS, D_MODEL), lambda t: (_mod_index(t), 0, 0)),
                 pl.BlockSpec((1, D_MODEL), lambda t: (0, 0)),
                 pl.BlockSpec((D_MODEL, 2 * LANE), lambda t: (0, 0))]
    args += [w_bf16, mod, norm_w2.reshape(1, D_MODEL), w_router]
    return pl.pallas_call(
        functools.partial(_outproj_kernel, widths=widths, n_x=len(ops)),
        out_shape=(jax.ShapeDtypeStruct((N_TOK, D_MODEL), F32),
                   jax.ShapeDtypeStruct((N_TOK * ROW_TILE, LANE), F32),
                   jax.ShapeDtypeStruct((N_TOK, LANE), F32)),
        grid=(N_TILES,),
        in_specs=in_specs,
        out_specs=(pl.BlockSpec((TM, D_MODEL), lambda t: (t, 0)),
                   pl.BlockSpec((TM * ROW_TILE, LANE), lambda t: (t, 0)),
                   pl.BlockSpec((TM, LANE), lambda t: (t, 0))),
        compiler_params=_params(("parallel",)),
        name="outproj",
    )(*args)


def _expert_kernel(blk_e_ref, nact_ref, rowtok_ref, h2_hbm, wg_ref, wu_ref, wd_ref, y_ref,
                   x_0, x_1, x_2, sems, wg_s, wu_s, wd_s):
    i = pl.program_id(0)
    n_act = nact_ref[0]
    e = blk_e_ref[i]
    prev = blk_e_ref[jnp.maximum(i - 1, 0)]
    bufs = (x_0, x_1, x_2)
    ring = MOE_AHEAD + 1
    assert ring == len(bufs)

    def row_copy(blk, r, slot):
        tok = rowtok_ref[blk * MOE_ROWS + r]
        return pltpu.make_async_copy(
            h2_hbm.at[pl.ds(pl.multiple_of(tok * ROW_TILE, ROW_TILE), ROW_TILE), :],
            bufs[slot].at[pl.ds(pl.multiple_of(r * ROW_TILE, ROW_TILE), ROW_TILE), :],
            sems.at[slot])

    def wait_block(slot):
        pltpu.make_async_copy(h2_hbm.at[pl.ds(0, MOE_ROWS * ROW_TILE), :], bufs[slot], sems.at[slot]).wait()

    @pl.when(i == 0)
    def _():
        for blk in range(MOE_AHEAD):
            def start_row(r, carry):
                row_copy(blk, r, blk).start()
                return carry
            lax.fori_loop(0, MOE_ROWS, start_row, 0)

    @pl.when((i == 0) | (e != prev))
    def _():
        wg_s[...] = wg_ref[...].astype(BF16)
        wu_s[...] = wu_ref[...].astype(BF16)
        wd_s[...] = wd_ref[...].astype(BF16)

    for slot in range(ring):
        @pl.when((i >= n_act) & (i < n_act + MOE_AHEAD) & (i % ring == slot))
        def _():
            wait_block(slot)

        @pl.when((i < n_act) & (i % ring == slot))
        def _():
            wait_block(slot)
            for r in range(MOE_ROWS):
                row_copy(i + MOE_AHEAD, r, (slot + MOE_AHEAD) % ring).start(priority=r % 2)
            x = jnp.concatenate([bufs[slot][pl.ds(s, MOE_ROWS, stride=ROW_TILE), :] for s in range(ROW_TILE)],
                                axis=1).astype(BF16)
            g = jnp.dot(x, wg_s[...], preferred_element_type=F32)
            u = jnp.dot(x, wu_s[...], preferred_element_type=F32)
            y_ref[...] = jnp.dot((_silu(g) * u).astype(BF16), wd_s[...], preferred_element_type=F32)

    @pl.when(i >= nact_ref[0])
    def _():
        y_ref[...] = jnp.zeros_like(y_ref)


def _experts(h2_tiles, row_tok, blk_e, n_active, w_gate, w_up, w_down, layer):
    n_rows = row_tok.shape[0]
    n_blocks = n_rows // MOE_ROWS
    grid_spec = pltpu.PrefetchScalarGridSpec(
        num_scalar_prefetch=3,
        grid=(n_blocks,),
        in_specs=[pl.BlockSpec(memory_space=pl.ANY),
                  pl.BlockSpec((None, None, D_MODEL, D_EXPERT), lambda i, be, na, rt: (layer, be[i], 0, 0)),
                  pl.BlockSpec((None, None, D_MODEL, D_EXPERT), lambda i, be, na, rt: (layer, be[i], 0, 0)),
                  pl.BlockSpec((None, None, D_EXPERT, D_MODEL), lambda i, be, na, rt: (layer, be[i], 0, 0))],
        out_specs=pl.BlockSpec((MOE_ROWS, D_MODEL), lambda i, be, na, rt: (i, 0)),
        scratch_shapes=[pltpu.VMEM((MOE_ROWS * ROW_TILE, LANE), F32),
                        pltpu.VMEM((MOE_ROWS * ROW_TILE, LANE), F32),
                        pltpu.VMEM((MOE_ROWS * ROW_TILE, LANE), F32),
                        pltpu.SemaphoreType.DMA((MOE_AHEAD + 1,)),
                        pltpu.VMEM((D_MODEL, D_EXPERT), BF16),
                        pltpu.VMEM((D_MODEL, D_EXPERT), BF16),
                        pltpu.VMEM((D_EXPERT, D_MODEL), BF16)])
    return pl.pallas_call(
        _expert_kernel,
        out_shape=jax.ShapeDtypeStruct((n_rows, D_MODEL), F32),
        grid_spec=grid_spec,
        compiler_params=_params(("arbitrary",)),
        name="experts",
    )(blk_e, n_active, row_tok, h2_tiles, w_gate, w_up, w_down)


def _combine_kernel(x_ref, y0_ref, y1_ref, rt_ref, mod_ref, o_ref):
    rt = rt_ref[...]
    y = rt[:, 2:3] * y0_ref[...] + rt[:, 3:4] * y1_ref[...]
    o_ref[...] = x_ref[...] + mod_ref[0, 5:6, :] * y


def _combine(x, y0, y1, route, mod):
    row = pl.BlockSpec((TM, D_MODEL), lambda t: (t, 0))
    return pl.pallas_call(
        _combine_kernel,
        out_shape=jax.ShapeDtypeStruct((N_TOK, D_MODEL), F32),
        grid=(N_TILES,),
        in_specs=[row, row, row,
                  pl.BlockSpec((TM, LANE), lambda t: (t, 0)),
                  pl.BlockSpec((1, MOD_ROWS, D_MODEL), lambda t: (_mod_index(t), 0, 0))],
        out_specs=row,
        compiler_params=_params(("parallel",)),
        name="combine",
    )(x, y0, y1, route, mod)


def _take_rows(a, idx):
    return a.at[idx].get(mode="promise_in_bounds")


def _moe(x, h2, route, mod, w_gate, w_up, w_down, layer):
    n_asg = 2 * N_TOK
    flat_e = route[:, :2].astype(jnp.int32).reshape(n_asg)
    flat_tok = jnp.arange(n_asg, dtype=jnp.int32) // 2
    onehot = (flat_e[:, None] == jnp.arange(N_EXPERTS, dtype=jnp.int32)[None, :]).astype(jnp.int32)
    csum = jnp.cumsum(onehot, axis=0)
    rank = jnp.take_along_axis(csum, flat_e[:, None], axis=1)[:, 0] - 1
    counts = csum[-1]
    padded = (counts + MOE_ROWS - 1) // MOE_ROWS * MOE_ROWS
    pad_end = jnp.cumsum(padded)
    pad_start = pad_end - padded
    dest = pad_start[flat_e] + rank
    n_rows = n_asg + (N_EXPERTS + MOE_AHEAD) * MOE_ROWS
    n_blocks = n_rows // MOE_ROWS
    row_tok = (jnp.arange(n_rows, dtype=jnp.int32) % N_TOK).at[dest].set(
        flat_tok, unique_indices=True, mode="promise_in_bounds")
    blk_start = jnp.arange(n_blocks, dtype=jnp.int32) * MOE_ROWS
    blk_e = jnp.sum((blk_start[:, None] >= pad_end[None, :]).astype(jnp.int32), axis=1)
    blk_e = jnp.minimum(blk_e, N_EXPERTS - 1)
    n_active = (pad_end[-1:] // MOE_ROWS).astype(jnp.int32)
    yb = _experts(h2, row_tok, blk_e, n_active, w_gate, w_up, w_down, layer)
    dest2 = dest.reshape(N_TOK, 2)
    return _combine(x, _take_rows(yb, dest2[:, 0]), _take_rows(yb, dest2[:, 1]), route, mod)


def _final_kernel(x_ref, w_ref, o_ref):
    o_ref[...] = _rms(x_ref[...], w_ref[...])


def _final_norm(x, w, tile0, n_rows):
    return pl.pallas_call(
        _final_kernel,
        out_shape=jax.ShapeDtypeStruct((n_rows, D_MODEL), F32),
        grid=(n_rows // TM,),
        in_specs=[pl.BlockSpec((TM, D_MODEL), lambda t: (t + tile0, 0)),
                  pl.BlockSpec((1, D_MODEL), lambda t: (0, 0))],
        out_specs=pl.BlockSpec((TM, D_MODEL), lambda t: (t, 0)),
        compiler_params=_params(("parallel",)),
        name="final_norm",
    )(x, w.reshape(1, D_MODEL))


def _hgrn_kernel(*refs, seq_len, layer, context):
    if context:
        (q_ref, ff_ref, fb_ref, v_ref, g_ref, lbl_ref, nw_ref,
         o_ref, sfin_ref, of_s, ob_s, sf_s, sb_s) = refs
    else:
        (q_ref, ff_ref, fb_ref, v_ref, g_ref, lbl_ref, nw_ref, s0_ref,
         o_ref, of_s, ob_s, sf_s, sb_s) = refs
    c = CHUNK_A
    n = seq_len // c

    def lower_bound(d):
        z = lbl_ref[d]
        e = jnp.exp(z - jnp.max(z, axis=0, keepdims=True))
        return jnp.sum(e[:layer + 1], axis=0, keepdims=True) / jnp.sum(e, axis=0, keepdims=True)

    lb_f = lower_bound(0)
    lb_b = lower_bound(1)
    row = lax.broadcasted_iota(jnp.int32, (c, c), 0)
    col = lax.broadcasted_iota(jnp.int32, (c, c), 1)
    causal = row >= col
    tri_f = causal.astype(F32)
    tri_b = (row <= col).astype(F32)

    if context:
        sf_s[...] = jnp.zeros_like(sf_s)
        sb_s[...] = jnp.zeros_like(sb_s)
    else:
        sf_s[...] = s0_ref[0, 0, 0].T
        sb_s[...] = s0_ref[0, 1, 0].T

    group = 8
    dirs = ((ff_ref, lb_f, tri_f, causal, c - 1, sf_s, of_s),
            (fb_ref, lb_b, tri_b, row <= col, 0, sb_s, ob_s))

    def body(i, carry):
        items = []
        for g in range(group):
            j = i * group + g
            items.append((0, pl.ds(pl.multiple_of(j * c, c), c)))
            items.append((1, pl.ds(pl.multiple_of((n - 1 - j) * c, c), c)))
        fs = [dirs[d][1] + (1.0 - dirs[d][1]) * _sigmoid(dirs[d][0][rows, :]) for d, rows in items]
        bs = [_tri_matmul(dirs[d][2], jnp.log(f)) for (d, _), f in zip(items, fs)]
        b_lasts = [b[dirs[d][4]:dirs[d][4] + 1, :] for (d, _), b in zip(items, bs)]
        q_ins = [_silu(q_ref[rows, :]) * jnp.exp(b) for (_, rows), b in zip(items, bs)]
        a_s = [jnp.where(dirs[d][3], _dot_nt(q_in, (1.0 - f) * jnp.exp(-b)), 0.0)
               for (d, _), q_in, f, b in zip(items, q_ins, fs, bs)]
        vs = [v_ref[rows, :] for _, rows in items]
        o_intra = [_dot(a, v) for a, v in zip(a_s, vs)]
        u_ts = [_dot_tn(v, (1.0 - f) * jnp.exp(b_last - b)) for v, f, b, b_last in zip(vs, fs, bs, b_lasts)]
        for d in (0, 1):
            st_ref, out_s = dirs[d][5], dirs[d][6]
            st = st_ref[...]
            for k, (dk, rows) in enumerate(items):
                if dk == d:
                    out_s[rows, :] = o_intra[k] + _dot_nt(q_ins[k], st)
                    st = st * jnp.exp(b_lasts[k]) + u_ts[k]
            st_ref[...] = st
        return carry

    lax.fori_loop(0, n // group, body, 0)

    if context:
        sfin_ref[0, 0, 0] = sf_s[...].T
        sfin_ref[0, 1, 0] = sb_s[...].T

    nw = nw_ref[...]

    def epilogue(j, carry):
        rows = pl.ds(pl.multiple_of(j * TM, TM), TM)
        o = of_s[rows, :] + ob_s[rows, :]
        o_ref[rows, :] = _rms(o, nw) * _silu(g_ref[rows, :])
        return carry

    lax.fori_loop(0, seq_len // TM, epilogue, 0)


def _hgrn(proj, lb_logits, norm_w, layer, context, s0=None):
    seq_len = SEQ if context else DEC_SEQ
    bsz = BATCH if context else DEC_BATCH
    row0 = 0 if context else N_PROMPT // seq_len

    def col(k):
        return pl.BlockSpec((seq_len, LANE), lambda b, h: (row0 + b, k * H_A + h))

    in_specs = [col(0), col(1), col(2), col(3), col(4),
                pl.BlockSpec((2, DEPTH + 1, LANE), lambda b, h: (0, 0, h)),
                pl.BlockSpec((1, LANE), lambda b, h: (0, 0))]
    args = [proj, proj, proj, proj, proj, lb_logits, norm_w.reshape(1, LANE)]
    o_shape = jax.ShapeDtypeStruct((bsz * seq_len, W_A), F32)
    o_spec = pl.BlockSpec((seq_len, LANE), lambda b, h: (b, h))
    scratch = [pltpu.VMEM((seq_len, LANE), F32), pltpu.VMEM((seq_len, LANE), F32),
               pltpu.VMEM((LANE, LANE), F32), pltpu.VMEM((LANE, LANE), F32)]
    kern = functools.partial(_hgrn_kernel, seq_len=seq_len, layer=layer, context=context)
    if context:
        return pl.pallas_call(
            kern,
            out_shape=(o_shape, jax.ShapeDtypeStruct((bsz, 2, H_A, DK_A, DK_A), F32)),
            grid=(bsz, H_A),
            in_specs=in_specs,
            out_specs=(o_spec, pl.BlockSpec((1, 2, 1, DK_A, DK_A), lambda b, h: (b, 0, h, 0, 0))),
            scratch_shapes=scratch,
            compiler_params=_params(("parallel", "parallel")),
            name="hgrn_ctx",
        )(*args)
    in_specs += [pl.BlockSpec((1, 2, 1, DK_A, DK_A), lambda b, h: (b, 0, h, 0, 0))]
    args += [s0]
    return pl.pallas_call(
        kern,
        out_shape=o_shape,
        grid=(bsz, H_A),
        in_specs=in_specs,
        out_specs=o_spec,
        scratch_shapes=scratch,
        compiler_params=_params(("parallel", "parallel")),
        name="hgrn_dec",
    )(*args)


ATT_TQ = 256
COL_QD = 5 * W_A // LANE
COL_KD = COL_QD + QK_B // LANE
COL_VD = COL_KD + QK_B // LANE


def _attn_kernel(*refs, seq_len, layer, context):
    if context:
        q_ref, k_ref, v_ref, lam_ref, nw_ref, o_ref, k_s, v_s = refs
    else:
        q_ref, k_ref, v_ref, ck_ref, cv_ref, lam_ref, nw_ref, o_ref, k_s, v_s = refs

    @pl.when(pl.program_id(2) == 0)
    def _():
        k_s[0:seq_len, :] = k_ref[...].astype(BF16)
        v_s[0:seq_len, :] = v_ref[...].astype(BF16)
        if not context:
            k_s[seq_len:, :] = ck_ref[0].astype(BF16)
            v_s[seq_len:, :] = cv_ref[0].astype(BF16)

    lam_init = 0.8 - 0.6 * math.exp(-0.3 * layer)
    lp = lam_ref[...]
    lam = (jnp.exp(jnp.sum(lp[0:1] * lp[1:2], axis=1, keepdims=True))
           - jnp.exp(jnp.sum(lp[2:3] * lp[3:4], axis=1, keepdims=True)) + lam_init)

    q = q_ref[...] * (DH_B ** -0.5)
    lane = lax.broadcasted_iota(jnp.int32, q.shape, 1)
    k = k_s[...]

    def softmax_map(first):
        s = _dot_nt(jnp.where((lane < DH_B) == first, q, 0.0), k)
        p = jnp.exp(s - jnp.max(s, axis=1, keepdims=True))
        return p, jnp.sum(p, axis=1, keepdims=True)

    p0, l0 = softmax_map(True)
    p1, l1 = softmax_map(False)
    a = p0 - (lam * l0 / l1) * p1
    o = jnp.dot(a.astype(BF16), v_s[...], preferred_element_type=F32) / l0
    o_ref[...] = _rms(o, nw_ref[...]) * (1.0 - lam_init)


def _attn(proj, lam_p, norm_w, layer, context, cache_k=None, cache_v=None):
    seq_len = SEQ if context else DEC_SEQ
    bsz = BATCH if context else DEC_BATCH
    row0 = 0 if context else N_PROMPT // seq_len
    nq = seq_len // ATT_TQ
    tile0 = row0 * nq
    t_k = seq_len if context else seq_len + PAST_LEN
    in_specs = [pl.BlockSpec((ATT_TQ, LANE), lambda b, h, i: (tile0 + b * nq + i, COL_QD + h)),
                pl.BlockSpec((seq_len, LANE), lambda b, h, i: (row0 + b, COL_KD + h)),
                pl.BlockSpec((seq_len, LANE), lambda b, h, i: (row0 + b, COL_VD + h))]
    args = [proj, proj, proj]
    if not context:
        in_specs += [pl.BlockSpec((1, PAST_LEN, LANE), lambda b, h, i: (b, 0, h)),
                     pl.BlockSpec((1, PAST_LEN, LANE), lambda b, h, i: (b, 0, h))]
        args += [cache_k, cache_v]
    in_specs += [pl.BlockSpec((4, DH_B), lambda b, h, i: (0, 0)),
                 pl.BlockSpec((1, LANE), lambda b, h, i: (0, 0))]
    args += [lam_p, norm_w.reshape(1, LANE)]
    return pl.pallas_call(
        functools.partial(_attn_kernel, seq_len=seq_len, layer=layer, context=context),
        out_shape=jax.ShapeDtypeStruct((bsz * seq_len, W_B), F32),
        grid=(bsz, H_B, nq),
        in_specs=in_specs,
        out_specs=pl.BlockSpec((ATT_TQ, LANE), lambda b, h, i: (b * nq + i, h)),
        scratch_shapes=[pltpu.VMEM((t_k, LANE), BF16), pltpu.VMEM((t_k, LANE), BF16)],
        compiler_params=_params(("parallel", "parallel", "arbitrary")),
        name="attn_ctx" if context else "attn_dec",
    )(*args)


COL_GATES = 4 * W_C // LANE
GATE_CHUNKS = 4


def _gdn_gates_kernel(g_ref, alog_ref, dtb_ref, col_ref, row_ref):
    c = GDN_CHUNK
    row = lax.broadcasted_iota(jnp.int32, (c, c), 0)
    col = lax.broadcasted_iota(jnp.int32, (c, c), 1)
    lane = lax.broadcasted_iota(jnp.int32, (c, LANE), 1)
    lower = (row >= col).astype(F32)
    upper = (row <= col).astype(F32)
    for j in range(GATE_CHUNKS):
        raw = g_ref[j * c:(j + 1) * c, :]
        z = raw + dtb_ref[...]
        softplus = jnp.maximum(z, 0.0) + jnp.log(1.0 + jnp.exp(-jnp.abs(z)))
        g = -jnp.exp(alog_ref[...]) * softplus
        out = jnp.where(lane < H_C, _tri_matmul(lower, g),
                        jnp.where(lane < 2 * H_C, _tri_matmul(upper, g), _sigmoid(raw)))
        col_ref[j * c:(j + 1) * c, :] = out
        row_ref[j] = out.T


def _gdn_gates(proj, alog_lane, dtb_lane):
    c = GDN_CHUNK
    n = N_TOK // c
    return pl.pallas_call(
        _gdn_gates_kernel,
        out_shape=(jax.ShapeDtypeStruct((N_TOK, LANE), F32), jax.ShapeDtypeStruct((n, LANE, c), F32)),
        grid=(n // GATE_CHUNKS,),
        in_specs=[pl.BlockSpec((GATE_CHUNKS * c, LANE), lambda i: (i, COL_GATES)),
                  pl.BlockSpec((1, LANE), lambda i: (0, 0)),
                  pl.BlockSpec((1, LANE), lambda i: (0, 0))],
        out_specs=(pl.BlockSpec((GATE_CHUNKS * c, LANE), lambda i: (i, 0)),
                   pl.BlockSpec((GATE_CHUNKS, LANE, c), lambda i: (i, 0, 0))),
        compiler_params=_params(("parallel",)),
        name="gdn_gates",
    )(proj, alog_lane, dtb_lane)


def _unit_tri_inverse_pairs(ms, row, col):
    c = ms[0].shape[0]
    zero = jnp.zeros((c, c), BF16)

    def mm(xs, ys):
        out = []
        for x, y in zip(xs, ys):
            yb = y.astype(BF16)
            blockdiag = jnp.concatenate([jnp.concatenate([yb[:, :c], zero], axis=1),
                                         jnp.concatenate([zero, yb[:, c:]], axis=1)], axis=0)
            out.append(jnp.dot(x.astype(BF16), blockdiag, preferred_element_type=F32))
        return out

    def add(xs, ys):
        return [x + y for x, y in zip(xs, ys)]

    eye = (row == col).astype(F32)
    a = [jnp.where((row // 16) == (col // 16), m, 0.0) for m in ms]
    a2 = mm(a, a)
    a4 = mm(a2, a2)
    a8 = mm(a4, a4)
    t = [eye - x for x in a]
    t = add(t, mm(t, a2))
    t = add(t, mm(t, a4))
    t = add(t, mm(t, a8))
    blk = 32
    while blk <= c:
        off = ((row // blk) == (col // blk)) & ((row // (blk // 2)) != (col // (blk // 2)))
        corr = mm(mm(t, [jnp.where(off, m, 0.0) for m in ms]), t)
        t = [x - y for x, y in zip(t, corr)]
        blk *= 2
    return t


def _gdn_kernel(*refs, seq_len, context):
    if context:
        (q_ref, k_ref, v_ref, go_ref, cwq_ref, cwk_ref, cwv_ref, gcol_ref, grow_ref, nw_ref,
         o_ref, sfin_ref, su_s, pr_s, of_s, ob_s, sf_s, sb_s) = refs
    else:
        (q_ref, k_ref, v_ref, go_ref, cwq_ref, cwk_ref, cwv_ref, gcol_ref, grow_ref, nw_ref, s0_ref,
         o_ref, su_s, pr_s, of_s, ob_s, sf_s, sb_s) = refs
    c = GDN_CHUNK
    n = seq_len // c
    head = pl.program_id(1)
    row = lax.broadcasted_iota(jnp.int32, (c, c), 0)
    col = lax.broadcasted_iota(jnp.int32, (c, c), 1)
    row2 = lax.broadcasted_iota(jnp.int32, (c, 2 * c), 0)
    col2 = lax.broadcasted_iota(jnp.int32, (c, 2 * c), 1) & (c - 1)
    lane = lax.broadcasted_iota(jnp.int32, (c, LANE), 1)
    rowi = lax.broadcasted_iota(jnp.int32, (c, LANE), 0)

    def gc_row_last(ci, d):
        gc_row = grow_ref[ci, pl.ds(d * H_C + head, 1), :]
        return gc_row, (gc_row[:, c - 1:c] if d == 0 else gc_row[:, 0:1])

    def chunk_inputs(ci):
        r0 = pl.multiple_of(ci * c, c)
        rows = pl.ds(r0, c)

        def conv(x_ref, w_ref):
            cur = x_ref[rows, :]
            before = x_ref[pl.ds(pl.multiple_of(jnp.maximum(r0 - 8, 0), 8), 8), :]
            after = x_ref[pl.ds(pl.multiple_of(jnp.minimum(r0 + c, seq_len - 8), 8), 8), :]
            prev_row = jnp.where(ci > 0, before[7:8, :], 0.0)
            next_row = jnp.where(ci < n - 1, after[0:1, :], 0.0)
            xm1 = jnp.where(rowi == 0, prev_row, pltpu.roll(cur, 1, 0))
            xp1 = jnp.where(rowi == c - 1, next_row, pltpu.roll(cur, c - 1, 0))
            w = w_ref[...]
            return _silu(xm1 * w[0:1, :] + cur * w[1:2, :] + xp1 * w[2:3, :])

        q = conv(q_ref, cwq_ref)
        k = conv(k_ref, cwk_ref)
        vn = conv(v_ref, cwv_ref)
        qn = q * lax.rsqrt(jnp.sum(q * q, axis=1, keepdims=True) + EPS) * (DK_C ** -0.5)
        kn = k * lax.rsqrt(jnp.sum(k * k, axis=1, keepdims=True) + EPS)
        kq = _dot_nt(jnp.concatenate([kn, qn], axis=0), kn)
        gates = gcol_ref[rows, :]
        per_dir = []
        ms = []
        for d in (0, 1):
            gc = jnp.sum(jnp.where(lane == d * H_C + head, gates, 0.0), axis=1, keepdims=True)
            beta = jnp.sum(jnp.where(lane == (2 + d) * H_C + head, gates, 0.0), axis=1, keepdims=True)
            gc_row, gc_last = gc_row_last(ci, d)
            incl = (row >= col) if d == 0 else (row <= col)
            strict = (row > col) if d == 0 else (row < col)
            decay = jnp.where(incl, jnp.exp(gc - gc_row), 0.0)
            m = jnp.where(strict, kq[:c] * beta * decay, 0.0)
            per_dir.append((gc, beta, kq[c:] * decay, (kn * jnp.exp(gc_last - gc)).T))
            ms.append(m)
        return jnp.concatenate(ms, axis=1), (rows, qn, kn, vn, per_dir)

    def chunk_outputs(ci, t, rest):
        rows, qn, kn, vn, per_dir = rest
        base = pl.multiple_of(ci * 2 * c, 2 * c)
        for d in (0, 1):
            gc, beta, qk, kd_t = per_dir[d]
            e = jnp.exp(gc)
            uw = _dot(t[:, d * c:(d + 1) * c], jnp.concatenate([vn * beta, kn * (beta * e)], axis=1))
            cross = _dot(jnp.concatenate([kd_t, qk], axis=0), uw)
            su_s[d, rows, :] = cross[:c, :c]
            (of_s, ob_s)[d][rows, :] = cross[c:, :c]
            pr_s[d, pl.ds(base, c), :] = cross[:c, c:].astype(BF16)
            pr_s[d, pl.ds(base + c, c), :] = (qn * e - cross[c:, c:]).astype(BF16)

    group = min(8, n)

    def prepare(j, carry):
        cis = [j * group + g for g in range(group)]
        staged = [chunk_inputs(ci) for ci in cis]
        ts = _unit_tri_inverse_pairs([s[0] for s in staged], row2, col2)
        for ci, t, s in zip(cis, ts, staged):
            chunk_outputs(ci, t, s[1])
        return carry

    lax.fori_loop(0, n // group, prepare, 0)

    if context:
        sf_s[...] = jnp.zeros_like(sf_s)
        sb_s[...] = jnp.zeros_like(sb_s)
    else:
        sf_s[...] = s0_ref[0, 0, 0]
        sb_s[...] = s0_ref[0, 1, 0]

    def advance(ci, d, s_ref, out_s):
        rows = pl.ds(pl.multiple_of(ci * c, c), c)
        _, gc_last = gc_row_last(ci, d)
        s = s_ref[...]
        ps = jnp.dot(pr_s[d, pl.ds(pl.multiple_of(ci * 2 * c, 2 * c), 2 * c), :], s.astype(BF16),
                     preferred_element_type=F32)
        out_s[rows, :] = out_s[rows, :] + ps[c:]
        s_ref[...] = s * jnp.exp(gc_last) - ps[:c] + su_s[d, rows, :]

    def body(i, carry):
        advance(i, 0, sf_s, of_s)
        advance(n - 1 - i, 1, sb_s, ob_s)
        return carry

    lax.fori_loop(0, n, body, 0)

    if context:
        sfin_ref[0, 0, 0] = sf_s[...]
        sfin_ref[0, 1, 0] = sb_s[...]

    nw = nw_ref[...]

    def epilogue(j, carry):
        rows = pl.ds(pl.multiple_of(j * TM, TM), TM)
        o = of_s[rows, :] + ob_s[rows, :]
        o_ref[rows, :] = _rms(o, nw) * _silu(go_ref[rows, :])
        return carry

    lax.fori_loop(0, seq_len // TM, epilogue, 0)


def _gdn(proj, conv_w, gcol, grow, norm_w, context, s0=None):
    seq_len = SEQ if context else DEC_SEQ
    bsz = BATCH if context else DEC_BATCH
    row0 = 0 if context else N_PROMPT // seq_len
    nc = seq_len // GDN_CHUNK

    def col(k):
        return pl.BlockSpec((seq_len, LANE), lambda b, h: (row0 + b, k * H_C + h))

    def cw(k):
        return pl.BlockSpec((3, LANE), lambda b, h: (0, k * H_C + h))

    in_specs = [col(0), col(1), col(2), col(3), cw(0), cw(1), cw(2),
                pl.BlockSpec((seq_len, LANE), lambda b, h: (row0 + b, 0)),
                pl.BlockSpec((nc, LANE, GDN_CHUNK), lambda b, h: (row0 + b, 0, 0)),
                pl.BlockSpec((1, LANE), lambda b, h: (0, 0))]
    args = [proj, proj, proj, proj, conv_w, conv_w, conv_w, gcol, grow, norm_w.reshape(1, LANE)]
    o_shape = jax.ShapeDtypeStruct((bsz * seq_len, W_C), F32)
    o_spec = pl.BlockSpec((seq_len, LANE), lambda b, h: (b, h))
    seq_buf = pltpu.VMEM((seq_len, LANE), F32)
    state_buf = pltpu.VMEM((DK_C, DK_C), F32)
    scratch = [pltpu.VMEM((2, seq_len, LANE), F32), pltpu.VMEM((2, 2 * seq_len, LANE), BF16),
               seq_buf, seq_buf, state_buf, state_buf]
    kern = functools.partial(_gdn_kernel, seq_len=seq_len, context=context)
    state_spec = pl.BlockSpec((1, 2, 1, DK_C, DK_C), lambda b, h: (b, 0, h, 0, 0))
    if context:
        return pl.pallas_call(
            kern,
            out_shape=(o_shape, jax.ShapeDtypeStruct((bsz, 2, H_C, DK_C, DK_C), F32)),
            grid=(bsz, H_C),
            in_specs=in_specs,
            out_specs=(o_spec, state_spec),
            scratch_shapes=scratch,
            compiler_params=_params(("parallel", "parallel")),
            name="gdn_ctx",
        )(*args)
    in_specs += [state_spec]
    args += [s0]
    return pl.pallas_call(
        kern,
        out_shape=o_shape,
        grid=(bsz, H_C),
        in_specs=in_specs,
        out_specs=o_spec,
        scratch_shapes=scratch,
        compiler_params=_params(("parallel", "parallel")),
        name="gdn_dec",
    )(*args)


def kernel(x_prompt, x_sample, c, cache_diff_k, cache_diff_v, state_hgrn, state_gdn, c_ctx,
           ada_w, ada_b, norm_w, final_norm_w, w_in_ab, hgrn_lb_logits, hgrn_norm_w,
           diff_lambda, diff_norm_w, w_in_c, gdn_conv_w, gdn_a_log, gdn_dt_bias, gdn_norm_w,
           w_out, moe_router_group, moe_router_expert, moe_w_gate, moe_w_up, moe_w_down):
    x = (x_prompt.reshape(N_PROMPT, D_MODEL), x_sample.reshape(N_SAMPLE, D_MODEL))
    cvec = jnp.concatenate([c_ctx[None, :], c, jnp.zeros((MOD_ROWS - N_MOD, D_MODEL), F32)], axis=0)
    mod_all = _adaln(cvec, ada_w, ada_b)[:, :N_MOD].reshape(DEPTH, N_MOD, 6, D_MODEL)
    mod_all = jnp.pad(mod_all, ((0, 0), (0, 0), (0, MOD_ROWS - 6), (0, 0)))
    cos, sin = _rope_tables()
    new_k, new_v, new_hgrn, new_gdn = [], [], [], []
    for l in range(DEPTH):
        i = l // 2
        mod = mod_all[l]
        if l % 2 == 0:
            proj = _inproj(x, mod, norm_w[l, 0], w_in_ab[i].astype(BF16),
                           rope=(cos, sin, (COL_QD * LANE, COL_VD * LANE)))
            oh_ctx, s_h = _hgrn(proj, hgrn_lb_logits, hgrn_norm_w[i], l, True)
            oh_dec = _hgrn(proj, hgrn_lb_logits, hgrn_norm_w[i], l, False, s0=state_hgrn[:, i])
            od_ctx = _attn(proj, diff_lambda[i], diff_norm_w[i], l, True)
            od_dec = _attn(proj, diff_lambda[i], diff_norm_w[i], l, False,
                           cache_k=cache_diff_k[:, i].reshape(DEC_BATCH, PAST_LEN, QK_B),
                           cache_v=cache_diff_v[:, i].reshape(DEC_BATCH, PAST_LEN, W_B))
            parts = ((oh_ctx, oh_dec), (od_ctx, od_dec))
            new_k.append(proj[:N_PROMPT, COL_KD * LANE:COL_VD * LANE].reshape(BATCH, SEQ, H_B, 2, DH_B))
            new_v.append(proj[:N_PROMPT, COL_VD * LANE:].reshape(BATCH, SEQ, H_B, DV_B))
            new_hgrn.append(s_h)
        else:
            w_c = jnp.pad(w_in_c[i], ((0, 0), (0, (COL_GATES + 1) * LANE - IN_C))).astype(BF16)
            proj = _inproj(x, mod, norm_w[l, 0], w_c)
            pad = jnp.zeros((LANE - 2 * H_C,), F32)
            alog_lane = jnp.concatenate([gdn_a_log[i, 0], gdn_a_log[i, 1], pad]).reshape(1, LANE)
            dtb_lane = jnp.concatenate([gdn_dt_bias[i, 0], gdn_dt_bias[i, 1], pad]).reshape(1, LANE)
            gcol, grow = _gdn_gates(proj, alog_lane, dtb_lane)
            oc_ctx, s_c = _gdn(proj, gdn_conv_w[i], gcol, grow, gdn_norm_w[i], True)
            oc_dec = _gdn(proj, gdn_conv_w[i], gcol, grow, gdn_norm_w[i], False, s0=state_gdn[:, i])
            parts = ((oc_ctx, oc_dec),)
            new_gdn.append(s_c)
        w_router = jnp.concatenate(
            [moe_router_group[l], moe_router_expert[l],
             jnp.zeros((D_MODEL, LANE - N_GROUPS - N_EXPERTS), F32)], axis=1)
        w_router_hi = w_router.astype(BF16)
        w_router = jnp.concatenate([w_router_hi, (w_router - w_router_hi.astype(F32)).astype(BF16)], axis=1)
        x, h2, route = _outproj(parts, w_out[l].astype(BF16), x, mod, norm_w[l, 1], w_router)
        x = _moe(x, h2, route, mod, moe_w_gate, moe_w_up, moe_w_down, l)
    y_prompt = _final_norm(x, final_norm_w, 0, N_PROMPT).reshape(BATCH, SEQ, D_MODEL)
    y_sample = _final_norm(x, final_norm_w, PROMPT_TILES, N_SAMPLE).reshape(DEC_BATCH, DEC_SEQ, D_MODEL)
    return (y_prompt, y_sample, jnp.stack(new_k, axis=1), jnp.stack(new_v, axis=1),
            jnp.stack(new_hgrn, axis=1), jnp.stack(new_gdn, axis=1))
```

```python
import functools
import math

import jax
import jax.numpy as jnp
from jax import lax
from jax.experimental import pallas as pl
from jax.experimental.pallas import tpu as pltpu

F32 = jnp.float32
BF16 = jnp.bfloat16
HIGHEST = lax.Precision.HIGHEST

D_MODEL = 1024
BATCH = 16
SEQ = 256
DEPTH = 2
DEC_BATCH = 4
DEC_SEQ = 4096
PAST_LEN = 256
GRID_W = 64
H_A = 4
DK_A = 128
W_A = 512
CHUNK_A = 32
H_B = 4
DH_B = 64
DV_B = 128
QK_B = 512
W_B = 512
ROPE_BASE = 10000.0
H_C = 8
DK_C = 128
W_C = 1024
N_GROUPS = 4
E_PER_GROUP = 8
N_EXPERTS = 32
D_EXPERT = 512
EPS = 1e-6
IN_AB = 5 * W_A + 2 * QK_B + W_B
IN_C = 4 * W_C + 4 * H_C

LANE = 128
N_PROMPT = BATCH * SEQ
N_SAMPLE = DEC_BATCH * DEC_SEQ
N_TOK = N_PROMPT + N_SAMPLE
TM = 256
N_TILES = N_TOK // TM
PROMPT_TILES = N_PROMPT // TM
TILES_PER_SAMPLE = DEC_SEQ // TM
N_MOD = 1 + DEC_BATCH
MOD_ROWS = 8
MOE_ROWS = 256
MOE_AHEAD = 2
ROW_TILE = D_MODEL // LANE
GDN_CHUNK = 128
VMEM_LIMIT = 56 * 1024 * 1024


def _mod_index(t):
    return jnp.where(t < PROMPT_TILES, 0, 1 + (t - PROMPT_TILES) // TILES_PER_SAMPLE)


def _sigmoid(x):
    return 1.0 / (1.0 + jnp.exp(-x))


def _silu(x):
    return x * _sigmoid(x)


def _rms(x, w):
    return x * lax.rsqrt(jnp.mean(x * x, axis=-1, keepdims=True) + EPS) * w


def _dot(a, b):
    return jnp.dot(a.astype(BF16), b.astype(BF16), preferred_element_type=F32)


def _dot_nt(a, b):
    return lax.dot_general(a.astype(BF16), b.astype(BF16), (((1,), (1,)), ((), ())),
                           preferred_element_type=F32)


def _dot_tn(a, b):
    return lax.dot_general(a.astype(BF16), b.astype(BF16), (((0,), (0,)), ((), ())),
                           preferred_element_type=F32)


def _dot_f32(a, b):
    return jnp.dot(a, b, precision=HIGHEST, preferred_element_type=F32)


def _tri_matmul(tri, x):
    hi = x.astype(BF16)
    rem = x - hi.astype(F32)
    mid = rem.astype(BF16)
    lo = (rem - mid.astype(F32)).astype(BF16)
    n = x.shape[1]
    r = jnp.dot(tri.astype(BF16), jnp.concatenate([hi, mid, lo], axis=1), preferred_element_type=F32)
    return r[:, :n] + r[:, n:2 * n] + r[:, 2 * n:]


def _params(sem):
    return pltpu.CompilerParams(dimension_semantics=sem, vmem_limit_bytes=VMEM_LIMIT)


def _adaln_kernel(c_ref, w_ref, b_ref, o_ref):
    s = _silu(c_ref[...])
    o_ref[0] = _dot(s, w_ref[0]) + b_ref[0]


def _adaln(cvec, ada_w, ada_b):
    nb = 4
    wb = 6 * D_MODEL // nb
    return pl.pallas_call(
        _adaln_kernel,
        out_shape=jax.ShapeDtypeStruct((DEPTH, MOD_ROWS, 6 * D_MODEL), F32),
        grid=(DEPTH, nb),
        in_specs=[pl.BlockSpec((MOD_ROWS, D_MODEL), lambda l, j: (0, 0)),
                  pl.BlockSpec((1, D_MODEL, wb), lambda l, j: (l, 0, j)),
                  pl.BlockSpec((1, 1, wb), lambda l, j: (l, 0, j))],
        out_specs=pl.BlockSpec((1, MOD_ROWS, wb), lambda l, j: (l, 0, j)),
        compiler_params=_params(("parallel", "parallel")),
        name="adaln",
    )(cvec, ada_w, ada_b.reshape(DEPTH, 1, 6 * D_MODEL))


def _row_specs(x, width):
    if isinstance(x, tuple):
        return ([pl.BlockSpec((TM, width), lambda t: (jnp.minimum(t, PROMPT_TILES - 1), 0)),
                 pl.BlockSpec((TM, width), lambda t: (jnp.maximum(t - PROMPT_TILES, 0), 0))], list(x))
    return [pl.BlockSpec((TM, width), lambda t: (t, 0))], [x]


def _read_rows(refs):
    if len(refs) == 2:
        return jnp.where(pl.program_id(0) < PROMPT_TILES, refs[0][...], refs[1][...])
    return refs[0][...]


def _inproj_kernel(*refs, n_x, rope_cols):
    x = _read_rows(refs[:n_x])
    if rope_cols is None:
        mod_ref, nw_ref, w_ref, o_ref = refs[n_x:]
    else:
        mod_ref, nw_ref, w_ref, cos_ref, sin_ref, o_ref = refs[n_x:]
    h = _rms(x, nw_ref[...]) * (1.0 + mod_ref[0, 1:2, :]) + mod_ref[0, 0:1, :]
    r = jnp.dot(h.astype(BF16), w_ref[...], preferred_element_type=F32)
    if rope_cols is None:
        o_ref[...] = r
        return
    lo, hi = rope_cols
    o_ref[:, :lo] = r[:, :lo]
    o_ref[:, hi:] = r[:, hi:]
    cos = cos_ref[...]
    sin = sin_ref[...]
    lane = lax.broadcasted_iota(jnp.int32, (TM, LANE), 1)
    upper = (lane & 16) != 0
    for c0 in range(lo, hi, LANE):
        v = r[:, c0:c0 + LANE]
        partner = jnp.where(upper, pltpu.roll(v, 16, 1), pltpu.roll(v, LANE - 16, 1))
        o_ref[:, c0:c0 + LANE] = v * cos + partner * sin


def _inproj(x, mod, norm_w, w_bf16, rope=None):
    p = w_bf16.shape[1]
    in_specs, args = _row_specs(x, D_MODEL)
    n_x = len(args)
    in_specs += [pl.BlockSpec((1, MOD_ROWS, D_MODEL), lambda t: (_mod_index(t), 0, 0)),
                 pl.BlockSpec((1, D_MODEL), lambda t: (0, 0)),
                 pl.BlockSpec((D_MODEL, p), lambda t: (0, 0))]
    args += [mod, norm_w.reshape(1, D_MODEL), w_bf16]
    rope_cols = None
    if rope is not None:
        cos, sin, rope_cols = rope

        def rope_index(t):
            return (jnp.where(t < PROMPT_TILES, 0, 1 + (t - PROMPT_TILES) % TILES_PER_SAMPLE), 0)

        in_specs += [pl.BlockSpec((TM, LANE), rope_index), pl.BlockSpec((TM, LANE), rope_index)]
        args += [cos, sin]
    return pl.pallas_call(
        functools.partial(_inproj_kernel, n_x=n_x, rope_cols=rope_cols),
        out_shape=jax.ShapeDtypeStruct((N_TOK, p), F32),
        grid=(N_TILES,),
        in_specs=in_specs,
        out_specs=pl.BlockSpec((TM, p), lambda t: (t, 0)),
        compiler_params=_params(("parallel",)),
        name="inproj",
    )(*args)


def _rope_tables():
    lane = jnp.arange(LANE)
    d = lane % DH_B
    use_col = (d // 32) == 1
    j = d % 16
    upper = ((d % 32) // 16) == 1
    inv_freq = ROPE_BASE ** (-j.astype(F32) / 16.0)
    t = jnp.arange(DEC_SEQ)
    row = (t // GRID_W).astype(F32)
    col = (t % GRID_W).astype(F32)
    pos = jnp.where(use_col[None, :], col[:, None], row[:, None])
    ang = pos * inv_freq[None, :]
    cos = jnp.cos(ang)
    sin = jnp.where(upper[None, :], jnp.sin(ang), -jnp.sin(ang))
    cos = jnp.concatenate([jnp.ones((TM, LANE), F32), cos], axis=0)
    sin = jnp.concatenate([jnp.zeros((TM, LANE), F32), sin], axis=0)
    return cos, sin


def _route(logits):
    lane = lax.broadcasted_iota(jnp.int32, logits.shape, 1)
    lanef = lane.astype(F32)
    neg = jnp.float32(-jnp.inf)
    gl = jnp.where(lane < N_GROUPS, logits, neg)
    gmax = jnp.max(gl, axis=1, keepdims=True)
    gsel = jnp.min(jnp.where(gl == gmax, lanef, float(LANE)), axis=1, keepdims=True)
    p_grp = 1.0 / jnp.sum(jnp.exp(gl - gmax), axis=1, keepdims=True)
    lo = float(N_GROUPS) + gsel * float(E_PER_GROUP)
    el = jnp.where((lanef >= lo) & (lanef < lo + float(E_PER_GROUP)), logits, neg)
    v1 = jnp.max(el, axis=1, keepdims=True)
    i1 = jnp.min(jnp.where(el == v1, lanef, float(LANE)), axis=1, keepdims=True)
    el2 = jnp.where(lanef == i1, neg, el)
    v2 = jnp.max(el2, axis=1, keepdims=True)
    i2 = jnp.min(jnp.where(el2 == v2, lanef, float(LANE)), axis=1, keepdims=True)
    t = jnp.exp(v2 - v1)
    w1 = p_grp / (1.0 + t)
    w2 = p_grp * t / (1.0 + t)
    out = jnp.where(lane == 0, i1 - float(N_GROUPS), 0.0)
    out = jnp.where(lane == 1, i2 - float(N_GROUPS), out)
    out = jnp.where(lane == 2, w1, out)
    out = jnp.where(lane == 3, w2, out)
    return out


def _outproj_kernel(*refs, widths, n_x):
    n_in = len(widths)
    x_refs = refs[2 * n_in:2 * n_in + n_x]
    w_ref, mod_ref, nw_ref, wr_ref, xn_ref, h2_ref, rt_ref = refs[2 * n_in + n_x:]
    y = None
    c0 = 0
    for k, wd in enumerate(widths):
        o = _read_rows(refs[2 * k:2 * k + 2])
        part = jnp.dot(o.astype(BF16), w_ref[c0:c0 + wd, :], preferred_element_type=F32)
        y = part if y is None else y + part
        c0 += wd
    xn = _read_rows(x_refs) + mod_ref[0, 2:3, :] * y
    xn_ref[...] = xn
    h2 = _rms(xn, nw_ref[...]) * (1.0 + mod_ref[0, 4:5, :]) + mod_ref[0, 3:4, :]
    for s in range(ROW_TILE):
        h2_ref[pl.ds(s, TM, stride=ROW_TILE), :] = h2[:, s * LANE:(s + 1) * LANE]
    h_hi = h2.astype(BF16)
    h_lo = (h2 - h_hi.astype(F32)).astype(BF16)
    wr = wr_ref[...]
    hw = jnp.dot(h_hi, wr, preferred_element_type=F32)
    logits = hw[:, :LANE] + hw[:, LANE:] + jnp.dot(h_lo, wr[:, :LANE], preferred_element_type=F32)
    rt_ref[...] = _route(logits)


def _outproj(parts, w_bf16, x, mod, norm_w2, w_router):
    widths = tuple(p[0].shape[1] for p in parts)
    in_specs, args = [], []
    for p, wd in zip(parts, widths):
        specs, ops = _row_specs(p, wd)
        in_specs += specs
        args += ops
    specs, ops = _row_specs(x, D_MODEL)
    in_specs += specs
    args += ops
    in_specs += [pl.BlockSpec((D_MODEL, D_MODEL), lambda t: (0, 0)),
                 pl.BlockSpec((1, MOD_ROWS, D_MODEL), lambda t: (_mod_index(t), 0, 0)),
                 pl.BlockSpec((1, D_MODEL), lambda t: (0, 0)),
                 pl.BlockSpec((D_MODEL, 2 * LANE), lambda t: (0, 0))]
    args += [w_bf16, mod, norm_w2.reshape(1, D_MODEL), w_router]
    return pl.pallas_call(
        functools.partial(_outproj_kernel, widths=widths, n_x=len(ops)),
        out_shape=(jax.ShapeDtypeStruct((N_TOK, D_MODEL), F32),
                   jax.ShapeDtypeStruct((N_TOK * ROW_TILE, LANE), F32),
                   jax.ShapeDtypeStruct((N_TOK, LANE), F32)),
        grid=(N_TILES,),
        in_specs=in_specs,
        out_specs=(pl.BlockSpec((TM, D_MODEL), lambda t: (t, 0)),
                   pl.BlockSpec((TM * ROW_TILE, LANE), lambda t: (t, 0)),
                   pl.BlockSpec((TM, LANE), lambda t: (t, 0))),
        compiler_params=_params(("parallel",)),
        name="outproj",
    )(*args)


def _expert_kernel(blk_e_ref, nact_ref, rowtok_ref, h2_hbm, wg_ref, wu_ref, wd_ref, y_ref,
                   x_0, x_1, x_2, sems, wg_s, wu_s, wd_s):
    i = pl.program_id(0)
    n_act = nact_ref[0]
    e = blk_e_ref[i]
    prev = blk_e_ref[jnp.maximum(i - 1, 0)]
    bufs = (x_0, x_1, x_2)
    ring = MOE_AHEAD + 1
    assert ring == len(bufs)

    def row_copy(blk, r, slot):
        tok = rowtok_ref[blk * MOE_ROWS + r]
        return pltpu.make_async_copy(
            h2_hbm.at[pl.ds(pl.multiple_of(tok * ROW_TILE, ROW_TILE), ROW_TILE), :],
            bufs[slot].at[pl.ds(pl.multiple_of(r * ROW_TILE, ROW_TILE), ROW_TILE), :],
            sems.at[slot])

    def wait_block(slot):
        pltpu.make_async_copy(h2_hbm.at[pl.ds(0, MOE_ROWS * ROW_TILE), :], bufs[slot], sems.at[slot]).wait()

    @pl.when(i == 0)
    def _():
        for blk in range(MOE_AHEAD):
            def start_row(r, carry):
                row_copy(blk, r, blk).start()
                return carry
            lax.fori_loop(0, MOE_ROWS, start_row, 0)

    @pl.when((i == 0) | (e != prev))
    def _():
        wg_s[...] = wg_ref[...].astype(BF16)
        wu_s[...] = wu_ref[...].astype(BF16)
        wd_s[...] = wd_ref[...].astype(BF16)

    for slot in range(ring):
        @pl.when((i >= n_act) & (i < n_act + MOE_AHEAD) & (i % ring == slot))
        def _():
            wait_block(slot)

        @pl.when((i < n_act) & (i % ring == slot))
        def _():
            wait_block(slot)
            for r in range(MOE_ROWS):
                row_copy(i + MOE_AHEAD, r, (slot + MOE_AHEAD) % ring).start(priority=r % 2)
            x = jnp.concatenate([bufs[slot][pl.ds(s, MOE_ROWS, stride=ROW_TILE), :] for s in range(ROW_TILE)],
                                axis=1).astype(BF16)
            g = jnp.dot(x, wg_s[...], preferred_element_type=F32)
            u = jnp.dot(x, wu_s[...], preferred_element_type=F32)
            y_ref[...] = jnp.dot((_silu(g) * u).astype(BF16), wd_s[...], preferred_element_type=F32)

    @pl.when(i >= nact_ref[0])
    def _():
        y_ref[...] = jnp.zeros_like(y_ref)


def _experts(h2_tiles, row_tok, blk_e, n_active, w_gate, w_up, w_down, layer):
    n_rows = row_tok.shape[0]
    n_blocks = n_rows // MOE_ROWS
    grid_spec = pltpu.PrefetchScalarGridSpec(
        num_scalar_prefetch=3,
        grid=(n_blocks,),
        in_specs=[pl.BlockSpec(memory_space=pl.ANY),
                  pl.BlockSpec((None, None, D_MODEL, D_EXPERT), lambda i, be, na, rt: (layer, be[i], 0, 0)),
                  pl.BlockSpec((None, None, D_MODEL, D_EXPERT), lambda i, be, na, rt: (layer, be[i], 0, 0)),
                  pl.BlockSpec((None, None, D_EXPERT, D_MODEL), lambda i, be, na, rt: (layer, be[i], 0, 0))],
        out_specs=pl.BlockSpec((MOE_ROWS, D_MODEL), lambda i, be, na, rt: (i, 0)),
        scratch_shapes=[pltpu.VMEM((MOE_ROWS * ROW_TILE, LANE), F32),
                        pltpu.VMEM((MOE_ROWS * ROW_TILE, LANE), F32),
                        pltpu.VMEM((MOE_ROWS * ROW_TILE, LANE), F32),
                        pltpu.SemaphoreType.DMA((MOE_AHEAD + 1,)),
                        pltpu.VMEM((D_MODEL, D_EXPERT), BF16),
                        pltpu.VMEM((D_MODEL, D_EXPERT), BF16),
                        pltpu.VMEM((D_EXPERT, D_MODEL), BF16)])
    return pl.pallas_call(
        _expert_kernel,
        out_shape=jax.ShapeDtypeStruct((n_rows, D_MODEL), F32),
        grid_spec=grid_spec,
        compiler_params=_params(("arbitrary",)),
        name="experts",
    )(blk_e, n_active, row_tok, h2_tiles, w_gate, w_up, w_down)


def _combine_kernel(x_ref, y0_ref, y1_ref, rt_ref, mod_ref, fw_ref, o_ref, *, final):
    rt = rt_ref[...]
    y = rt[:, 2:3] * y0_ref[...] + rt[:, 3:4] * y1_ref[...]
    x = x_ref[...] + mod_ref[0, 5:6, :] * y
    o_ref[...] = _rms(x, fw_ref[...]) if final else x


def _combine(x, y0, y1, route, mod, final_w, final=False, tile0=0, n_rows=N_TOK):
    row = pl.BlockSpec((TM, D_MODEL), lambda t: (t + tile0, 0))
    return pl.pallas_call(
        functools.partial(_combine_kernel, final=final),
        out_shape=jax.ShapeDtypeStruct((n_rows, D_MODEL), F32),
        grid=(n_rows // TM,),
        in_specs=[row, row, row,
                  pl.BlockSpec((TM, LANE), lambda t: (t + tile0, 0)),
                  pl.BlockSpec((1, MOD_ROWS, D_MODEL), lambda t: (_mod_index(t + tile0), 0, 0)),
                  pl.BlockSpec((1, D_MODEL), lambda t: (0, 0))],
        out_specs=pl.BlockSpec((TM, D_MODEL), lambda t: (t, 0)),
        compiler_params=_params(("parallel",)),
        name="combine",
    )(x, y0, y1, route, mod, final_w.reshape(1, D_MODEL))


def _take_rows(a, idx):
    return a.at[idx].get(mode="promise_in_bounds")


def _moe(x, h2, route, mod, w_gate, w_up, w_down, layer, final_w):
    n_asg = 2 * N_TOK
    flat_e = route[:, :2].astype(jnp.int32).reshape(n_asg)
    flat_tok = jnp.arange(n_asg, dtype=jnp.int32) // 2
    onehot = (flat_e[:, None] == jnp.arange(N_EXPERTS, dtype=jnp.int32)[None, :]).astype(jnp.int32)
    csum = jnp.cumsum(onehot, axis=0)
    rank = jnp.take_along_axis(csum, flat_e[:, None], axis=1)[:, 0] - 1
    counts = csum[-1]
    padded = (counts + MOE_ROWS - 1) // MOE_ROWS * MOE_ROWS
    pad_end = jnp.cumsum(padded)
    pad_start = pad_end - padded
    dest = pad_start[flat_e] + rank
    n_rows = n_asg + (N_EXPERTS + MOE_AHEAD) * MOE_ROWS
    n_blocks = n_rows // MOE_ROWS
    row_tok = (jnp.arange(n_rows, dtype=jnp.int32) % N_TOK).at[dest].set(
        flat_tok, unique_indices=True, mode="promise_in_bounds")
    blk_start = jnp.arange(n_blocks, dtype=jnp.int32) * MOE_ROWS
    blk_e = jnp.sum((blk_start[:, None] >= pad_end[None, :]).astype(jnp.int32), axis=1)
    blk_e = jnp.minimum(blk_e, N_EXPERTS - 1)
    n_active = (pad_end[-1:] // MOE_ROWS).astype(jnp.int32)
    yb = _experts(h2, row_tok, blk_e, n_active, w_gate, w_up, w_down, layer)
    dest2 = dest.reshape(N_TOK, 2)
    y0 = _take_rows(yb, dest2[:, 0])
    y1 = _take_rows(yb, dest2[:, 1])
    if layer < DEPTH - 1:
        return _combine(x, y0, y1, route, mod, final_w)
    return (_combine(x, y0, y1, route, mod, final_w, final=True, n_rows=N_PROMPT),
            _combine(x, y0, y1, route, mod, final_w, final=True, tile0=PROMPT_TILES, n_rows=N_SAMPLE))


def _hgrn_kernel(*refs, seq_len, layer, context):
    if context:
        (q_ref, ff_ref, fb_ref, v_ref, g_ref, lbl_ref, nw_ref,
         o_ref, sfin_ref, of_s, ob_s, sf_s, sb_s) = refs
    else:
        (q_ref, ff_ref, fb_ref, v_ref, g_ref, lbl_ref, nw_ref, s0_ref,
         o_ref, of_s, ob_s, sf_s, sb_s) = refs
    c = CHUNK_A
    n = seq_len // c

    def lower_bound(d):
        z = lbl_ref[d]
        e = jnp.exp(z - jnp.max(z, axis=0, keepdims=True))
        return jnp.sum(e[:layer + 1], axis=0, keepdims=True) / jnp.sum(e, axis=0, keepdims=True)

    lb_f = lower_bound(0)
    lb_b = lower_bound(1)
    row = lax.broadcasted_iota(jnp.int32, (c, c), 0)
    col = lax.broadcasted_iota(jnp.int32, (c, c), 1)
    causal = row >= col
    tri_f = causal.astype(F32)
    tri_b = (row <= col).astype(F32)

    if context:
        sf_s[...] = jnp.zeros_like(sf_s)
        sb_s[...] = jnp.zeros_like(sb_s)
    else:
        sf_s[...] = s0_ref[0, 0, 0].T
        sb_s[...] = s0_ref[0, 1, 0].T

    group = 8
    dirs = ((ff_ref, lb_f, tri_f, causal, c - 1, sf_s, of_s),
            (fb_ref, lb_b, tri_b, row <= col, 0, sb_s, ob_s))

    def body(i, carry):
        items = []
        for g in range(group):
            j = i * group + g
            items.append((0, pl.ds(pl.multiple_of(j * c, c), c)))
            items.append((1, pl.ds(pl.multiple_of((n - 1 - j) * c, c), c)))
        fs = [dirs[d][1] + (1.0 - dirs[d][1]) * _sigmoid(dirs[d][0][rows, :]) for d, rows in items]
        bs = [_tri_matmul(dirs[d][2], jnp.log(f)) for (d, _), f in zip(items, fs)]
        b_lasts = [b[dirs[d][4]:dirs[d][4] + 1, :] for (d, _), b in zip(items, bs)]
        q_ins = [_silu(q_ref[rows, :]) * jnp.exp(b) for (_, rows), b in zip(items, bs)]
        a_s = [jnp.where(dirs[d][3], _dot_nt(q_in, (1.0 - f) * jnp.exp(-b)), 0.0)
               for (d, _), q_in, f, b in zip(items, q_ins, fs, bs)]
        vs = [v_ref[rows, :] for _, rows in items]
        o_intra = [_dot(a, v) for a, v in zip(a_s, vs)]
        u_ts = [_dot_tn(v, (1.0 - f) * jnp.exp(b_last - b)) for v, f, b, b_last in zip(vs, fs, bs, b_lasts)]
        for d in (0, 1):
            st_ref, out_s = dirs[d][5], dirs[d][6]
            st = st_ref[...]
            for k, (dk, rows) in enumerate(items):
                if dk == d:
                    out_s[rows, :] = o_intra[k] + _dot_nt(q_ins[k], st)
                    st = st * jnp.exp(b_lasts[k]) + u_ts[k]
            st_ref[...] = st
        return carry

    lax.fori_loop(0, n // group, body, 0)

    if context:
        sfin_ref[0, 0, 0] = sf_s[...].T
        sfin_ref[0, 1, 0] = sb_s[...].T

    nw = nw_ref[...]

    def epilogue(j, carry):
        rows = pl.ds(pl.multiple_of(j * TM, TM), TM)
        o = of_s[rows, :] + ob_s[rows, :]
        o_ref[rows, :] = _rms(o, nw) * _silu(g_ref[rows, :])
        return carry

    lax.fori_loop(0, seq_len // TM, epilogue, 0)


def _hgrn(proj, lb_logits, norm_w, layer, context, s0=None):
    seq_len = SEQ if context else DEC_SEQ
    bsz = BATCH if context else DEC_BATCH
    row0 = 0 if context else N_PROMPT // seq_len

    def col(k):
        return pl.BlockSpec((seq_len, LANE), lambda b, h: (row0 + b, k * H_A + h))

    in_specs = [col(0), col(1), col(2), col(3), col(4),
                pl.BlockSpec((2, DEPTH + 1, LANE), lambda b, h: (0, 0, h)),
                pl.BlockSpec((1, LANE), lambda b, h: (0, 0))]
    args = [proj, proj, proj, proj, proj, lb_logits, norm_w.reshape(1, LANE)]
    o_shape = jax.ShapeDtypeStruct((bsz * seq_len, W_A), F32)
    o_spec = pl.BlockSpec((seq_len, LANE), lambda b, h: (b, h))
    scratch = [pltpu.VMEM((seq_len, LANE), F32), pltpu.VMEM((seq_len, LANE), F32),
               pltpu.VMEM((LANE, LANE), F32), pltpu.VMEM((LANE, LANE), F32)]
    kern = functools.partial(_hgrn_kernel, seq_len=seq_len, layer=layer, context=context)
    if context:
        return pl.pallas_call(
            kern,
            out_shape=(o_shape, jax.ShapeDtypeStruct((bsz, 2, H_A, DK_A, DK_A), F32)),
            grid=(bsz, H_A),
            in_specs=in_specs,
            out_specs=(o_spec, pl.BlockSpec((1, 2, 1, DK_A, DK_A), lambda b, h: (b, 0, h, 0, 0))),
            scratch_shapes=scratch,
            compiler_params=_params(("parallel", "parallel")),
            name="hgrn_ctx",
        )(*args)
    in_specs += [pl.BlockSpec((1, 2, 1, DK_A, DK_A), lambda b, h: (b, 0, h, 0, 0))]
    args += [s0]
    return pl.pallas_call(
        kern,
        out_shape=o_shape,
        grid=(bsz, H_A),
        in_specs=in_specs,
        out_specs=o_spec,
        scratch_shapes=scratch,
        compiler_params=_params(("parallel", "parallel")),
        name="hgrn_dec",
    )(*args)


ATT_TQ = 256
COL_QD = 5 * W_A // LANE
COL_KD = COL_QD + QK_B // LANE
COL_VD = COL_KD + QK_B // LANE


def _attn_kernel(*refs, seq_len, layer, context):
    if context:
        q_ref, k_ref, v_ref, lam_ref, nw_ref, o_ref, k_s, v_s = refs
    else:
        q_ref, k_ref, v_ref, ck_ref, cv_ref, lam_ref, nw_ref, o_ref, k_s, v_s = refs

    @pl.when(pl.program_id(2) == 0)
    def _():
        k_s[0:seq_len, :] = k_ref[...].astype(BF16)
        v_s[0:seq_len, :] = v_ref[...].astype(BF16)
        if not context:
            k_s[seq_len:, :] = ck_ref[0].astype(BF16)
            v_s[seq_len:, :] = cv_ref[0].astype(BF16)

    lam_init = 0.8 - 0.6 * math.exp(-0.3 * layer)
    lp = lam_ref[...]
    lam = (jnp.exp(jnp.sum(lp[0:1] * lp[1:2], axis=1, keepdims=True))
           - jnp.exp(jnp.sum(lp[2:3] * lp[3:4], axis=1, keepdims=True)) + lam_init)

    q = q_ref[...] * (DH_B ** -0.5 * math.log2(math.e))
    lane = lax.broadcasted_iota(jnp.int32, q.shape, 1)
    k = k_s[...]

    def softmax_map(first):
        s = _dot_nt(jnp.where((lane < DH_B) == first, q, 0.0), k)
        p = jnp.exp2(s - jnp.max(s, axis=1, keepdims=True))
        return p, jnp.sum(p, axis=1, keepdims=True)

    p0, l0 = softmax_map(True)
    p1, l1 = softmax_map(False)
    a = p0 - (lam * l0 / l1) * p1
    o = jnp.dot(a.astype(BF16), v_s[...], preferred_element_type=F32) / l0
    o_ref[...] = _rms(o, nw_ref[...]) * (1.0 - lam_init)


def _attn(proj, lam_p, norm_w, layer, context, cache_k=None, cache_v=None):
    seq_len = SEQ if context else DEC_SEQ
    bsz = BATCH if context else DEC_BATCH
    row0 = 0 if context else N_PROMPT // seq_len
    nq = seq_len // ATT_TQ
    tile0 = row0 * nq
    t_k = seq_len if context else seq_len + PAST_LEN
    in_specs = [pl.BlockSpec((ATT_TQ, LANE), lambda b, h, i: (tile0 + b * nq + i, COL_QD + h)),
                pl.BlockSpec((seq_len, LANE), lambda b, h, i: (row0 + b, COL_KD + h)),
                pl.BlockSpec((seq_len, LANE), lambda b, h, i: (row0 + b, COL_VD + h))]
    args = [proj, proj, proj]
    if not context:
        in_specs += [pl.BlockSpec((1, PAST_LEN, LANE), lambda b, h, i: (b, 0, h)),
                     pl.BlockSpec((1, PAST_LEN, LANE), lambda b, h, i: (b, 0, h))]
        args += [cache_k, cache_v]
    in_specs += [pl.BlockSpec((4, DH_B), lambda b, h, i: (0, 0)),
                 pl.BlockSpec((1, LANE), lambda b, h, i: (0, 0))]
    args += [lam_p, norm_w.reshape(1, LANE)]
    return pl.pallas_call(
        functools.partial(_attn_kernel, seq_len=seq_len, layer=layer, context=context),
        out_shape=jax.ShapeDtypeStruct((bsz * seq_len, W_B), F32),
        grid=(bsz, H_B, nq),
        in_specs=in_specs,
        out_specs=pl.BlockSpec((ATT_TQ, LANE), lambda b, h, i: (b * nq + i, h)),
        scratch_shapes=[pltpu.VMEM((t_k, LANE), BF16), pltpu.VMEM((t_k, LANE), BF16)],
        compiler_params=_params(("parallel", "parallel", "arbitrary")),
        name="attn_ctx" if context else "attn_dec",
    )(*args)


COL_GATES = 4 * W_C // LANE
GATE_CHUNKS = 4
GDN_CTX_SEQS = 4


def _gdn_gates_kernel(g_ref, alog_ref, dtb_ref, col_ref, row_ref):
    c = GDN_CHUNK
    row = lax.broadcasted_iota(jnp.int32, (c, c), 0)
    col = lax.broadcasted_iota(jnp.int32, (c, c), 1)
    lane = lax.broadcasted_iota(jnp.int32, (c, LANE), 1)
    lower = (row >= col).astype(F32)
    upper = (row <= col).astype(F32)
    for j in range(GATE_CHUNKS):
        raw = g_ref[j * c:(j + 1) * c, :]
        z = raw + dtb_ref[...]
        softplus = jnp.maximum(z, 0.0) + jnp.log(1.0 + jnp.exp(-jnp.abs(z)))
        g = -jnp.exp(alog_ref[...]) * softplus
        out = jnp.where(lane < H_C, _tri_matmul(lower, g),
                        jnp.where(lane < 2 * H_C, _tri_matmul(upper, g), _sigmoid(raw)))
        col_ref[j * c:(j + 1) * c, :] = out
        row_ref[j] = out.T


def _gdn_gates(proj, alog_lane, dtb_lane):
    c = GDN_CHUNK
    n = N_TOK // c
    return pl.pallas_call(
        _gdn_gates_kernel,
        out_shape=(jax.ShapeDtypeStruct((N_TOK, LANE), F32), jax.ShapeDtypeStruct((n, LANE, c), F32)),
        grid=(n // GATE_CHUNKS,),
        in_specs=[pl.BlockSpec((GATE_CHUNKS * c, LANE), lambda i: (i, COL_GATES)),
                  pl.BlockSpec((1, LANE), lambda i: (0, 0)),
                  pl.BlockSpec((1, LANE), lambda i: (0, 0))],
        out_specs=(pl.BlockSpec((GATE_CHUNKS * c, LANE), lambda i: (i, 0)),
                   pl.BlockSpec((GATE_CHUNKS, LANE, c), lambda i: (i, 0, 0))),
        compiler_params=_params(("parallel",)),
        name="gdn_gates",
    )(proj, alog_lane, dtb_lane)


def _unit_tri_inverse_pairs(ms, row, col):
    c = ms[0].shape[0]
    zero = jnp.zeros((c, c), BF16)

    def mm(xs, ys):
        out = []
        for x, y in zip(xs, ys):
            yb = y.astype(BF16)
            blockdiag = jnp.concatenate([jnp.concatenate([yb[:, :c], zero], axis=1),
                                         jnp.concatenate([zero, yb[:, c:]], axis=1)], axis=0)
            out.append(jnp.dot(x.astype(BF16), blockdiag, preferred_element_type=F32))
        return out

    def add(xs, ys):
        return [x + y for x, y in zip(xs, ys)]

    eye = (row == col).astype(F32)
    a = [jnp.where((row // 16) == (col // 16), m, 0.0) for m in ms]
    a2 = mm(a, a)
    a4 = mm(a2, a2)
    a8 = mm(a4, a4)
    t = [eye - x for x in a]
    t = add(t, mm(t, a2))
    t = add(t, mm(t, a4))
    t = add(t, mm(t, a8))
    blk = 32
    while blk <= c:
        off = ((row // blk) == (col // blk)) & ((row // (blk // 2)) != (col // (blk // 2)))
        corr = mm(mm(t, [jnp.where(off, m, 0.0) for m in ms]), t)
        t = [x - y for x, y in zip(t, corr)]
        blk *= 2
    return t


def _gdn_kernel(*refs, seq_len, n_seq, context):
    if context:
        (q_ref, k_ref, v_ref, go_ref, cwq_ref, cwk_ref, cwv_ref, gcol_ref, grow_ref, nw_ref,
         o_ref, sfin_ref, su_s, pr_s, of_s, ob_s, sf_s, sb_s) = refs
    else:
        (q_ref, k_ref, v_ref, go_ref, cwq_ref, cwk_ref, cwv_ref, gcol_ref, grow_ref, nw_ref, s0_ref,
         o_ref, su_s, pr_s, of_s, ob_s, sf_s, sb_s) = refs
    c = GDN_CHUNK
    cps = seq_len // c
    n = n_seq * cps
    head = pl.program_id(1)
    row = lax.broadcasted_iota(jnp.int32, (c, c), 0)
    col = lax.broadcasted_iota(jnp.int32, (c, c), 1)
    row2 = lax.broadcasted_iota(jnp.int32, (c, 2 * c), 0)
    col2 = lax.broadcasted_iota(jnp.int32, (c, 2 * c), 1) & (c - 1)
    lane = lax.broadcasted_iota(jnp.int32, (c, LANE), 1)
    rowi = lax.broadcasted_iota(jnp.int32, (c, LANE), 0)

    def gc_row_last(ci, d):
        gc_row = grow_ref[ci, pl.ds(d * H_C + head, 1), :]
        return gc_row, (gc_row[:, c - 1:c] if d == 0 else gc_row[:, 0:1])

    def chunk_inputs(ci):
        r0 = pl.multiple_of(ci * c, c)
        rows = pl.ds(r0, c)

        def conv(x_ref, w_ref):
            cur = x_ref[rows, :]
            before = x_ref[pl.ds(pl.multiple_of(jnp.maximum(r0 - 8, 0), 8), 8), :]
            after = x_ref[pl.ds(pl.multiple_of(jnp.minimum(r0 + c, n * c - 8), 8), 8), :]
            in_seq = lax.rem(ci, cps)
            prev_row = jnp.where(in_seq > 0, before[7:8, :], 0.0)
            next_row = jnp.where(in_seq < cps - 1, after[0:1, :], 0.0)
            xm1 = jnp.where(rowi == 0, prev_row, pltpu.roll(cur, 1, 0))
            xp1 = jnp.where(rowi == c - 1, next_row, pltpu.roll(cur, c - 1, 0))
            w = w_ref[...]
            return _silu(xm1 * w[0:1, :] + cur * w[1:2, :] + xp1 * w[2:3, :])

        q = conv(q_ref, cwq_ref)
        k = conv(k_ref, cwk_ref)
        vn = conv(v_ref, cwv_ref)
        qn = q * lax.rsqrt(jnp.sum(q * q, axis=1, keepdims=True) + EPS) * (DK_C ** -0.5)
        kn = k * lax.rsqrt(jnp.sum(k * k, axis=1, keepdims=True) + EPS)
        kq = _dot_nt(jnp.concatenate([kn, qn], axis=0), kn)
        gates = gcol_ref[rows, :]
        per_dir = []
        ms = []
        for d in (0, 1):
            gc = jnp.sum(jnp.where(lane == d * H_C + head, gates, 0.0), axis=1, keepdims=True)
            beta = jnp.sum(jnp.where(lane == (2 + d) * H_C + head, gates, 0.0), axis=1, keepdims=True)
            gc_row, gc_last = gc_row_last(ci, d)
            incl = (row >= col) if d == 0 else (row <= col)
            strict = (row > col) if d == 0 else (row < col)
            decay = jnp.where(incl, jnp.exp(gc - gc_row), 0.0)
            m = jnp.where(strict, kq[:c] * beta * decay, 0.0)
            per_dir.append((gc, beta, kq[c:] * decay, (kn * jnp.exp(gc_last - gc)).T))
            ms.append(m)
        return jnp.concatenate(ms, axis=1), (rows, qn, kn, vn, per_dir)

    def chunk_outputs(ci, t, rest):
        rows, qn, kn, vn, per_dir = rest
        base = pl.multiple_of(ci * 2 * c, 2 * c)
        for d in (0, 1):
            gc, beta, qk, kd_t = per_dir[d]
            e = jnp.exp(gc)
            uw = _dot(t[:, d * c:(d + 1) * c], jnp.concatenate([vn * beta, kn * (beta * e)], axis=1))
            cross = _dot(jnp.concatenate([kd_t, qk], axis=0), uw)
            su_s[d, rows, :] = cross[:c, :c]
            (of_s, ob_s)[d][rows, :] = cross[c:, :c]
            pr_s[d, pl.ds(base, c), :] = cross[:c, c:].astype(BF16)
            pr_s[d, pl.ds(base + c, c), :] = (qn * e - cross[c:, c:]).astype(BF16)

    group = min(8, n)

    def prepare(j, carry):
        cis = [j * group + g for g in range(group)]
        staged = [chunk_inputs(ci) for ci in cis]
        ts = _unit_tri_inverse_pairs([s[0] for s in staged], row2, col2)
        for ci, t, s in zip(cis, ts, staged):
            chunk_outputs(ci, t, s[1])
        return carry

    lax.fori_loop(0, n // group, prepare, 0)

    def advance(ci, d, s_ref, out_s):
        rows = pl.ds(pl.multiple_of(ci * c, c), c)
        _, gc_last = gc_row_last(ci, d)
        s = s_ref[...]
        ps = jnp.dot(pr_s[d, pl.ds(pl.multiple_of(ci * 2 * c, 2 * c), 2 * c), :], s.astype(BF16),
                     preferred_element_type=F32)
        out_s[rows, :] = out_s[rows, :] + ps[c:]
        s_ref[...] = s * jnp.exp(gc_last) - ps[:c] + su_s[d, rows, :]

    for s in range(n_seq):
        if context:
            sf_s[...] = jnp.zeros_like(sf_s)
            sb_s[...] = jnp.zeros_like(sb_s)
        else:
            sf_s[...] = s0_ref[s, 0, 0]
            sb_s[...] = s0_ref[s, 1, 0]

        def body(i, carry, first=s * cps):
            advance(first + i, 0, sf_s, of_s)
            advance(first + cps - 1 - i, 1, sb_s, ob_s)
            return carry

        lax.fori_loop(0, cps, body, 0)

        if context:
            sfin_ref[s, 0, 0] = sf_s[...]
            sfin_ref[s, 1, 0] = sb_s[...]

    nw = nw_ref[...]

    def epilogue(j, carry):
        rows = pl.ds(pl.multiple_of(j * TM, TM), TM)
        o = of_s[rows, :] + ob_s[rows, :]
        o_ref[rows, :] = _rms(o, nw) * _silu(go_ref[rows, :])
        return carry

    lax.fori_loop(0, n * c // TM, epilogue, 0)


def _gdn(proj, conv_w, gcol, grow, norm_w, context, s0=None):
    seq_len = SEQ if context else DEC_SEQ
    n_seq = GDN_CTX_SEQS if context else 1
    n_blk = (BATCH if context else DEC_BATCH) // n_seq
    rows = n_seq * seq_len
    row0 = 0 if context else N_PROMPT // rows
    nc = rows // GDN_CHUNK

    def col(k):
        return pl.BlockSpec((rows, LANE), lambda b, h: (row0 + b, k * H_C + h))

    def cw(k):
        return pl.BlockSpec((3, LANE), lambda b, h: (0, k * H_C + h))

    in_specs = [col(0), col(1), col(2), col(3), cw(0), cw(1), cw(2),
                pl.BlockSpec((rows, LANE), lambda b, h: (row0 + b, 0)),
                pl.BlockSpec((nc, LANE, GDN_CHUNK), lambda b, h: (row0 + b, 0, 0)),
                pl.BlockSpec((1, LANE), lambda b, h: (0, 0))]
    args = [proj, proj, proj, proj, conv_w, conv_w, conv_w, gcol, grow, norm_w.reshape(1, LANE)]
    o_shape = jax.ShapeDtypeStruct((n_blk * rows, W_C), F32)
    o_spec = pl.BlockSpec((rows, LANE), lambda b, h: (b, h))
    seq_buf = pltpu.VMEM((rows, LANE), F32)
    state_buf = pltpu.VMEM((DK_C, DK_C), F32)
    scratch = [pltpu.VMEM((2, rows, LANE), F32), pltpu.VMEM((2, 2 * rows, LANE), BF16),
               seq_buf, seq_buf, state_buf, state_buf]
    kern = functools.partial(_gdn_kernel, seq_len=seq_len, n_seq=n_seq, context=context)
    state_spec = pl.BlockSpec((n_seq, 2, 1, DK_C, DK_C), lambda b, h: (b, 0, h, 0, 0))
    if context:
        return pl.pallas_call(
            kern,
            out_shape=(o_shape, jax.ShapeDtypeStruct((BATCH, 2, H_C, DK_C, DK_C), F32)),
            grid=(n_blk, H_C),
            in_specs=in_specs,
            out_specs=(o_spec, state_spec),
            scratch_shapes=scratch,
            compiler_params=_params(("parallel", "parallel")),
            name="gdn_ctx",
        )(*args)
    in_specs += [state_spec]
    args += [s0]
    return pl.pallas_call(
        kern,
        out_shape=o_shape,
        grid=(n_blk, H_C),
        in_specs=in_specs,
        out_specs=o_spec,
        scratch_shapes=scratch,
        compiler_params=_params(("parallel", "parallel")),
        name="gdn_dec",
    )(*args)


def kernel(x_prompt, x_sample, c, cache_diff_k, cache_diff_v, state_hgrn, state_gdn, c_ctx,
           ada_w, ada_b, norm_w, final_norm_w, w_in_ab, hgrn_lb_logits, hgrn_norm_w,
           diff_lambda, diff_norm_w, w_in_c, gdn_conv_w, gdn_a_log, gdn_dt_bias, gdn_norm_w,
           w_out, moe_router_group, moe_router_expert, moe_w_gate, moe_w_up, moe_w_down):
    x = (x_prompt.reshape(N_PROMPT, D_MODEL), x_sample.reshape(N_SAMPLE, D_MODEL))
    cvec = jnp.concatenate([c_ctx[None, :], c, jnp.zeros((MOD_ROWS - N_MOD, D_MODEL), F32)], axis=0)
    mod_all = _adaln(cvec, ada_w, ada_b)[:, :N_MOD].reshape(DEPTH, N_MOD, 6, D_MODEL)
    mod_all = jnp.pad(mod_all, ((0, 0), (0, 0), (0, MOD_ROWS - 6), (0, 0)))
    cos, sin = _rope_tables()
    new_k, new_v, new_hgrn, new_gdn = [], [], [], []
    for l in range(DEPTH):
        i = l // 2
        mod = mod_all[l]
        if l % 2 == 0:
            proj = _inproj(x, mod, norm_w[l, 0], w_in_ab[i].astype(BF16),
                           rope=(cos, sin, (COL_QD * LANE, COL_VD * LANE)))
            oh_ctx, s_h = _hgrn(proj, hgrn_lb_logits, hgrn_norm_w[i], l, True)
            oh_dec = _hgrn(proj, hgrn_lb_logits, hgrn_norm_w[i], l, False, s0=state_hgrn[:, i])
            od_ctx = _attn(proj, diff_lambda[i], diff_norm_w[i], l, True)
            od_dec = _attn(proj, diff_lambda[i], diff_norm_w[i], l, False,
                           cache_k=cache_diff_k[:, i].reshape(DEC_BATCH, PAST_LEN, QK_B),
                           cache_v=cache_diff_v[:, i].reshape(DEC_BATCH, PAST_LEN, W_B))
            parts = ((oh_ctx, oh_dec), (od_ctx, od_dec))
            new_k.append(proj[:N_PROMPT, COL_KD * LANE:COL_VD * LANE].reshape(BATCH, SEQ, H_B, 2, DH_B))
            new_v.append(proj[:N_PROMPT, COL_VD * LANE:].reshape(BATCH, SEQ, H_B, DV_B))
            new_hgrn.append(s_h)
        else:
            w_c = jnp.pad(w_in_c[i], ((0, 0), (0, (COL_GATES + 1) * LANE - IN_C))).astype(BF16)
            proj = _inproj(x, mod, norm_w[l, 0], w_c)
            pad = jnp.zeros((LANE - 2 * H_C,), F32)
            alog_lane = jnp.concatenate([gdn_a_log[i, 0], gdn_a_log[i, 1], pad]).reshape(1, LANE)
            dtb_lane = jnp.concatenate([gdn_dt_bias[i, 0], gdn_dt_bias[i, 1], pad]).reshape(1, LANE)
            gcol, grow = _gdn_gates(proj, alog_lane, dtb_lane)
            oc_ctx, s_c = _gdn(proj, gdn_conv_w[i], gcol, grow, gdn_norm_w[i], True)
            oc_dec = _gdn(proj, gdn_conv_w[i], gcol, grow, gdn_norm_w[i], False, s0=state_gdn[:, i])
            parts = ((oc_ctx, oc_dec),)
            new_gdn.append(s_c)
        w_router = jnp.concatenate(
            [moe_router_group[l], moe_router_expert[l],
             jnp.zeros((D_MODEL, LANE - N_GROUPS - N_EXPERTS), F32)], axis=1)
        w_router_hi = w_router.astype(BF16)
        w_router = jnp.concatenate([w_router_hi, (w_router - w_router_hi.astype(F32)).astype(BF16)], axis=1)
        x, h2, route = _outproj(parts, w_out[l].astype(BF16), x, mod, norm_w[l, 1], w_router)
        x = _moe(x, h2, route, mod, moe_w_gate, moe_w_up, moe_w_down, l, final_norm_w)
    y_prompt = x[0].reshape(BATCH, SEQ, D_MODEL)
    y_sample = x[1].reshape(DEC_BATCH, DEC_SEQ, D_MODEL)
    return (y_prompt, y_sample, jnp.stack(new_k, axis=1), jnp.stack(new_v, axis=1),
            jnp.stack(new_hgrn, axis=1), jnp.stack(new_gdn, axis=1))
```

```python
import functools
import math

import jax
import jax.numpy as jnp
from jax import lax
from jax.experimental import pallas as pl
from jax.experimental.pallas import tpu as pltpu

F32 = jnp.float32
BF16 = jnp.bfloat16
HIGHEST = lax.Precision.HIGHEST

D_MODEL = 1024
BATCH = 16
SEQ = 256
DEPTH = 2
DEC_BATCH = 4
DEC_SEQ = 4096
PAST_LEN = 256
GRID_W = 64
H_A = 4
DK_A = 128
W_A = 512
CHUNK_A = 32
H_B = 4
DH_B = 64
DV_B = 128
QK_B = 512
W_B = 512
ROPE_BASE = 10000.0
H_C = 8
DK_C = 128
W_C = 1024
N_GROUPS = 4
E_PER_GROUP = 8
N_EXPERTS = 32
D_EXPERT = 512
EPS = 1e-6
IN_AB = 5 * W_A + 2 * QK_B + W_B
IN_C = 4 * W_C + 4 * H_C

LANE = 128
N_PROMPT = BATCH * SEQ
N_SAMPLE = DEC_BATCH * DEC_SEQ
N_TOK = N_PROMPT + N_SAMPLE
TM = 256
N_TILES = N_TOK // TM
PROMPT_TILES = N_PROMPT // TM
TILES_PER_SAMPLE = DEC_SEQ // TM
N_MOD = 1 + DEC_BATCH
MOD_ROWS = 8
MOE_ROWS = 256
MOE_AHEAD = 2
ROW_TILE = D_MODEL // LANE
GDN_CHUNK = 128
VMEM_LIMIT = 56 * 1024 * 1024


def _mod_index(t):
    return jnp.where(t < PROMPT_TILES, 0, 1 + (t - PROMPT_TILES) // TILES_PER_SAMPLE)


def _sigmoid(x):
    return 1.0 / (1.0 + jnp.exp(-x))


def _silu(x):
    return x * _sigmoid(x)


def _rms(x, w):
    return x * lax.rsqrt(jnp.mean(x * x, axis=-1, keepdims=True) + EPS) * w


def _dot(a, b):
    return jnp.dot(a.astype(BF16), b.astype(BF16), preferred_element_type=F32)


def _dot_nt(a, b):
    return lax.dot_general(a.astype(BF16), b.astype(BF16), (((1,), (1,)), ((), ())),
                           preferred_element_type=F32)


def _dot_tn(a, b):
    return lax.dot_general(a.astype(BF16), b.astype(BF16), (((0,), (0,)), ((), ())),
                           preferred_element_type=F32)


def _dot_f32(a, b):
    return jnp.dot(a, b, precision=HIGHEST, preferred_element_type=F32)


def _tri_matmul(tri, x):
    hi = x.astype(BF16)
    rem = x - hi.astype(F32)
    mid = rem.astype(BF16)
    lo = (rem - mid.astype(F32)).astype(BF16)
    n = x.shape[1]
    r = jnp.dot(tri.astype(BF16), jnp.concatenate([hi, mid, lo], axis=1), preferred_element_type=F32)
    return r[:, :n] + r[:, n:2 * n] + r[:, 2 * n:]


def _params(sem):
    return pltpu.CompilerParams(dimension_semantics=sem, vmem_limit_bytes=VMEM_LIMIT)


def _adaln_kernel(c_ref, w_ref, b_ref, o_ref):
    s = _silu(c_ref[...])
    o_ref[0] = _dot(s, w_ref[0]) + b_ref[0]


def _adaln(cvec, ada_w, ada_b):
    nb = 4
    wb = 6 * D_MODEL // nb
    return pl.pallas_call(
        _adaln_kernel,
        out_shape=jax.ShapeDtypeStruct((DEPTH, MOD_ROWS, 6 * D_MODEL), F32),
        grid=(DEPTH, nb),
        in_specs=[pl.BlockSpec((MOD_ROWS, D_MODEL), lambda l, j: (0, 0)),
                  pl.BlockSpec((1, D_MODEL, wb), lambda l, j: (l, 0, j)),
                  pl.BlockSpec((1, 1, wb), lambda l, j: (l, 0, j))],
        out_specs=pl.BlockSpec((1, MOD_ROWS, wb), lambda l, j: (l, 0, j)),
        compiler_params=_params(("parallel", "parallel")),
        name="adaln",
    )(cvec, ada_w, ada_b.reshape(DEPTH, 1, 6 * D_MODEL))


def _row_specs(x, width):
    if isinstance(x, tuple):
        return ([pl.BlockSpec((TM, width), lambda t: (jnp.minimum(t, PROMPT_TILES - 1), 0)),
                 pl.BlockSpec((TM, width), lambda t: (jnp.maximum(t - PROMPT_TILES, 0), 0))], list(x))
    return [pl.BlockSpec((TM, width), lambda t: (t, 0))], [x]


def _read_rows(refs):
    if len(refs) == 2:
        return jnp.where(pl.program_id(0) < PROMPT_TILES, refs[0][...], refs[1][...])
    return refs[0][...]


def _inproj_kernel(*refs, n_x, rope_cols):
    x = _read_rows(refs[:n_x])
    if rope_cols is None:
        mod_ref, nw_ref, w_ref, o_ref = refs[n_x:]
    else:
        mod_ref, nw_ref, w_ref, cos_ref, sin_ref, o_ref = refs[n_x:]
    h = _rms(x, nw_ref[...]) * (1.0 + mod_ref[0, 1:2, :]) + mod_ref[0, 0:1, :]
    r = jnp.dot(h.astype(BF16), w_ref[...], preferred_element_type=F32)
    if rope_cols is None:
        o_ref[...] = r
        return
    lo, hi = rope_cols
    o_ref[:, :lo] = r[:, :lo]
    o_ref[:, hi:] = r[:, hi:]
    cos = cos_ref[...]
    sin = sin_ref[...]
    lane = lax.broadcasted_iota(jnp.int32, (TM, LANE), 1)
    upper = (lane & 16) != 0
    for c0 in range(lo, hi, LANE):
        v = r[:, c0:c0 + LANE]
        partner = jnp.where(upper, pltpu.roll(v, 16, 1), pltpu.roll(v, LANE - 16, 1))
        o_ref[:, c0:c0 + LANE] = v * cos + partner * sin


def _inproj(x, mod, norm_w, w_bf16, rope=None):
    p = w_bf16.shape[1]
    in_specs, args = _row_specs(x, D_MODEL)
    n_x = len(args)
    in_specs += [pl.BlockSpec((1, MOD_ROWS, D_MODEL), lambda t: (_mod_index(t), 0, 0)),
                 pl.BlockSpec((1, D_MODEL), lambda t: (0, 0)),
                 pl.BlockSpec((D_MODEL, p), lambda t: (0, 0))]
    args += [mod, norm_w.reshape(1, D_MODEL), w_bf16]
    rope_cols = None
    if rope is not None:
        cos, sin, rope_cols = rope

        def rope_index(t):
            return (jnp.where(t < PROMPT_TILES, 0, 1 + (t - PROMPT_TILES) % TILES_PER_SAMPLE), 0)

        in_specs += [pl.BlockSpec((TM, LANE), rope_index), pl.BlockSpec((TM, LANE), rope_index)]
        args += [cos, sin]
    return pl.pallas_call(
        functools.partial(_inproj_kernel, n_x=n_x, rope_cols=rope_cols),
        out_shape=jax.ShapeDtypeStruct((N_TOK, p), F32),
        grid=(N_TILES,),
        in_specs=in_specs,
        out_specs=pl.BlockSpec((TM, p), lambda t: (t, 0)),
        compiler_params=_params(("parallel",)),
        name="inproj",
    )(*args)


def _rope_tables():
    lane = jnp.arange(LANE)
    d = lane % DH_B
    use_col = (d // 32) == 1
    j = d % 16
    upper = ((d % 32) // 16) == 1
    inv_freq = ROPE_BASE ** (-j.astype(F32) / 16.0)
    t = jnp.arange(DEC_SEQ)
    row = (t // GRID_W).astype(F32)
    col = (t % GRID_W).astype(F32)
    pos = jnp.where(use_col[None, :], col[:, None], row[:, None])
    ang = pos * inv_freq[None, :]
    cos = jnp.cos(ang)
    sin = jnp.where(upper[None, :], jnp.sin(ang), -jnp.sin(ang))
    cos = jnp.concatenate([jnp.ones((TM, LANE), F32), cos], axis=0)
    sin = jnp.concatenate([jnp.zeros((TM, LANE), F32), sin], axis=0)
    return cos, sin


def _route(logits):
    lane = lax.broadcasted_iota(jnp.int32, logits.shape, 1)
    lanef = lane.astype(F32)
    neg = jnp.float32(-jnp.inf)
    gl = jnp.where(lane < N_GROUPS, logits, neg)
    gmax = jnp.max(gl, axis=1, keepdims=True)
    gsel = jnp.min(jnp.where(gl == gmax, lanef, float(LANE)), axis=1, keepdims=True)
    p_grp = 1.0 / jnp.sum(jnp.exp(gl - gmax), axis=1, keepdims=True)
    lo = float(N_GROUPS) + gsel * float(E_PER_GROUP)
    el = jnp.where((lanef >= lo) & (lanef < lo + float(E_PER_GROUP)), logits, neg)
    v1 = jnp.max(el, axis=1, keepdims=True)
    i1 = jnp.min(jnp.where(el == v1, lanef, float(LANE)), axis=1, keepdims=True)
    el2 = jnp.where(lanef == i1, neg, el)
    v2 = jnp.max(el2, axis=1, keepdims=True)
    i2 = jnp.min(jnp.where(el2 == v2, lanef, float(LANE)), axis=1, keepdims=True)
    t = jnp.exp(v2 - v1)
    w1 = p_grp / (1.0 + t)
    w2 = p_grp * t / (1.0 + t)
    out = jnp.where(lane == 0, i1 - float(N_GROUPS), 0.0)
    out = jnp.where(lane == 1, i2 - float(N_GROUPS), out)
    out = jnp.where(lane == 2, w1, out)
    out = jnp.where(lane == 3, w2, out)
    return out


def _outproj_kernel(*refs, widths, n_x):
    n_in = len(widths)
    x_refs = refs[2 * n_in:2 * n_in + n_x]
    w_ref, mod_ref, nw_ref, wr_ref, xn_ref, h2_ref, rt_ref = refs[2 * n_in + n_x:]
    y = None
    c0 = 0
    for k, wd in enumerate(widths):
        o = _read_rows(refs[2 * k:2 * k + 2])
        part = jnp.dot(o.astype(BF16), w_ref[c0:c0 + wd, :], preferred_element_type=F32)
        y = part if y is None else y + part
        c0 += wd
    xn = _read_rows(x_refs) + mod_ref[0, 2:3, :] * y
    xn_ref[...] = xn
    h2 = _rms(xn, nw_ref[...]) * (1.0 + mod_ref[0, 4:5, :]) + mod_ref[0, 3:4, :]
    for s in range(ROW_TILE):
        h2_ref[pl.ds(s, TM, stride=ROW_TILE), :] = h2[:, s * LANE:(s + 1) * LANE]
    h_hi = h2.astype(BF16)
    h_lo = (h2 - h_hi.astype(F32)).astype(BF16)
    wr = wr_ref[...]
    hw = jnp.dot(h_hi, wr, preferred_element_type=F32)
    logits = hw[:, :LANE] + hw[:, LANE:] + jnp.dot(h_lo, wr[:, :LANE], preferred_element_type=F32)
    rt_ref[...] = _route(logits)


def _outproj(parts, w_bf16, x, mod, norm_w2, w_router):
    widths = tuple(p[0].shape[1] for p in parts)
    in_specs, args = [], []
    for p, wd in zip(parts, widths):
        specs, ops = _row_specs(p, wd)
        in_specs += specs
        args += ops
    specs, ops = _row_specs(x, D_MODEL)
    in_specs += specs
    args += ops
    in_specs += [pl.BlockSpec((D_MODEL, D_MODEL), lambda t: (0, 0)),
                 pl.BlockSpec((1, MOD_ROWS, D_MODEL), lambda t: (_mod_index(t), 0, 0)),
                 pl.BlockSpec((1, D_MODEL), lambda t: (0, 0)),
                 pl.BlockSpec((D_MODEL, 2 * LANE), lambda t: (0, 0))]
    args += [w_bf16, mod, norm_w2.reshape(1, D_MODEL), w_router]
    return pl.pallas_call(
        functools.partial(_outproj_kernel, widths=widths, n_x=len(ops)),
        out_shape=(jax.ShapeDtypeStruct((N_TOK, D_MODEL), F32),
                   jax.ShapeDtypeStruct((N_TOK * ROW_TILE, LANE), F32),
                   jax.ShapeDtypeStruct((N_TOK, LANE), F32)),
        grid=(N_TILES,),
        in_specs=in_specs,
        out_specs=(pl.BlockSpec((TM, D_MODEL), lambda t: (t, 0)),
                   pl.BlockSpec((TM * ROW_TILE, LANE), lambda t: (t, 0)),
                   pl.BlockSpec((TM, LANE), lambda t: (t, 0))),
        compiler_params=_params(("parallel",)),
        name="outproj",
    )(*args)


def _expert_kernel(blk_e_ref, nact_ref, rowtok_ref, h2_hbm, wg_ref, wu_ref, wd_ref, y_ref,
                   x_0, x_1, x_2, sems, wg_s, wu_s, wd_s):
    i = pl.program_id(0)
    n_act = nact_ref[0]
    e = blk_e_ref[i]
    prev = blk_e_ref[jnp.maximum(i - 1, 0)]
    bufs = (x_0, x_1, x_2)
    ring = MOE_AHEAD + 1
    assert ring == len(bufs)

    def row_copy(blk, r, slot):
        tok = rowtok_ref[blk * MOE_ROWS + r]
        return pltpu.make_async_copy(
            h2_hbm.at[pl.ds(pl.multiple_of(tok * ROW_TILE, ROW_TILE), ROW_TILE), :],
            bufs[slot].at[pl.ds(pl.multiple_of(r * ROW_TILE, ROW_TILE), ROW_TILE), :],
            sems.at[slot])

    def wait_block(slot):
        pltpu.make_async_copy(h2_hbm.at[pl.ds(0, MOE_ROWS * ROW_TILE), :], bufs[slot], sems.at[slot]).wait()

    @pl.when(i == 0)
    def _():
        for blk in range(MOE_AHEAD):
            def start_row(r, carry):
                row_copy(blk, r, blk).start()
                return carry
            lax.fori_loop(0, MOE_ROWS, start_row, 0)

    @pl.when((i == 0) | (e != prev))
    def _():
        wg_s[...] = wg_ref[...].astype(BF16)
        wu_s[...] = wu_ref[...].astype(BF16)
        wd_s[...] = wd_ref[...].astype(BF16)

    for slot in range(ring):
        @pl.when((i >= n_act) & (i < n_act + MOE_AHEAD) & (i % ring == slot))
        def _():
            wait_block(slot)

        @pl.when((i < n_act) & (i % ring == slot))
        def _():
            wait_block(slot)
            for r in range(MOE_ROWS):
                row_copy(i + MOE_AHEAD, r, (slot + MOE_AHEAD) % ring).start(priority=r % 2)
            x = jnp.concatenate([bufs[slot][pl.ds(s, MOE_ROWS, stride=ROW_TILE), :] for s in range(ROW_TILE)],
                                axis=1).astype(BF16)
            g = jnp.dot(x, wg_s[...], preferred_element_type=F32)
            u = jnp.dot(x, wu_s[...], preferred_element_type=F32)
            y_ref[...] = jnp.dot((_silu(g) * u).astype(BF16), wd_s[...], preferred_element_type=F32)

    @pl.when(i >= nact_ref[0])
    def _():
        y_ref[...] = jnp.zeros_like(y_ref)


def _experts(h2_tiles, row_tok, blk_e, n_active, w_gate, w_up, w_down, layer):
    n_rows = row_tok.shape[0]
    n_blocks = n_rows // MOE_ROWS
    grid_spec = pltpu.PrefetchScalarGridSpec(
        num_scalar_prefetch=3,
        grid=(n_blocks,),
        in_specs=[pl.BlockSpec(memory_space=pl.ANY),
                  pl.BlockSpec((None, None, D_MODEL, D_EXPERT), lambda i, be, na, rt: (layer, be[i], 0, 0)),
                  pl.BlockSpec((None, None, D_MODEL, D_EXPERT), lambda i, be, na, rt: (layer, be[i], 0, 0)),
                  pl.BlockSpec((None, None, D_EXPERT, D_MODEL), lambda i, be, na, rt: (layer, be[i], 0, 0))],
        out_specs=pl.BlockSpec((MOE_ROWS, D_MODEL), lambda i, be, na, rt: (i, 0)),
        scratch_shapes=[pltpu.VMEM((MOE_ROWS * ROW_TILE, LANE), F32),
                        pltpu.VMEM((MOE_ROWS * ROW_TILE, LANE), F32),
                        pltpu.VMEM((MOE_ROWS * ROW_TILE, LANE), F32),
                        pltpu.SemaphoreType.DMA((MOE_AHEAD + 1,)),
                        pltpu.VMEM((D_MODEL, D_EXPERT), BF16),
                        pltpu.VMEM((D_MODEL, D_EXPERT), BF16),
                        pltpu.VMEM((D_EXPERT, D_MODEL), BF16)])
    return pl.pallas_call(
        _expert_kernel,
        out_shape=jax.ShapeDtypeStruct((n_rows, D_MODEL), F32),
        grid_spec=grid_spec,
        compiler_params=_params(("arbitrary",)),
        name="experts",
    )(blk_e, n_active, row_tok, h2_tiles, w_gate, w_up, w_down)


def _combine_kernel(x_ref, y0_ref, y1_ref, rt_ref, mod_ref, fw_ref, o_ref, *, final):
    rt = rt_ref[...]
    y = rt[:, 2:3] * y0_ref[...] + rt[:, 3:4] * y1_ref[...]
    x = x_ref[...] + mod_ref[0, 5:6, :] * y
    o_ref[...] = _rms(x, fw_ref[...]) if final else x


def _combine(x, y0, y1, route, mod, final_w, final=False, tile0=0, n_rows=N_TOK):
    row = pl.BlockSpec((TM, D_MODEL), lambda t: (t + tile0, 0))
    return pl.pallas_call(
        functools.partial(_combine_kernel, final=final),
        out_shape=jax.ShapeDtypeStruct((n_rows, D_MODEL), F32),
        grid=(n_rows // TM,),
        in_specs=[row, row, row,
                  pl.BlockSpec((TM, LANE), lambda t: (t + tile0, 0)),
                  pl.BlockSpec((1, MOD_ROWS, D_MODEL), lambda t: (_mod_index(t + tile0), 0, 0)),
                  pl.BlockSpec((1, D_MODEL), lambda t: (0, 0))],
        out_specs=pl.BlockSpec((TM, D_MODEL), lambda t: (t, 0)),
        compiler_params=_params(("parallel",)),
        name="combine",
    )(x, y0, y1, route, mod, final_w.reshape(1, D_MODEL))


def _take_rows(a, idx):
    return a.at[idx].get(mode="promise_in_bounds")


def _moe(x, h2, route, mod, w_gate, w_up, w_down, layer, final_w):
    n_asg = 2 * N_TOK
    flat_e = route[:, :2].astype(jnp.int32).reshape(n_asg)
    flat_tok = jnp.arange(n_asg, dtype=jnp.int32) // 2
    onehot = (flat_e[:, None] == jnp.arange(N_EXPERTS, dtype=jnp.int32)[None, :]).astype(jnp.int32)
    csum = jnp.cumsum(onehot, axis=0)
    rank = jnp.take_along_axis(csum, flat_e[:, None], axis=1)[:, 0] - 1
    counts = csum[-1]
    padded = (counts + MOE_ROWS - 1) // MOE_ROWS * MOE_ROWS
    pad_end = jnp.cumsum(padded)
    pad_start = pad_end - padded
    dest = pad_start[flat_e] + rank
    n_rows = n_asg + (N_EXPERTS + MOE_AHEAD) * MOE_ROWS
    n_blocks = n_rows // MOE_ROWS
    row_tok = (jnp.arange(n_rows, dtype=jnp.int32) % N_TOK).at[dest].set(
        flat_tok, unique_indices=True, mode="promise_in_bounds")
    blk_start = jnp.arange(n_blocks, dtype=jnp.int32) * MOE_ROWS
    blk_e = jnp.sum((blk_start[:, None] >= pad_end[None, :]).astype(jnp.int32), axis=1)
    blk_e = jnp.minimum(blk_e, N_EXPERTS - 1)
    n_active = (pad_end[-1:] // MOE_ROWS).astype(jnp.int32)
    yb = _experts(h2, row_tok, blk_e, n_active, w_gate, w_up, w_down, layer)
    dest2 = dest.reshape(N_TOK, 2)
    y0 = _take_rows(yb, dest2[:, 0])
    y1 = _take_rows(yb, dest2[:, 1])
    if layer < DEPTH - 1:
        return _combine(x, y0, y1, route, mod, final_w)
    return (_combine(x, y0, y1, route, mod, final_w, final=True, n_rows=N_PROMPT),
            _combine(x, y0, y1, route, mod, final_w, final=True, tile0=PROMPT_TILES, n_rows=N_SAMPLE))


def _hgrn_kernel(*refs, seq_len, layer, context):
    if context:
        (q_ref, ff_ref, fb_ref, v_ref, g_ref, lbl_ref, nw_ref,
         o_ref, sfin_ref, of_s, ob_s, sf_s, sb_s) = refs
    else:
        (q_ref, ff_ref, fb_ref, v_ref, g_ref, lbl_ref, nw_ref, s0_ref,
         o_ref, of_s, ob_s, sf_s, sb_s) = refs
    c = CHUNK_A
    n = seq_len // c

    def lower_bound(d):
        z = lbl_ref[d]
        e = jnp.exp(z - jnp.max(z, axis=0, keepdims=True))
        return jnp.sum(e[:layer + 1], axis=0, keepdims=True) / jnp.sum(e, axis=0, keepdims=True)

    lb_f = lower_bound(0)
    lb_b = lower_bound(1)
    row = lax.broadcasted_iota(jnp.int32, (c, c), 0)
    col = lax.broadcasted_iota(jnp.int32, (c, c), 1)
    causal = row >= col
    tri_f = causal.astype(F32)
    tri_b = (row <= col).astype(F32)

    if context:
        sf_s[...] = jnp.zeros_like(sf_s)
        sb_s[...] = jnp.zeros_like(sb_s)
    else:
        sf_s[...] = s0_ref[0, 0, 0].T
        sb_s[...] = s0_ref[0, 1, 0].T

    group = min(16, n)
    dirs = ((ff_ref, lb_f, tri_f, causal, c - 1, sf_s, of_s),
            (fb_ref, lb_b, tri_b, row <= col, 0, sb_s, ob_s))

    def body(i, carry):
        items = []
        for g in range(group):
            j = i * group + g
            items.append((0, pl.ds(pl.multiple_of(j * c, c), c)))
            items.append((1, pl.ds(pl.multiple_of((n - 1 - j) * c, c), c)))
        fs = [dirs[d][1] + (1.0 - dirs[d][1]) * _sigmoid(dirs[d][0][rows, :]) for d, rows in items]
        bs = [_tri_matmul(dirs[d][2], jnp.log(f)) for (d, _), f in zip(items, fs)]
        b_lasts = [b[dirs[d][4]:dirs[d][4] + 1, :] for (d, _), b in zip(items, bs)]
        q_ins = [_silu(q_ref[rows, :]) * jnp.exp(b) for (_, rows), b in zip(items, bs)]
        a_s = [jnp.where(dirs[d][3], _dot_nt(q_in, (1.0 - f) * jnp.exp(-b)), 0.0)
               for (d, _), q_in, f, b in zip(items, q_ins, fs, bs)]
        vs = [v_ref[rows, :] for _, rows in items]
        o_intra = [_dot(a, v) for a, v in zip(a_s, vs)]
        u_ts = [_dot_tn(v, (1.0 - f) * jnp.exp(b_last - b)) for v, f, b, b_last in zip(vs, fs, bs, b_lasts)]
        for d in (0, 1):
            st_ref, out_s = dirs[d][5], dirs[d][6]
            st = st_ref[...]
            for k, (dk, rows) in enumerate(items):
                if dk == d:
                    out_s[rows, :] = o_intra[k] + _dot_nt(q_ins[k], st)
                    st = st * jnp.exp(b_lasts[k]) + u_ts[k]
            st_ref[...] = st
        return carry

    lax.fori_loop(0, n // group, body, 0)

    if context:
        sfin_ref[0, 0, 0] = sf_s[...].T
        sfin_ref[0, 1, 0] = sb_s[...].T

    nw = nw_ref[...]

    def epilogue(j, carry):
        rows = pl.ds(pl.multiple_of(j * TM, TM), TM)
        o = of_s[rows, :] + ob_s[rows, :]
        o_ref[rows, :] = _rms(o, nw) * _silu(g_ref[rows, :])
        return carry

    lax.fori_loop(0, seq_len // TM, epilogue, 0)


def _hgrn(proj, lb_logits, norm_w, layer, context, s0=None):
    seq_len = SEQ if context else DEC_SEQ
    bsz = BATCH if context else DEC_BATCH
    row0 = 0 if context else N_PROMPT // seq_len

    def col(k):
        return pl.BlockSpec((seq_len, LANE), lambda b, h: (row0 + b, k * H_A + h))

    in_specs = [col(0), col(1), col(2), col(3), col(4),
                pl.BlockSpec((2, DEPTH + 1, LANE), lambda b, h: (0, 0, h)),
                pl.BlockSpec((1, LANE), lambda b, h: (0, 0))]
    args = [proj, proj, proj, proj, proj, lb_logits, norm_w.reshape(1, LANE)]
    o_shape = jax.ShapeDtypeStruct((bsz * seq_len, W_A), F32)
    o_spec = pl.BlockSpec((seq_len, LANE), lambda b, h: (b, h))
    scratch = [pltpu.VMEM((seq_len, LANE), F32), pltpu.VMEM((seq_len, LANE), F32),
               pltpu.VMEM((LANE, LANE), F32), pltpu.VMEM((LANE, LANE), F32)]
    kern = functools.partial(_hgrn_kernel, seq_len=seq_len, layer=layer, context=context)
    if context:
        return pl.pallas_call(
            kern,
            out_shape=(o_shape, jax.ShapeDtypeStruct((bsz, 2, H_A, DK_A, DK_A), F32)),
            grid=(bsz, H_A),
            in_specs=in_specs,
            out_specs=(o_spec, pl.BlockSpec((1, 2, 1, DK_A, DK_A), lambda b, h: (b, 0, h, 0, 0))),
            scratch_shapes=scratch,
            compiler_params=_params(("parallel", "parallel")),
            name="hgrn_ctx",
        )(*args)
    in_specs += [pl.BlockSpec((1, 2, 1, DK_A, DK_A), lambda b, h: (b, 0, h, 0, 0))]
    args += [s0]
    return pl.pallas_call(
        kern,
        out_shape=o_shape,
        grid=(bsz, H_A),
        in_specs=in_specs,
        out_specs=o_spec,
        scratch_shapes=scratch,
        compiler_params=_params(("parallel", "parallel")),
        name="hgrn_dec",
    )(*args)


ATT_SUB = 256
ATT_TQ = 512
COL_QD = 5 * W_A // LANE
COL_KD = COL_QD + QK_B // LANE
COL_VD = COL_KD + QK_B // LANE


def _attn_kernel(*refs, seq_len, layer, context):
    if context:
        q_ref, k_ref, v_ref, lam_ref, nw_ref, o_ref, k_s, v_s = refs
    else:
        q_ref, k_ref, v_ref, ck_ref, cv_ref, lam_ref, nw_ref, o_ref, k_s, v_s = refs

    @pl.when(pl.program_id(2) == 0)
    def _():
        k_s[0:seq_len, :] = k_ref[...].astype(BF16)
        v_s[0:seq_len, :] = v_ref[...].astype(BF16)
        if not context:
            k_s[seq_len:, :] = ck_ref[0].astype(BF16)
            v_s[seq_len:, :] = cv_ref[0].astype(BF16)

    lam_init = 0.8 - 0.6 * math.exp(-0.3 * layer)
    lp = lam_ref[...]
    lam = (jnp.exp(jnp.sum(lp[0:1] * lp[1:2], axis=1, keepdims=True))
           - jnp.exp(jnp.sum(lp[2:3] * lp[3:4], axis=1, keepdims=True)) + lam_init)

    k = k_s[...]
    v = v_s[...]
    nw = nw_ref[...]
    lane = lax.broadcasted_iota(jnp.int32, (ATT_SUB, LANE), 1)
    for r0 in range(0, q_ref.shape[0], ATT_SUB):
        q = q_ref[r0:r0 + ATT_SUB, :] * (DH_B ** -0.5 * math.log2(math.e))
        s0 = _dot_nt(jnp.where(lane < DH_B, q, 0.0), k)
        s1 = _dot_nt(jnp.where(lane < DH_B, 0.0, q), k)
        p0 = jnp.exp2(s0 - jnp.max(s0, axis=1, keepdims=True))
        p1 = jnp.exp2(s1 - jnp.max(s1, axis=1, keepdims=True))
        l0 = jnp.sum(p0, axis=1, keepdims=True)
        l1 = jnp.sum(p1, axis=1, keepdims=True)
        a = p0 - (lam * l0 / l1) * p1
        o = jnp.dot(a.astype(BF16), v, preferred_element_type=F32) / l0
        o_ref[r0:r0 + ATT_SUB, :] = _rms(o, nw) * (1.0 - lam_init)


def _attn(proj, lam_p, norm_w, layer, context, cache_k=None, cache_v=None):
    seq_len = SEQ if context else DEC_SEQ
    bsz = BATCH if context else DEC_BATCH
    row0 = 0 if context else N_PROMPT // seq_len
    tq = min(ATT_TQ, seq_len)
    nq = seq_len // tq
    tile0 = row0 * nq
    t_k = seq_len if context else seq_len + PAST_LEN
    in_specs = [pl.BlockSpec((tq, LANE), lambda b, h, i: (tile0 + b * nq + i, COL_QD + h)),
                pl.BlockSpec((seq_len, LANE), lambda b, h, i: (row0 + b, COL_KD + h)),
                pl.BlockSpec((seq_len, LANE), lambda b, h, i: (row0 + b, COL_VD + h))]
    args = [proj, proj, proj]
    if not context:
        in_specs += [pl.BlockSpec((1, PAST_LEN, LANE), lambda b, h, i: (b, 0, h)),
                     pl.BlockSpec((1, PAST_LEN, LANE), lambda b, h, i: (b, 0, h))]
        args += [cache_k, cache_v]
    in_specs += [pl.BlockSpec((4, DH_B), lambda b, h, i: (0, 0)),
                 pl.BlockSpec((1, LANE), lambda b, h, i: (0, 0))]
    args += [lam_p, norm_w.reshape(1, LANE)]
    return pl.pallas_call(
        functools.partial(_attn_kernel, seq_len=seq_len, layer=layer, context=context),
        out_shape=jax.ShapeDtypeStruct((bsz * seq_len, W_B), F32),
        grid=(bsz, H_B, nq),
        in_specs=in_specs,
        out_specs=pl.BlockSpec((tq, LANE), lambda b, h, i: (b * nq + i, h)),
        scratch_shapes=[pltpu.VMEM((t_k, LANE), BF16), pltpu.VMEM((t_k, LANE), BF16)],
        compiler_params=_params(("parallel", "parallel", "arbitrary")),
        name="attn_ctx" if context else "attn_dec",
    )(*args)


COL_GATES = 4 * W_C // LANE
GATE_CHUNKS = 4
GDN_CTX_SEQS = 4


def _gdn_gates_kernel(g_ref, alog_ref, dtb_ref, col_ref, row_ref):
    c = GDN_CHUNK
    row = lax.broadcasted_iota(jnp.int32, (c, c), 0)
    col = lax.broadcasted_iota(jnp.int32, (c, c), 1)
    lane = lax.broadcasted_iota(jnp.int32, (c, LANE), 1)
    lower = (row >= col).astype(F32)
    upper = (row <= col).astype(F32)
    for j in range(GATE_CHUNKS):
        raw = g_ref[j * c:(j + 1) * c, :]
        z = raw + dtb_ref[...]
        softplus = jnp.maximum(z, 0.0) + jnp.log(1.0 + jnp.exp(-jnp.abs(z)))
        g = -jnp.exp(alog_ref[...]) * softplus
        out = jnp.where(lane < H_C, _tri_matmul(lower, g),
                        jnp.where(lane < 2 * H_C, _tri_matmul(upper, g), _sigmoid(raw)))
        col_ref[j * c:(j + 1) * c, :] = out
        row_ref[j] = out.T


def _gdn_gates(proj, alog_lane, dtb_lane):
    c = GDN_CHUNK
    n = N_TOK // c
    return pl.pallas_call(
        _gdn_gates_kernel,
        out_shape=(jax.ShapeDtypeStruct((N_TOK, LANE), F32), jax.ShapeDtypeStruct((n, LANE, c), F32)),
        grid=(n // GATE_CHUNKS,),
        in_specs=[pl.BlockSpec((GATE_CHUNKS * c, LANE), lambda i: (i, COL_GATES)),
                  pl.BlockSpec((1, LANE), lambda i: (0, 0)),
                  pl.BlockSpec((1, LANE), lambda i: (0, 0))],
        out_specs=(pl.BlockSpec((GATE_CHUNKS * c, LANE), lambda i: (i, 0)),
                   pl.BlockSpec((GATE_CHUNKS, LANE, c), lambda i: (i, 0, 0))),
        compiler_params=_params(("parallel",)),
        name="gdn_gates",
    )(proj, alog_lane, dtb_lane)


def _unit_tri_inverse_pairs(ms, row, col):
    c = ms[0].shape[0]

    def mm(xs, ys):
        out = []
        for x, y in zip(xs, ys):
            xb = x.astype(BF16)
            yb = y.astype(BF16)
            out.append(jnp.concatenate(
                [jnp.dot(xb[:, :c], yb[:, :c], preferred_element_type=F32),
                 jnp.dot(xb[:, c:], yb[:, c:], preferred_element_type=F32)], axis=1))
        return out

    def add(xs, ys):
        return [x + y for x, y in zip(xs, ys)]

    eye = (row == col).astype(F32)
    a = [jnp.where((row // 16) == (col // 16), m, 0.0) for m in ms]
    a2 = mm(a, a)
    a4 = mm(a2, a2)
    a8 = mm(a4, a4)
    t = [eye - x for x in a]
    t = add(t, mm(t, a2))
    t = add(t, mm(t, a4))
    t = add(t, mm(t, a8))
    blk = 32
    while blk <= c:
        off = ((row // blk) == (col // blk)) & ((row // (blk // 2)) != (col // (blk // 2)))
        corr = mm(mm(t, [jnp.where(off, m, 0.0) for m in ms]), t)
        t = [x - y for x, y in zip(t, corr)]
        blk *= 2
    return t


def _gdn_kernel(*refs, seq_len, n_seq, context):
    if context:
        (q_ref, k_ref, v_ref, go_ref, cwq_ref, cwk_ref, cwv_ref, gcol_ref, grow_ref, nw_ref,
         o_ref, sfin_ref, su_s, pr_s, of_s, ob_s, sf_s, sb_s) = refs
    else:
        (q_ref, k_ref, v_ref, go_ref, cwq_ref, cwk_ref, cwv_ref, gcol_ref, grow_ref, nw_ref, s0_ref,
         o_ref, su_s, pr_s, of_s, ob_s, sf_s, sb_s) = refs
    c = GDN_CHUNK
    cps = seq_len // c
    n = n_seq * cps
    head = pl.program_id(1)
    row = lax.broadcasted_iota(jnp.int32, (c, c), 0)
    col = lax.broadcasted_iota(jnp.int32, (c, c), 1)
    row2 = lax.broadcasted_iota(jnp.int32, (c, 2 * c), 0)
    col2 = lax.broadcasted_iota(jnp.int32, (c, 2 * c), 1) & (c - 1)
    lane = lax.broadcasted_iota(jnp.int32, (c, LANE), 1)
    rowi = lax.broadcasted_iota(jnp.int32, (c, LANE), 0)

    def gc_row_last(ci, d):
        gc_row = grow_ref[ci, pl.ds(d * H_C + head, 1), :]
        return gc_row, (gc_row[:, c - 1:c] if d == 0 else gc_row[:, 0:1])

    def chunk_inputs(ci):
        r0 = pl.multiple_of(ci * c, c)
        rows = pl.ds(r0, c)

        def conv(x_ref, w_ref):
            cur = x_ref[rows, :]
            before = x_ref[pl.ds(pl.multiple_of(jnp.maximum(r0 - 8, 0), 8), 8), :]
            after = x_ref[pl.ds(pl.multiple_of(jnp.minimum(r0 + c, n * c - 8), 8), 8), :]
            in_seq = lax.rem(ci, cps)
            prev_row = jnp.where(in_seq > 0, before[7:8, :], 0.0)
            next_row = jnp.where(in_seq < cps - 1, after[0:1, :], 0.0)
            xm1 = jnp.where(rowi == 0, prev_row, pltpu.roll(cur, 1, 0))
            xp1 = jnp.where(rowi == c - 1, next_row, pltpu.roll(cur, c - 1, 0))
            w = w_ref[...]
            return _silu(xm1 * w[0:1, :] + cur * w[1:2, :] + xp1 * w[2:3, :])

        q = conv(q_ref, cwq_ref)
        k = conv(k_ref, cwk_ref)
        vn = conv(v_ref, cwv_ref)
        qn = q * lax.rsqrt(jnp.sum(q * q, axis=1, keepdims=True) + EPS) * (DK_C ** -0.5)
        kn = k * lax.rsqrt(jnp.sum(k * k, axis=1, keepdims=True) + EPS)
        kq = _dot_nt(jnp.concatenate([kn, qn], axis=0), kn)
        gates = gcol_ref[rows, :]
        per_dir = []
        ms = []
        for d in (0, 1):
            gc = jnp.sum(jnp.where(lane == d * H_C + head, gates, 0.0), axis=1, keepdims=True)
            beta = jnp.sum(jnp.where(lane == (2 + d) * H_C + head, gates, 0.0), axis=1, keepdims=True)
            gc_row, gc_last = gc_row_last(ci, d)
            incl = (row >= col) if d == 0 else (row <= col)
            strict = (row > col) if d == 0 else (row < col)
            decay = jnp.where(incl, jnp.exp(gc - gc_row), 0.0)
            m = jnp.where(strict, kq[:c] * beta * decay, 0.0)
            per_dir.append((gc, beta, kq[c:] * decay, (kn * jnp.exp(gc_last - gc)).T))
            ms.append(m)
        return jnp.concatenate(ms, axis=1), (rows, qn, kn, vn, per_dir)

    def chunk_outputs(ci, t, rest):
        rows, qn, kn, vn, per_dir = rest
        base = pl.multiple_of(ci * 2 * c, 2 * c)
        for d in (0, 1):
            gc, beta, qk, kd_t = per_dir[d]
            e = jnp.exp(gc)
            uw = _dot(t[:, d * c:(d + 1) * c], jnp.concatenate([vn * beta, kn * (beta * e)], axis=1))
            cross = _dot(jnp.concatenate([kd_t, qk], axis=0), uw)
            su_s[d, rows, :] = cross[:c, :c]
            (of_s, ob_s)[d][rows, :] = cross[c:, :c]
            pr_s[d, pl.ds(base, c), :] = cross[:c, c:].astype(BF16)
            pr_s[d, pl.ds(base + c, c), :] = (qn * e - cross[c:, c:]).astype(BF16)

    group = min(8, n)

    def prepare(j, carry):
        cis = [j * group + g for g in range(group)]
        staged = [chunk_inputs(ci) for ci in cis]
        ts = _unit_tri_inverse_pairs([s[0] for s in staged], row2, col2)
        for ci, t, s in zip(cis, ts, staged):
            chunk_outputs(ci, t, s[1])
        return carry

    lax.fori_loop(0, n // group, prepare, 0)

    def advance(ci, d, s_ref, out_s):
        rows = pl.ds(pl.multiple_of(ci * c, c), c)
        _, gc_last = gc_row_last(ci, d)
        s = s_ref[...]
        ps = jnp.dot(pr_s[d, pl.ds(pl.multiple_of(ci * 2 * c, 2 * c), 2 * c), :], s.astype(BF16),
                     preferred_element_type=F32)
        out_s[rows, :] = out_s[rows, :] + ps[c:]
        s_ref[...] = s * jnp.exp(gc_last) - ps[:c] + su_s[d, rows, :]

    for s in range(n_seq):
        if context:
            sf_s[...] = jnp.zeros_like(sf_s)
            sb_s[...] = jnp.zeros_like(sb_s)
        else:
            sf_s[...] = s0_ref[s, 0, 0]
            sb_s[...] = s0_ref[s, 1, 0]

        def body(i, carry, first=s * cps):
            advance(first + i, 0, sf_s, of_s)
            advance(first + cps - 1 - i, 1, sb_s, ob_s)
            return carry

        lax.fori_loop(0, cps, body, 0)

        if context:
            sfin_ref[s, 0, 0] = sf_s[...]
            sfin_ref[s, 1, 0] = sb_s[...]

    nw = nw_ref[...]

    def epilogue(j, carry):
        rows = pl.ds(pl.multiple_of(j * TM, TM), TM)
        o = of_s[rows, :] + ob_s[rows, :]
        o_ref[rows, :] = _rms(o, nw) * _silu(go_ref[rows, :])
        return carry

    lax.fori_loop(0, n * c // TM, epilogue, 0)


def _gdn(proj, conv_w, gcol, grow, norm_w, context, s0=None):
    seq_len = SEQ if context else DEC_SEQ
    n_seq = GDN_CTX_SEQS if context else 1
    n_blk = (BATCH if context else DEC_BATCH) // n_seq
    rows = n_seq * seq_len
    row0 = 0 if context else N_PROMPT // rows
    nc = rows // GDN_CHUNK

    def col(k):
        return pl.BlockSpec((rows, LANE), lambda b, h: (row0 + b, k * H_C + h))

    def cw(k):
        return pl.BlockSpec((3, LANE), lambda b, h: (0, k * H_C + h))

    in_specs = [col(0), col(1), col(2), col(3), cw(0), cw(1), cw(2),
                pl.BlockSpec((rows, LANE), lambda b, h: (row0 + b, 0)),
                pl.BlockSpec((nc, LANE, GDN_CHUNK), lambda b, h: (row0 + b, 0, 0)),
                pl.BlockSpec((1, LANE), lambda b, h: (0, 0))]
    args = [proj, proj, proj, proj, conv_w, conv_w, conv_w, gcol, grow, norm_w.reshape(1, LANE)]
    o_shape = jax.ShapeDtypeStruct((n_blk * rows, W_C), F32)
    o_spec = pl.BlockSpec((rows, LANE), lambda b, h: (b, h))
    seq_buf = pltpu.VMEM((rows, LANE), F32)
    state_buf = pltpu.VMEM((DK_C, DK_C), F32)
    scratch = [pltpu.VMEM((2, rows, LANE), F32), pltpu.VMEM((2, 2 * rows, LANE), BF16),
               seq_buf, seq_buf, state_buf, state_buf]
    kern = functools.partial(_gdn_kernel, seq_len=seq_len, n_seq=n_seq, context=context)
    state_spec = pl.BlockSpec((n_seq, 2, 1, DK_C, DK_C), lambda b, h: (b, 0, h, 0, 0))
    if context:
        return pl.pallas_call(
            kern,
            out_shape=(o_shape, jax.ShapeDtypeStruct((BATCH, 2, H_C, DK_C, DK_C), F32)),
            grid=(n_blk, H_C),
            in_specs=in_specs,
            out_specs=(o_spec, state_spec),
            scratch_shapes=scratch,
            compiler_params=_params(("parallel", "parallel")),
            name="gdn_ctx",
        )(*args)
    in_specs += [state_spec]
    args += [s0]
    return pl.pallas_call(
        kern,
        out_shape=o_shape,
        grid=(n_blk, H_C),
        in_specs=in_specs,
        out_specs=o_spec,
        scratch_shapes=scratch,
        compiler_params=_params(("parallel", "parallel")),
        name="gdn_dec",
    )(*args)


def kernel(x_prompt, x_sample, c, cache_diff_k, cache_diff_v, state_hgrn, state_gdn, c_ctx,
           ada_w, ada_b, norm_w, final_norm_w, w_in_ab, hgrn_lb_logits, hgrn_norm_w,
           diff_lambda, diff_norm_w, w_in_c, gdn_conv_w, gdn_a_log, gdn_dt_bias, gdn_norm_w,
           w_out, moe_router_group, moe_router_expert, moe_w_gate, moe_w_up, moe_w_down):
    x = (x_prompt.reshape(N_PROMPT, D_MODEL), x_sample.reshape(N_SAMPLE, D_MODEL))
    cvec = jnp.concatenate([c_ctx[None, :], c, jnp.zeros((MOD_ROWS - N_MOD, D_MODEL), F32)], axis=0)
    mod_all = _adaln(cvec, ada_w, ada_b)[:, :N_MOD].reshape(DEPTH, N_MOD, 6, D_MODEL)
    mod_all = jnp.pad(mod_all, ((0, 0), (0, 0), (0, MOD_ROWS - 6), (0, 0)))
    cos, sin = _rope_tables()
    new_k, new_v, new_hgrn, new_gdn = [], [], [], []
    for l in range(DEPTH):
        i = l // 2
        mod = mod_all[l]
        if l % 2 == 0:
            proj = _inproj(x, mod, norm_w[l, 0], w_in_ab[i].astype(BF16),
                           rope=(cos, sin, (COL_QD * LANE, COL_VD * LANE)))
            oh_ctx, s_h = _hgrn(proj, hgrn_lb_logits, hgrn_norm_w[i], l, True)
            oh_dec = _hgrn(proj, hgrn_lb_logits, hgrn_norm_w[i], l, False, s0=state_hgrn[:, i])
            od_ctx = _attn(proj, diff_lambda[i], diff_norm_w[i], l, True)
            od_dec = _attn(proj, diff_lambda[i], diff_norm_w[i], l, False,
                           cache_k=cache_diff_k[:, i].reshape(DEC_BATCH, PAST_LEN, QK_B),
                           cache_v=cache_diff_v[:, i].reshape(DEC_BATCH, PAST_LEN, W_B))
            parts = ((oh_ctx, oh_dec), (od_ctx, od_dec))
            new_k.append(proj[:N_PROMPT, COL_KD * LANE:COL_VD * LANE].reshape(BATCH, SEQ, H_B, 2, DH_B))
            new_v.append(proj[:N_PROMPT, COL_VD * LANE:].reshape(BATCH, SEQ, H_B, DV_B))
            new_hgrn.append(s_h)
        else:
            w_c = jnp.pad(w_in_c[i], ((0, 0), (0, (COL_GATES + 1) * LANE - IN_C))).astype(BF16)
            proj = _inproj(x, mod, norm_w[l, 0], w_c)
            pad = jnp.zeros((LANE - 2 * H_C,), F32)
            alog_lane = jnp.concatenate([gdn_a_log[i, 0], gdn_a_log[i, 1], pad]).reshape(1, LANE)
            dtb_lane = jnp.concatenate([gdn_dt_bias[i, 0], gdn_dt_bias[i, 1], pad]).reshape(1, LANE)
            gcol, grow = _gdn_gates(proj, alog_lane, dtb_lane)
            oc_ctx, s_c = _gdn(proj, gdn_conv_w[i], gcol, grow, gdn_norm_w[i], True)
            oc_dec = _gdn(proj, gdn_conv_w[i], gcol, grow, gdn_norm_w[i], False, s0=state_gdn[:, i])
            parts = ((oc_ctx, oc_dec),)
            new_gdn.append(s_c)
        w_router = jnp.concatenate(
            [moe_router_group[l], moe_router_expert[l],
             jnp.zeros((D_MODEL, LANE - N_GROUPS - N_EXPERTS), F32)], axis=1)
        w_router_hi = w_router.astype(BF16)
        w_router = jnp.concatenate([w_router_hi, (w_router - w_router_hi.astype(F32)).astype(BF16)], axis=1)
        x, h2, route = _outproj(parts, w_out[l].astype(BF16), x, mod, norm_w[l, 1], w_router)
        x = _moe(x, h2, route, mod, moe_w_gate, moe_w_up, moe_w_down, l, final_norm_w)
    y_prompt = x[0].reshape(BATCH, SEQ, D_MODEL)
    y_sample = x[1].reshape(DEC_BATCH, DEC_SEQ, D_MODEL)
    return (y_prompt, y_sample, jnp.stack(new_k, axis=1), jnp.stack(new_v, axis=1),
            jnp.stack(new_hgrn, axis=1), jnp.stack(new_gdn, axis=1))
```

```python
import functools
import math

import jax
import jax.numpy as jnp
from jax import lax
from jax.experimental import pallas as pl
from jax.experimental.pallas import tpu as pltpu

F32 = jnp.float32
BF16 = jnp.bfloat16
HIGHEST = lax.Precision.HIGHEST

D_MODEL = 1024
BATCH = 16
SEQ = 256
DEPTH = 2
DEC_BATCH = 4
DEC_SEQ = 4096
PAST_LEN = 256
GRID_W = 64
H_A = 4
DK_A = 128
W_A = 512
CHUNK_A = 32
H_B = 4
DH_B = 64
DV_B = 128
QK_B = 512
W_B = 512
ROPE_BASE = 10000.0
H_C = 8
DK_C = 128
W_C = 1024
N_GROUPS = 4
E_PER_GROUP = 8
N_EXPERTS = 32
D_EXPERT = 512
EPS = 1e-6
IN_AB = 5 * W_A + 2 * QK_B + W_B
IN_C = 4 * W_C + 4 * H_C

LANE = 128
N_PROMPT = BATCH * SEQ
N_SAMPLE = DEC_BATCH * DEC_SEQ
N_TOK = N_PROMPT + N_SAMPLE
TM = 256
N_TILES = N_TOK // TM
PROMPT_TILES = N_PROMPT // TM
TILES_PER_SAMPLE = DEC_SEQ // TM
N_MOD = 1 + DEC_BATCH
MOD_ROWS = 8
MOE_ROWS = 256
MOE_AHEAD = 2
ROW_TILE = D_MODEL // LANE
GDN_CHUNK = 128
VMEM_LIMIT = 56 * 1024 * 1024


def _mod_index(t):
    return jnp.where(t < PROMPT_TILES, 0, 1 + (t - PROMPT_TILES) // TILES_PER_SAMPLE)


def _sigmoid(x):
    return 1.0 / (1.0 + jnp.exp(-x))


def _silu(x):
    return x * _sigmoid(x)


def _rms(x, w):
    return x * lax.rsqrt(jnp.mean(x * x, axis=-1, keepdims=True) + EPS) * w


def _dot(a, b):
    return jnp.dot(a.astype(BF16), b.astype(BF16), preferred_element_type=F32)


def _dot_nt(a, b):
    return lax.dot_general(a.astype(BF16), b.astype(BF16), (((1,), (1,)), ((), ())),
                           preferred_element_type=F32)


def _dot_tn(a, b):
    return lax.dot_general(a.astype(BF16), b.astype(BF16), (((0,), (0,)), ((), ())),
                           preferred_element_type=F32)


def _dot_f32(a, b):
    return jnp.dot(a, b, precision=HIGHEST, preferred_element_type=F32)


def _tri_matmul(tri, x):
    hi = x.astype(BF16)
    rem = x - hi.astype(F32)
    mid = rem.astype(BF16)
    lo = (rem - mid.astype(F32)).astype(BF16)
    n = x.shape[1]
    r = jnp.dot(tri.astype(BF16), jnp.concatenate([hi, mid, lo], axis=1), preferred_element_type=F32)
    return r[:, :n] + r[:, n:2 * n] + r[:, 2 * n:]


def _params(sem):
    return pltpu.CompilerParams(dimension_semantics=sem, vmem_limit_bytes=VMEM_LIMIT)


def _adaln_kernel(c_ref, w_ref, b_ref, o_ref):
    s = _silu(c_ref[...])
    o_ref[0] = _dot(s, w_ref[0]) + b_ref[0]


def _adaln(cvec, ada_w, ada_b):
    nb = 4
    wb = 6 * D_MODEL // nb
    return pl.pallas_call(
        _adaln_kernel,
        out_shape=jax.ShapeDtypeStruct((DEPTH, MOD_ROWS, 6 * D_MODEL), F32),
        grid=(DEPTH, nb),
        in_specs=[pl.BlockSpec((MOD_ROWS, D_MODEL), lambda l, j: (0, 0)),
                  pl.BlockSpec((1, D_MODEL, wb), lambda l, j: (l, 0, j)),
                  pl.BlockSpec((1, 1, wb), lambda l, j: (l, 0, j))],
        out_specs=pl.BlockSpec((1, MOD_ROWS, wb), lambda l, j: (l, 0, j)),
        compiler_params=_params(("parallel", "parallel")),
        name="adaln",
    )(cvec, ada_w, ada_b.reshape(DEPTH, 1, 6 * D_MODEL))


def _row_specs(x, width):
    if isinstance(x, tuple):
        return ([pl.BlockSpec((TM, width), lambda t: (jnp.minimum(t, PROMPT_TILES - 1), 0)),
                 pl.BlockSpec((TM, width), lambda t: (jnp.maximum(t - PROMPT_TILES, 0), 0))], list(x))
    return [pl.BlockSpec((TM, width), lambda t: (t, 0))], [x]


def _read_rows(refs):
    if len(refs) == 2:
        return jnp.where(pl.program_id(0) < PROMPT_TILES, refs[0][...], refs[1][...])
    return refs[0][...]


def _inproj_kernel(*refs, n_x, rope_cols):
    x = _read_rows(refs[:n_x])
    if rope_cols is None:
        mod_ref, nw_ref, w_ref, o_ref = refs[n_x:]
    else:
        mod_ref, nw_ref, w_ref, cos_ref, sin_ref, o_ref = refs[n_x:]
    h = _rms(x, nw_ref[...]) * (1.0 + mod_ref[0, 1:2, :]) + mod_ref[0, 0:1, :]
    r = jnp.dot(h.astype(BF16), w_ref[...], preferred_element_type=F32)
    if rope_cols is None:
        o_ref[...] = r
        return
    lo, hi = rope_cols
    o_ref[:, :lo] = r[:, :lo]
    o_ref[:, hi:] = r[:, hi:]
    cos = cos_ref[...]
    sin = sin_ref[...]
    lane = lax.broadcasted_iota(jnp.int32, (TM, LANE), 1)
    upper = (lane & 16) != 0
    for c0 in range(lo, hi, LANE):
        v = r[:, c0:c0 + LANE]
        partner = jnp.where(upper, pltpu.roll(v, 16, 1), pltpu.roll(v, LANE - 16, 1))
        o_ref[:, c0:c0 + LANE] = v * cos + partner * sin


def _inproj(x, mod, norm_w, w_bf16, rope=None):
    p = w_bf16.shape[1]
    in_specs, args = _row_specs(x, D_MODEL)
    n_x = len(args)
    in_specs += [pl.BlockSpec((1, MOD_ROWS, D_MODEL), lambda t: (_mod_index(t), 0, 0)),
                 pl.BlockSpec((1, D_MODEL), lambda t: (0, 0)),
                 pl.BlockSpec((D_MODEL, p), lambda t: (0, 0))]
    args += [mod, norm_w.reshape(1, D_MODEL), w_bf16]
    rope_cols = None
    if rope is not None:
        cos, sin, rope_cols = rope

        def rope_index(t):
            return (jnp.where(t < PROMPT_TILES, 0, 1 + (t - PROMPT_TILES) % TILES_PER_SAMPLE), 0)

        in_specs += [pl.BlockSpec((TM, LANE), rope_index), pl.BlockSpec((TM, LANE), rope_index)]
        args += [cos, sin]
    return pl.pallas_call(
        functools.partial(_inproj_kernel, n_x=n_x, rope_cols=rope_cols),
        out_shape=jax.ShapeDtypeStruct((N_TOK, p), F32),
        grid=(N_TILES,),
        in_specs=in_specs,
        out_specs=pl.BlockSpec((TM, p), lambda t: (t, 0)),
        compiler_params=_params(("parallel",)),
        name="inproj",
    )(*args)


def _rope_tables():
    lane = jnp.arange(LANE)
    d = lane % DH_B
    use_col = (d // 32) == 1
    j = d % 16
    upper = ((d % 32) // 16) == 1
    inv_freq = ROPE_BASE ** (-j.astype(F32) / 16.0)
    t = jnp.arange(DEC_SEQ)
    row = (t // GRID_W).astype(F32)
    col = (t % GRID_W).astype(F32)
    pos = jnp.where(use_col[None, :], col[:, None], row[:, None])
    ang = pos * inv_freq[None, :]
    cos = jnp.cos(ang)
    sin = jnp.where(upper[None, :], jnp.sin(ang), -jnp.sin(ang))
    cos = jnp.concatenate([jnp.ones((TM, LANE), F32), cos], axis=0)
    sin = jnp.concatenate([jnp.zeros((TM, LANE), F32), sin], axis=0)
    return cos, sin


def _route(logits):
    lane = lax.broadcasted_iota(jnp.int32, logits.shape, 1)
    lanef = lane.astype(F32)
    neg = jnp.float32(-jnp.inf)
    gl = jnp.where(lane < N_GROUPS, logits, neg)
    gmax = jnp.max(gl, axis=1, keepdims=True)
    gsel = jnp.min(jnp.where(gl == gmax, lanef, float(LANE)), axis=1, keepdims=True)
    p_grp = 1.0 / jnp.sum(jnp.exp(gl - gmax), axis=1, keepdims=True)
    lo = float(N_GROUPS) + gsel * float(E_PER_GROUP)
    el = jnp.where((lanef >= lo) & (lanef < lo + float(E_PER_GROUP)), logits, neg)
    v1 = jnp.max(el, axis=1, keepdims=True)
    i1 = jnp.min(jnp.where(el == v1, lanef, float(LANE)), axis=1, keepdims=True)
    el2 = jnp.where(lanef == i1, neg, el)
    v2 = jnp.max(el2, axis=1, keepdims=True)
    i2 = jnp.min(jnp.where(el2 == v2, lanef, float(LANE)), axis=1, keepdims=True)
    t = jnp.exp(v2 - v1)
    w1 = p_grp / (1.0 + t)
    w2 = p_grp * t / (1.0 + t)
    out = jnp.where(lane == 0, i1 - float(N_GROUPS), 0.0)
    out = jnp.where(lane == 1, i2 - float(N_GROUPS), out)
    out = jnp.where(lane == 2, w1, out)
    out = jnp.where(lane == 3, w2, out)
    return out


def _outproj_kernel(*refs, widths, n_x):
    n_in = len(widths)
    x_refs = refs[2 * n_in:2 * n_in + n_x]
    w_ref, mod_ref, nw_ref, wr_ref, xn_ref, h2_ref, rt_ref = refs[2 * n_in + n_x:]
    y = None
    c0 = 0
    for k, wd in enumerate(widths):
        o = _read_rows(refs[2 * k:2 * k + 2])
        part = jnp.dot(o.astype(BF16), w_ref[c0:c0 + wd, :], preferred_element_type=F32)
        y = part if y is None else y + part
        c0 += wd
    xn = _read_rows(x_refs) + mod_ref[0, 2:3, :] * y
    xn_ref[...] = xn
    h2 = _rms(xn, nw_ref[...]) * (1.0 + mod_ref[0, 4:5, :]) + mod_ref[0, 3:4, :]
    for s in range(ROW_TILE):
        h2_ref[pl.ds(s, TM, stride=ROW_TILE), :] = h2[:, s * LANE:(s + 1) * LANE]
    h_hi = h2.astype(BF16)
    h_lo = (h2 - h_hi.astype(F32)).astype(BF16)
    wr = wr_ref[...]
    hw = jnp.dot(h_hi, wr, preferred_element_type=F32)
    logits = hw[:, :LANE] + hw[:, LANE:] + jnp.dot(h_lo, wr[:, :LANE], preferred_element_type=F32)
    rt_ref[...] = _route(logits)


def _outproj(parts, w_bf16, x, mod, norm_w2, w_router):
    widths = tuple(p[0].shape[1] for p in parts)
    in_specs, args = [], []
    for p, wd in zip(parts, widths):
        specs, ops = _row_specs(p, wd)
        in_specs += specs
        args += ops
    specs, ops = _row_specs(x, D_MODEL)
    in_specs += specs
    args += ops
    in_specs += [pl.BlockSpec((D_MODEL, D_MODEL), lambda t: (0, 0)),
                 pl.BlockSpec((1, MOD_ROWS, D_MODEL), lambda t: (_mod_index(t), 0, 0)),
                 pl.BlockSpec((1, D_MODEL), lambda t: (0, 0)),
                 pl.BlockSpec((D_MODEL, 2 * LANE), lambda t: (0, 0))]
    args += [w_bf16, mod, norm_w2.reshape(1, D_MODEL), w_router]
    return pl.pallas_call(
        functools.partial(_outproj_kernel, widths=widths, n_x=len(ops)),
        out_shape=(jax.ShapeDtypeStruct((N_TOK, D_MODEL), F32),
                   jax.ShapeDtypeStruct((N_TOK * ROW_TILE, LANE), F32),
                   jax.ShapeDtypeStruct((N_TOK, LANE), F32)),
        grid=(N_TILES,),
        in_specs=in_specs,
        out_specs=(pl.BlockSpec((TM, D_MODEL), lambda t: (t, 0)),
                   pl.BlockSpec((TM * ROW_TILE, LANE), lambda t: (t, 0)),
                   pl.BlockSpec((TM, LANE), lambda t: (t, 0))),
        compiler_params=_params(("parallel",)),
        name="outproj",
    )(*args)


def _expert_kernel(blk_e_ref, nact_ref, rowtok_ref, h2_hbm, wg_ref, wu_ref, wd_ref, y_ref,
                   x_0, x_1, x_2, sems, wg_s, wu_s, wd_s):
    i = pl.program_id(0)
    n_act = nact_ref[0]
    e = blk_e_ref[i]
    prev = blk_e_ref[jnp.maximum(i - 1, 0)]
    bufs = (x_0, x_1, x_2)
    ring = MOE_AHEAD + 1
    assert ring == len(bufs)

    def row_copy(blk, r, slot):
        tok = rowtok_ref[blk * MOE_ROWS + r]
        return pltpu.make_async_copy(
            h2_hbm.at[pl.ds(pl.multiple_of(tok * ROW_TILE, ROW_TILE), ROW_TILE), :],
            bufs[slot].at[pl.ds(pl.multiple_of(r * ROW_TILE, ROW_TILE), ROW_TILE), :],
            sems.at[slot])

    def wait_block(slot):
        pltpu.make_async_copy(h2_hbm.at[pl.ds(0, MOE_ROWS * ROW_TILE), :], bufs[slot], sems.at[slot]).wait()

    @pl.when(i == 0)
    def _():
        for blk in range(MOE_AHEAD):
            def start_row(r, carry):
                row_copy(blk, r, blk).start()
                return carry
            lax.fori_loop(0, MOE_ROWS, start_row, 0)

    @pl.when((i == 0) | (e != prev))
    def _():
        wg_s[...] = wg_ref[...].astype(BF16)
        wu_s[...] = wu_ref[...].astype(BF16)
        wd_s[...] = wd_ref[...].astype(BF16)

    for slot in range(ring):
        @pl.when((i >= n_act) & (i < n_act + MOE_AHEAD) & (i % ring == slot))
        def _():
            wait_block(slot)

        @pl.when((i < n_act) & (i % ring == slot))
        def _():
            wait_block(slot)
            for r in range(MOE_ROWS):
                row_copy(i + MOE_AHEAD, r, (slot + MOE_AHEAD) % ring).start(priority=1)
            x = jnp.concatenate([bufs[slot][pl.ds(s, MOE_ROWS, stride=ROW_TILE), :] for s in range(ROW_TILE)],
                                axis=1).astype(BF16)
            g = jnp.dot(x, wg_s[...], preferred_element_type=F32)
            u = jnp.dot(x, wu_s[...], preferred_element_type=F32)
            y_ref[...] = jnp.dot((_silu(g) * u).astype(BF16), wd_s[...], preferred_element_type=F32)

    @pl.when(i >= nact_ref[0])
    def _():
        y_ref[...] = jnp.zeros_like(y_ref)


def _experts(h2_tiles, row_tok, blk_e, n_active, w_gate, w_up, w_down, layer):
    n_rows = row_tok.shape[0]
    n_blocks = n_rows // MOE_ROWS
    grid_spec = pltpu.PrefetchScalarGridSpec(
        num_scalar_prefetch=3,
        grid=(n_blocks,),
        in_specs=[pl.BlockSpec(memory_space=pl.ANY),
                  pl.BlockSpec((None, None, D_MODEL, D_EXPERT), lambda i, be, na, rt: (layer, be[i], 0, 0)),
                  pl.BlockSpec((None, None, D_MODEL, D_EXPERT), lambda i, be, na, rt: (layer, be[i], 0, 0)),
                  pl.BlockSpec((None, None, D_EXPERT, D_MODEL), lambda i, be, na, rt: (layer, be[i], 0, 0))],
        out_specs=pl.BlockSpec((MOE_ROWS, D_MODEL), lambda i, be, na, rt: (i, 0)),
        scratch_shapes=[pltpu.VMEM((MOE_ROWS * ROW_TILE, LANE), F32),
                        pltpu.VMEM((MOE_ROWS * ROW_TILE, LANE), F32),
                        pltpu.VMEM((MOE_ROWS * ROW_TILE, LANE), F32),
                        pltpu.SemaphoreType.DMA((MOE_AHEAD + 1,)),
                        pltpu.VMEM((D_MODEL, D_EXPERT), BF16),
                        pltpu.VMEM((D_MODEL, D_EXPERT), BF16),
                        pltpu.VMEM((D_EXPERT, D_MODEL), BF16)])
    return pl.pallas_call(
        _expert_kernel,
        out_shape=jax.ShapeDtypeStruct((n_rows, D_MODEL), F32),
        grid_spec=grid_spec,
        compiler_params=_params(("arbitrary",)),
        name="experts",
    )(blk_e, n_active, row_tok, h2_tiles, w_gate, w_up, w_down)


def _combine_kernel(x_ref, y0_ref, y1_ref, rt_ref, mod_ref, fw_ref, o_ref, *, final):
    rt = rt_ref[...]
    y = rt[:, 2:3] * y0_ref[...] + rt[:, 3:4] * y1_ref[...]
    x = x_ref[...] + mod_ref[0, 5:6, :] * y
    o_ref[...] = _rms(x, fw_ref[...]) if final else x


def _combine(x, y0, y1, route, mod, final_w, final=False, tile0=0, n_rows=N_TOK):
    row = pl.BlockSpec((TM, D_MODEL), lambda t: (t + tile0, 0))
    return pl.pallas_call(
        functools.partial(_combine_kernel, final=final),
        out_shape=jax.ShapeDtypeStruct((n_rows, D_MODEL), F32),
        grid=(n_rows // TM,),
        in_specs=[row, row, row,
                  pl.BlockSpec((TM, LANE), lambda t: (t + tile0, 0)),
                  pl.BlockSpec((1, MOD_ROWS, D_MODEL), lambda t: (_mod_index(t + tile0), 0, 0)),
                  pl.BlockSpec((1, D_MODEL), lambda t: (0, 0))],
        out_specs=pl.BlockSpec((TM, D_MODEL), lambda t: (t, 0)),
        compiler_params=_params(("parallel",)),
        name="combine",
    )(x, y0, y1, route, mod, final_w.reshape(1, D_MODEL))


def _take_rows(a, idx):
    return a.at[idx].get(mode="promise_in_bounds")


def _moe(x, h2, route, mod, w_gate, w_up, w_down, layer, final_w):
    n_asg = 2 * N_TOK
    flat_e = route[:, :2].astype(jnp.int32).reshape(n_asg)
    onehot = (flat_e[:, None] == jnp.arange(N_EXPERTS, dtype=jnp.int32)[None, :]).astype(jnp.int32)
    csum = jnp.cumsum(onehot, axis=0)
    rank = jnp.take_along_axis(csum, flat_e[:, None], axis=1)[:, 0] - 1
    counts = csum[-1]
    padded = (counts + MOE_ROWS - 1) // MOE_ROWS * MOE_ROWS
    pad_end = jnp.cumsum(padded)
    pad_start = pad_end - padded
    dest = pad_start[flat_e] + rank
    n_rows = n_asg + (N_EXPERTS + MOE_AHEAD) * MOE_ROWS
    n_blocks = n_rows // MOE_ROWS
    blk_start = jnp.arange(n_blocks, dtype=jnp.int32) * MOE_ROWS
    blk_e = jnp.sum((blk_start[:, None] >= pad_end[None, :]).astype(jnp.int32), axis=1)
    blk_e = jnp.minimum(blk_e, N_EXPERTS - 1)
    order = jnp.argsort(flat_e, stable=True).astype(jnp.int32)
    row_e = jnp.repeat(blk_e, MOE_ROWS)
    start = jnp.cumsum(counts) - counts
    src = jnp.arange(n_rows, dtype=jnp.int32) + (start - pad_start)[row_e]
    row_tok = order.at[jnp.clip(src, 0, n_asg - 1)].get(mode="promise_in_bounds") // 2
    n_active = (pad_end[-1:] // MOE_ROWS).astype(jnp.int32)
    yb = _experts(h2, row_tok, blk_e, n_active, w_gate, w_up, w_down, layer)
    dest2 = dest.reshape(N_TOK, 2)
    y0 = _take_rows(yb, dest2[:, 0])
    y1 = _take_rows(yb, dest2[:, 1])
    if layer < DEPTH - 1:
        return _combine(x, y0, y1, route, mod, final_w)
    return (_combine(x, y0, y1, route, mod, final_w, final=True, n_rows=N_PROMPT),
            _combine(x, y0, y1, route, mod, final_w, final=True, tile0=PROMPT_TILES, n_rows=N_SAMPLE))


def _hgrn_kernel(*refs, seq_len, layer, context):
    if context:
        (q_ref, ff_ref, fb_ref, v_ref, g_ref, lbl_ref, nw_ref,
         o_ref, sfin_ref, of_s, ob_s, sf_s, sb_s) = refs
    else:
        (q_ref, ff_ref, fb_ref, v_ref, g_ref, lbl_ref, nw_ref, s0_ref,
         o_ref, of_s, ob_s, sf_s, sb_s) = refs
    c = CHUNK_A
    n = seq_len // c

    def lower_bound(d):
        z = lbl_ref[d]
        e = jnp.exp(z - jnp.max(z, axis=0, keepdims=True))
        return jnp.sum(e[:layer + 1], axis=0, keepdims=True) / jnp.sum(e, axis=0, keepdims=True)

    lb_f = lower_bound(0)
    lb_b = lower_bound(1)
    row = lax.broadcasted_iota(jnp.int32, (c, c), 0)
    col = lax.broadcasted_iota(jnp.int32, (c, c), 1)
    causal = row >= col
    tri_f = causal.astype(F32)
    tri_b = (row <= col).astype(F32)

    if context:
        sf_s[...] = jnp.zeros_like(sf_s)
        sb_s[...] = jnp.zeros_like(sb_s)
    else:
        sf_s[...] = s0_ref[0, 0, 0].T
        sb_s[...] = s0_ref[0, 1, 0].T

    group = min(16, n)
    dirs = ((ff_ref, lb_f, tri_f, causal, c - 1, sf_s, of_s),
            (fb_ref, lb_b, tri_b, row <= col, 0, sb_s, ob_s))

    def body(i, carry):
        items = []
        for g in range(group):
            j = i * group + g
            items.append((0, pl.ds(pl.multiple_of(j * c, c), c)))
            items.append((1, pl.ds(pl.multiple_of((n - 1 - j) * c, c), c)))
        fs = [dirs[d][1] + (1.0 - dirs[d][1]) * _sigmoid(dirs[d][0][rows, :]) for d, rows in items]
        bs = [_tri_matmul(dirs[d][2], jnp.log(f)) for (d, _), f in zip(items, fs)]
        b_lasts = [b[dirs[d][4]:dirs[d][4] + 1, :] for (d, _), b in zip(items, bs)]
        q_ins = [_silu(q_ref[rows, :]) * jnp.exp(b) for (_, rows), b in zip(items, bs)]
        a_s = [jnp.where(dirs[d][3], _dot_nt(q_in, (1.0 - f) * jnp.exp(-b)), 0.0)
               for (d, _), q_in, f, b in zip(items, q_ins, fs, bs)]
        vs = [v_ref[rows, :] for _, rows in items]
        o_intra = [_dot(a, v) for a, v in zip(a_s, vs)]
        u_ts = [_dot_tn(v, (1.0 - f) * jnp.exp(b_last - b)) for v, f, b, b_last in zip(vs, fs, bs, b_lasts)]
        for d in (0, 1):
            st_ref, out_s = dirs[d][5], dirs[d][6]
            st = st_ref[...]
            for k, (dk, rows) in enumerate(items):
                if dk == d:
                    out_s[rows, :] = o_intra[k] + _dot_nt(q_ins[k], st)
                    st = st * jnp.exp(b_lasts[k]) + u_ts[k]
            st_ref[...] = st
        return carry

    lax.fori_loop(0, n // group, body, 0)

    if context:
        sfin_ref[0, 0, 0] = sf_s[...].T
        sfin_ref[0, 1, 0] = sb_s[...].T

    nw = nw_ref[...]

    def epilogue(j, carry):
        rows = pl.ds(pl.multiple_of(j * TM, TM), TM)
        o = of_s[rows, :] + ob_s[rows, :]
        o_ref[rows, :] = _rms(o, nw) * _silu(g_ref[rows, :])
        return carry

    lax.fori_loop(0, seq_len // TM, epilogue, 0)


def _hgrn(proj, lb_logits, norm_w, layer, context, s0=None):
    seq_len = SEQ if context else DEC_SEQ
    bsz = BATCH if context else DEC_BATCH
    row0 = 0 if context else N_PROMPT // seq_len

    def col(k):
        return pl.BlockSpec((seq_len, LANE), lambda b, h: (row0 + b, k * H_A + h))

    in_specs = [col(0), col(1), col(2), col(3), col(4),
                pl.BlockSpec((2, DEPTH + 1, LANE), lambda b, h: (0, 0, h)),
                pl.BlockSpec((1, LANE), lambda b, h: (0, 0))]
    args = [proj, proj, proj, proj, proj, lb_logits, norm_w.reshape(1, LANE)]
    o_shape = jax.ShapeDtypeStruct((bsz * seq_len, W_A), F32)
    o_spec = pl.BlockSpec((seq_len, LANE), lambda b, h: (b, h))
    scratch = [pltpu.VMEM((seq_len, LANE), F32), pltpu.VMEM((seq_len, LANE), F32),
               pltpu.VMEM((LANE, LANE), F32), pltpu.VMEM((LANE, LANE), F32)]
    kern = functools.partial(_hgrn_kernel, seq_len=seq_len, layer=layer, context=context)
    if context:
        return pl.pallas_call(
            kern,
            out_shape=(o_shape, jax.ShapeDtypeStruct((bsz, 2, H_A, DK_A, DK_A), F32)),
            grid=(bsz, H_A),
            in_specs=in_specs,
            out_specs=(o_spec, pl.BlockSpec((1, 2, 1, DK_A, DK_A), lambda b, h: (b, 0, h, 0, 0))),
            scratch_shapes=scratch,
            compiler_params=_params(("parallel", "parallel")),
            name="hgrn_ctx",
        )(*args)
    in_specs += [pl.BlockSpec((1, 2, 1, DK_A, DK_A), lambda b, h: (b, 0, h, 0, 0))]
    args += [s0]
    return pl.pallas_call(
        kern,
        out_shape=o_shape,
        grid=(bsz, H_A),
        in_specs=in_specs,
        out_specs=o_spec,
        scratch_shapes=scratch,
        compiler_params=_params(("parallel", "parallel")),
        name="hgrn_dec",
    )(*args)


ATT_SUB = 256
ATT_TQ = 512
COL_QD = 5 * W_A // LANE
COL_KD = COL_QD + QK_B // LANE
COL_VD = COL_KD + QK_B // LANE


def _attn_kernel(*refs, seq_len, layer, context):
    if context:
        q_ref, k_ref, v_ref, lam_ref, nw_ref, o_ref, k_s, v_s = refs
    else:
        q_ref, k_ref, v_ref, ck_ref, cv_ref, lam_ref, nw_ref, o_ref, k_s, v_s = refs

    @pl.when(pl.program_id(2) == 0)
    def _():
        k_s[0:seq_len, :] = k_ref[...].astype(BF16)
        v_s[0:seq_len, :] = v_ref[...].astype(BF16)
        if not context:
            k_s[seq_len:, :] = ck_ref[0].astype(BF16)
            v_s[seq_len:, :] = cv_ref[0].astype(BF16)

    lam_init = 0.8 - 0.6 * math.exp(-0.3 * layer)
    lp = lam_ref[...]
    lam = (jnp.exp(jnp.sum(lp[0:1] * lp[1:2], axis=1, keepdims=True))
           - jnp.exp(jnp.sum(lp[2:3] * lp[3:4], axis=1, keepdims=True)) + lam_init)

    k = k_s[...]
    v = v_s[...]
    nw = nw_ref[...]
    lane = lax.broadcasted_iota(jnp.int32, (ATT_SUB, LANE), 1)
    for r0 in range(0, q_ref.shape[0], ATT_SUB):
        q = q_ref[r0:r0 + ATT_SUB, :] * (DH_B ** -0.5 * math.log2(math.e))
        s0 = _dot_nt(jnp.where(lane < DH_B, q, 0.0), k)
        s1 = _dot_nt(jnp.where(lane < DH_B, 0.0, q), k)
        p0 = jnp.exp2(s0 - jnp.max(s0, axis=1, keepdims=True))
        p1 = jnp.exp2(s1 - jnp.max(s1, axis=1, keepdims=True))
        l0 = jnp.sum(p0, axis=1, keepdims=True)
        l1 = jnp.sum(p1, axis=1, keepdims=True)
        a = p0 - (lam * l0 / l1) * p1
        o = jnp.dot(a.astype(BF16), v, preferred_element_type=F32) / l0
        o_ref[r0:r0 + ATT_SUB, :] = _rms(o, nw) * (1.0 - lam_init)


def _attn(proj, lam_p, norm_w, layer, context, cache_k=None, cache_v=None):
    seq_len = SEQ if context else DEC_SEQ
    bsz = BATCH if context else DEC_BATCH
    row0 = 0 if context else N_PROMPT // seq_len
    tq = min(ATT_TQ, seq_len)
    nq = seq_len // tq
    tile0 = row0 * nq
    t_k = seq_len if context else seq_len + PAST_LEN
    in_specs = [pl.BlockSpec((tq, LANE), lambda b, h, i: (tile0 + b * nq + i, COL_QD + h)),
                pl.BlockSpec((seq_len, LANE), lambda b, h, i: (row0 + b, COL_KD + h)),
                pl.BlockSpec((seq_len, LANE), lambda b, h, i: (row0 + b, COL_VD + h))]
    args = [proj, proj, proj]
    if not context:
        in_specs += [pl.BlockSpec((1, PAST_LEN, LANE), lambda b, h, i: (b, 0, h)),
                     pl.BlockSpec((1, PAST_LEN, LANE), lambda b, h, i: (b, 0, h))]
        args += [cache_k, cache_v]
    in_specs += [pl.BlockSpec((4, DH_B), lambda b, h, i: (0, 0)),
                 pl.BlockSpec((1, LANE), lambda b, h, i: (0, 0))]
    args += [lam_p, norm_w.reshape(1, LANE)]
    return pl.pallas_call(
        functools.partial(_attn_kernel, seq_len=seq_len, layer=layer, context=context),
        out_shape=jax.ShapeDtypeStruct((bsz * seq_len, W_B), F32),
        grid=(bsz, H_B, nq),
        in_specs=in_specs,
        out_specs=pl.BlockSpec((tq, LANE), lambda b, h, i: (b * nq + i, h)),
        scratch_shapes=[pltpu.VMEM((t_k, LANE), BF16), pltpu.VMEM((t_k, LANE), BF16)],
        compiler_params=_params(("parallel", "parallel", "arbitrary")),
        name="attn_ctx" if context else "attn_dec",
    )(*args)


COL_GATES = 4 * W_C // LANE
GATE_CHUNKS = 4
GDN_CTX_SEQS = 4


def _gdn_gates_kernel(g_ref, alog_ref, dtb_ref, col_ref, row_ref):
    c = GDN_CHUNK
    row = lax.broadcasted_iota(jnp.int32, (c, c), 0)
    col = lax.broadcasted_iota(jnp.int32, (c, c), 1)
    lane = lax.broadcasted_iota(jnp.int32, (c, LANE), 1)
    lower = (row >= col).astype(F32)
    upper = (row <= col).astype(F32)
    for j in range(GATE_CHUNKS):
        raw = g_ref[j * c:(j + 1) * c, :]
        z = raw + dtb_ref[...]
        softplus = jnp.maximum(z, 0.0) + jnp.log(1.0 + jnp.exp(-jnp.abs(z)))
        g = -jnp.exp(alog_ref[...]) * softplus
        out = jnp.where(lane < H_C, _tri_matmul(lower, g),
                        jnp.where(lane < 2 * H_C, _tri_matmul(upper, g), _sigmoid(raw)))
        col_ref[j * c:(j + 1) * c, :] = out
        row_ref[j] = out.T


def _gdn_gates(proj, alog_lane, dtb_lane):
    c = GDN_CHUNK
    n = N_TOK // c
    return pl.pallas_call(
        _gdn_gates_kernel,
        out_shape=(jax.ShapeDtypeStruct((N_TOK, LANE), F32), jax.ShapeDtypeStruct((n, LANE, c), F32)),
        grid=(n // GATE_CHUNKS,),
        in_specs=[pl.BlockSpec((GATE_CHUNKS * c, LANE), lambda i: (i, COL_GATES)),
                  pl.BlockSpec((1, LANE), lambda i: (0, 0)),
                  pl.BlockSpec((1, LANE), lambda i: (0, 0))],
        out_specs=(pl.BlockSpec((GATE_CHUNKS * c, LANE), lambda i: (i, 0)),
                   pl.BlockSpec((GATE_CHUNKS, LANE, c), lambda i: (i, 0, 0))),
        compiler_params=_params(("parallel",)),
        name="gdn_gates",
    )(proj, alog_lane, dtb_lane)


def _unit_tri_inverse_pairs(ms, row, col):
    c = ms[0].shape[0]

    def mm(xs, ys):
        out = []
        for x, y in zip(xs, ys):
            xb = x.astype(BF16)
            yb = y.astype(BF16)
            out.append(jnp.concatenate(
                [jnp.dot(xb[:, :c], yb[:, :c], preferred_element_type=F32),
                 jnp.dot(xb[:, c:], yb[:, c:], preferred_element_type=F32)], axis=1))
        return out

    def add(xs, ys):
        return [x + y for x, y in zip(xs, ys)]

    eye = (row == col).astype(F32)
    a = [jnp.where((row // 16) == (col // 16), m, 0.0) for m in ms]
    a2 = mm(a, a)
    a4 = mm(a2, a2)
    a8 = mm(a4, a4)
    t = [eye - x for x in a]
    t = add(t, mm(t, a2))
    t = add(t, mm(t, a4))
    t = add(t, mm(t, a8))
    blk = 32
    while blk <= c:
        off = ((row // blk) == (col // blk)) & ((row // (blk // 2)) != (col // (blk // 2)))
        corr = mm(mm(t, [jnp.where(off, m, 0.0) for m in ms]), t)
        t = [x - y for x, y in zip(t, corr)]
        blk *= 2
    return t


def _gdn_kernel(*refs, seq_len, n_seq, context):
    if context:
        (q_ref, k_ref, v_ref, go_ref, cwq_ref, cwk_ref, cwv_ref, gcol_ref, grow_ref, nw_ref,
         o_ref, sfin_ref, su_s, pr_s, of_s, ob_s, sf_s, sb_s) = refs
    else:
        (q_ref, k_ref, v_ref, go_ref, cwq_ref, cwk_ref, cwv_ref, gcol_ref, grow_ref, nw_ref, s0_ref,
         o_ref, su_s, pr_s, of_s, ob_s, sf_s, sb_s) = refs
    c = GDN_CHUNK
    cps = seq_len // c
    n = n_seq * cps
    head = pl.program_id(1)
    row = lax.broadcasted_iota(jnp.int32, (c, c), 0)
    col = lax.broadcasted_iota(jnp.int32, (c, c), 1)
    row2 = lax.broadcasted_iota(jnp.int32, (c, 2 * c), 0)
    col2 = lax.broadcasted_iota(jnp.int32, (c, 2 * c), 1) & (c - 1)
    lane = lax.broadcasted_iota(jnp.int32, (c, LANE), 1)
    rowi = lax.broadcasted_iota(jnp.int32, (c, LANE), 0)

    def gc_row_last(ci, d):
        gc_row = grow_ref[ci, pl.ds(d * H_C + head, 1), :]
        return gc_row, (gc_row[:, c - 1:c] if d == 0 else gc_row[:, 0:1])

    def chunk_inputs(ci):
        r0 = pl.multiple_of(ci * c, c)
        rows = pl.ds(r0, c)

        def conv(x_ref, w_ref):
            cur = x_ref[rows, :]
            before = x_ref[pl.ds(pl.multiple_of(jnp.maximum(r0 - 8, 0), 8), 8), :]
            after = x_ref[pl.ds(pl.multiple_of(jnp.minimum(r0 + c, n * c - 8), 8), 8), :]
            in_seq = lax.rem(ci, cps)
            prev_row = jnp.where(in_seq > 0, before[7:8, :], 0.0)
            next_row = jnp.where(in_seq < cps - 1, after[0:1, :], 0.0)
            xm1 = jnp.where(rowi == 0, prev_row, pltpu.roll(cur, 1, 0))
            xp1 = jnp.where(rowi == c - 1, next_row, pltpu.roll(cur, c - 1, 0))
            w = w_ref[...]
            return _silu(xm1 * w[0:1, :] + cur * w[1:2, :] + xp1 * w[2:3, :])

        q = conv(q_ref, cwq_ref)
        k = conv(k_ref, cwk_ref)
        vn = conv(v_ref, cwv_ref)
        qn = q * lax.rsqrt(jnp.sum(q * q, axis=1, keepdims=True) + EPS) * (DK_C ** -0.5)
        kn = k * lax.rsqrt(jnp.sum(k * k, axis=1, keepdims=True) + EPS)
        kq = _dot_nt(jnp.concatenate([kn, qn], axis=0), kn)
        gates = gcol_ref[rows, :]
        per_dir = []
        ms = []
        for d in (0, 1):
            gc = jnp.sum(jnp.where(lane == d * H_C + head, gates, 0.0), axis=1, keepdims=True)
            beta = jnp.sum(jnp.where(lane == (2 + d) * H_C + head, gates, 0.0), axis=1, keepdims=True)
            gc_row, gc_last = gc_row_last(ci, d)
            incl = (row >= col) if d == 0 else (row <= col)
            strict = (row > col) if d == 0 else (row < col)
            decay = jnp.where(incl, jnp.exp(gc - gc_row), 0.0)
            m = jnp.where(strict, kq[:c] * beta * decay, 0.0)
            per_dir.append((gc, beta, kq[c:] * decay, (kn * jnp.exp(gc_last - gc)).T))
            ms.append(m)
        return jnp.concatenate(ms, axis=1), (rows, qn, kn, vn, per_dir)

    def chunk_outputs(ci, t, rest):
        rows, qn, kn, vn, per_dir = rest
        base = pl.multiple_of(ci * 2 * c, 2 * c)
        for d in (0, 1):
            gc, beta, qk, kd_t = per_dir[d]
            e = jnp.exp(gc)
            uw = _dot(t[:, d * c:(d + 1) * c], jnp.concatenate([vn * beta, kn * (beta * e)], axis=1))
            cross = _dot(jnp.concatenate([kd_t, qk], axis=0), uw)
            su_s[d, rows, :] = cross[:c, :c]
            (of_s, ob_s)[d][rows, :] = cross[c:, :c]
            pr_s[d, pl.ds(base, c), :] = cross[:c, c:].astype(BF16)
            pr_s[d, pl.ds(base + c, c), :] = (qn * e - cross[c:, c:]).astype(BF16)

    group = min(8, n)

    def prepare(j, carry):
        cis = [j * group + g for g in range(group)]
        staged = [chunk_inputs(ci) for ci in cis]
        ts = _unit_tri_inverse_pairs([s[0] for s in staged], row2, col2)
        for ci, t, s in zip(cis, ts, staged):
            chunk_outputs(ci, t, s[1])
        return carry

    lax.fori_loop(0, n // group, prepare, 0)

    def advance(ci, d, s_ref, out_s):
        rows = pl.ds(pl.multiple_of(ci * c, c), c)
        _, gc_last = gc_row_last(ci, d)
        s = s_ref[...]
        ps = jnp.dot(pr_s[d, pl.ds(pl.multiple_of(ci * 2 * c, 2 * c), 2 * c), :], s.astype(BF16),
                     preferred_element_type=F32)
        out_s[rows, :] = out_s[rows, :] + ps[c:]
        s_ref[...] = s * jnp.exp(gc_last) - ps[:c] + su_s[d, rows, :]

    for s in range(n_seq):
        if context:
            sf_s[...] = jnp.zeros_like(sf_s)
            sb_s[...] = jnp.zeros_like(sb_s)
        else:
            sf_s[...] = s0_ref[s, 0, 0]
            sb_s[...] = s0_ref[s, 1, 0]

        def body(i, carry, first=s * cps):
            advance(first + i, 0, sf_s, of_s)
            advance(first + cps - 1 - i, 1, sb_s, ob_s)
            return carry

        lax.fori_loop(0, cps, body, 0)

        if context:
            sfin_ref[s, 0, 0] = sf_s[...]
            sfin_ref[s, 1, 0] = sb_s[...]

    nw = nw_ref[...]

    def epilogue(j, carry):
        rows = pl.ds(pl.multiple_of(j * TM, TM), TM)
        o = of_s[rows, :] + ob_s[rows, :]
        o_ref[rows, :] = _rms(o, nw) * _silu(go_ref[rows, :])
        return carry

    lax.fori_loop(0, n * c // TM, epilogue, 0)


def _gdn(proj, conv_w, gcol, grow, norm_w, context, s0=None):
    seq_len = SEQ if context else DEC_SEQ
    n_seq = GDN_CTX_SEQS if context else 1
    n_blk = (BATCH if context else DEC_BATCH) // n_seq
    rows = n_seq * seq_len
    row0 = 0 if context else N_PROMPT // rows
    nc = rows // GDN_CHUNK

    def col(k):
        return pl.BlockSpec((rows, LANE), lambda b, h: (row0 + b, k * H_C + h))

    def cw(k):
        return pl.BlockSpec((3, LANE), lambda b, h: (0, k * H_C + h))

    in_specs = [col(0), col(1), col(2), col(3), cw(0), cw(1), cw(2),
                pl.BlockSpec((rows, LANE), lambda b, h: (row0 + b, 0)),
                pl.BlockSpec((nc, LANE, GDN_CHUNK), lambda b, h: (row0 + b, 0, 0)),
                pl.BlockSpec((1, LANE), lambda b, h: (0, 0))]
    args = [proj, proj, proj, proj, conv_w, conv_w, conv_w, gcol, grow, norm_w.reshape(1, LANE)]
    o_shape = jax.ShapeDtypeStruct((n_blk * rows, W_C), F32)
    o_spec = pl.BlockSpec((rows, LANE), lambda b, h: (b, h))
    seq_buf = pltpu.VMEM((rows, LANE), F32)
    state_buf = pltpu.VMEM((DK_C, DK_C), F32)
    scratch = [pltpu.VMEM((2, rows, LANE), F32), pltpu.VMEM((2, 2 * rows, LANE), BF16),
               seq_buf, seq_buf, state_buf, state_buf]
    kern = functools.partial(_gdn_kernel, seq_len=seq_len, n_seq=n_seq, context=context)
    state_spec = pl.BlockSpec((n_seq, 2, 1, DK_C, DK_C), lambda b, h: (b, 0, h, 0, 0))
    if context:
        return pl.pallas_call(
            kern,
            out_shape=(o_shape, jax.ShapeDtypeStruct((BATCH, 2, H_C, DK_C, DK_C), F32)),
            grid=(n_blk, H_C),
            in_specs=in_specs,
            out_specs=(o_spec, state_spec),
            scratch_shapes=scratch,
            compiler_params=_params(("parallel", "parallel")),
            name="gdn_ctx",
        )(*args)
    in_specs += [state_spec]
    args += [s0]
    return pl.pallas_call(
        kern,
        out_shape=o_shape,
        grid=(n_blk, H_C),
        in_specs=in_specs,
        out_specs=o_spec,
        scratch_shapes=scratch,
        compiler_params=_params(("parallel", "parallel")),
        name="gdn_dec",
    )(*args)


def kernel(x_prompt, x_sample, c, cache_diff_k, cache_diff_v, state_hgrn, state_gdn, c_ctx,
           ada_w, ada_b, norm_w, final_norm_w, w_in_ab, hgrn_lb_logits, hgrn_norm_w,
           diff_lambda, diff_norm_w, w_in_c, gdn_conv_w, gdn_a_log, gdn_dt_bias, gdn_norm_w,
           w_out, moe_router_group, moe_router_expert, moe_w_gate, moe_w_up, moe_w_down):
    x = (x_prompt.reshape(N_PROMPT, D_MODEL), x_sample.reshape(N_SAMPLE, D_MODEL))
    cvec = jnp.concatenate([c_ctx[None, :], c, jnp.zeros((MOD_ROWS - N_MOD, D_MODEL), F32)], axis=0)
    mod_all = _adaln(cvec, ada_w, ada_b)[:, :N_MOD].reshape(DEPTH, N_MOD, 6, D_MODEL)
    mod_all = jnp.pad(mod_all, ((0, 0), (0, 0), (0, MOD_ROWS - 6), (0, 0)))
    cos, sin = _rope_tables()
    new_k, new_v, new_hgrn, new_gdn = [], [], [], []
    for l in range(DEPTH):
        i = l // 2
        mod = mod_all[l]
        if l % 2 == 0:
            proj = _inproj(x, mod, norm_w[l, 0], w_in_ab[i].astype(BF16),
                           rope=(cos, sin, (COL_QD * LANE, COL_VD * LANE)))
            oh_ctx, s_h = _hgrn(proj, hgrn_lb_logits, hgrn_norm_w[i], l, True)
            oh_dec = _hgrn(proj, hgrn_lb_logits, hgrn_norm_w[i], l, False, s0=state_hgrn[:, i])
            od_ctx = _attn(proj, diff_lambda[i], diff_norm_w[i], l, True)
            od_dec = _attn(proj, diff_lambda[i], diff_norm_w[i], l, False,
                           cache_k=cache_diff_k[:, i].reshape(DEC_BATCH, PAST_LEN, QK_B),
                           cache_v=cache_diff_v[:, i].reshape(DEC_BATCH, PAST_LEN, W_B))
            parts = ((oh_ctx, oh_dec), (od_ctx, od_dec))
            new_k.append(proj[:N_PROMPT, COL_KD * LANE:COL_VD * LANE].reshape(BATCH, SEQ, H_B, 2, DH_B))
            new_v.append(proj[:N_PROMPT, COL_VD * LANE:].reshape(BATCH, SEQ, H_B, DV_B))
            new_hgrn.append(s_h)
        else:
            w_c = jnp.pad(w_in_c[i], ((0, 0), (0, (COL_GATES + 1) * LANE - IN_C))).astype(BF16)
            proj = _inproj(x, mod, norm_w[l, 0], w_c)
            pad = jnp.zeros((LANE - 2 * H_C,), F32)
            alog_lane = jnp.concatenate([gdn_a_log[i, 0], gdn_a_log[i, 1], pad]).reshape(1, LANE)
            dtb_lane = jnp.concatenate([gdn_dt_bias[i, 0], gdn_dt_bias[i, 1], pad]).reshape(1, LANE)
            gcol, grow = _gdn_gates(proj, alog_lane, dtb_lane)
            oc_ctx, s_c = _gdn(proj, gdn_conv_w[i], gcol, grow, gdn_norm_w[i], True)
            oc_dec = _gdn(proj, gdn_conv_w[i], gcol, grow, gdn_norm_w[i], False, s0=state_gdn[:, i])
            parts = ((oc_ctx, oc_dec),)
            new_gdn.append(s_c)
        w_router = jnp.concatenate(
            [moe_router_group[l], moe_router_expert[l],
             jnp.zeros((D_MODEL, LANE - N_GROUPS - N_EXPERTS), F32)], axis=1)
        w_router_hi = w_router.astype(BF16)
        w_router = jnp.concatenate([w_router_hi, (w_router - w_router_hi.astype(F32)).astype(BF16)], axis=1)
        x, h2, route = _outproj(parts, w_out[l].astype(BF16), x, mod, norm_w[l, 1], w_router)
        x = _moe(x, h2, route, mod, moe_w_gate, moe_w_up, moe_w_down, l, final_norm_w)
    y_prompt = x[0].reshape(BATCH, SEQ, D_MODEL)
    y_sample = x[1].reshape(DEC_BATCH, DEC_SEQ, D_MODEL)
    return (y_prompt, y_sample, jnp.stack(new_k, axis=1), jnp.stack(new_v, axis=1),
            jnp.stack(new_hgrn, axis=1), jnp.stack(new_gdn, axis=1))
```

```python
import functools
import math

import jax
import jax.numpy as jnp
from jax import lax
from jax.experimental import pallas as pl
from jax.experimental.pallas import tpu as pltpu

F32 = jnp.float32
BF16 = jnp.bfloat16
HIGHEST = lax.Precision.HIGHEST

D_MODEL = 1024
BATCH = 16
SEQ = 256
DEPTH = 2
DEC_BATCH = 4
DEC_SEQ = 4096
PAST_LEN = 256
GRID_W = 64
H_A = 4
DK_A = 128
W_A = 512
CHUNK_A = 32
H_B = 4
DH_B = 64
DV_B = 128
QK_B = 512
W_B = 512
ROPE_BASE = 10000.0
H_C = 8
DK_C = 128
W_C = 1024
N_GROUPS = 4
E_PER_GROUP = 8
N_EXPERTS = 32
D_EXPERT = 512
EPS = 1e-6
IN_AB = 5 * W_A + 2 * QK_B + W_B
IN_C = 4 * W_C + 4 * H_C

LANE = 128
N_PROMPT = BATCH * SEQ
N_SAMPLE = DEC_BATCH * DEC_SEQ
N_TOK = N_PROMPT + N_SAMPLE
TM = 256
N_TILES = N_TOK // TM
PROMPT_TILES = N_PROMPT // TM
TILES_PER_SAMPLE = DEC_SEQ // TM
N_MOD = 1 + DEC_BATCH
MOD_ROWS = 8
MOE_ROWS = 256
MOE_AHEAD = 2
ROW_TILE = D_MODEL // LANE
GDN_CHUNK = 128
VMEM_LIMIT = 56 * 1024 * 1024


def _mod_index(t):
    return jnp.where(t < PROMPT_TILES, 0, 1 + (t - PROMPT_TILES) // TILES_PER_SAMPLE)


def _sigmoid(x):
    return 1.0 / (1.0 + jnp.exp(-x))


def _silu(x):
    return x * _sigmoid(x)


def _rms(x, w):
    return x * lax.rsqrt(jnp.mean(x * x, axis=-1, keepdims=True) + EPS) * w


def _dot(a, b):
    return jnp.dot(a.astype(BF16), b.astype(BF16), preferred_element_type=F32)


def _dot_nt(a, b):
    return lax.dot_general(a.astype(BF16), b.astype(BF16), (((1,), (1,)), ((), ())),
                           preferred_element_type=F32)


def _dot_tn(a, b):
    return lax.dot_general(a.astype(BF16), b.astype(BF16), (((0,), (0,)), ((), ())),
                           preferred_element_type=F32)


def _dot_f32(a, b):
    return jnp.dot(a, b, precision=HIGHEST, preferred_element_type=F32)


def _tri_matmul(tri, x):
    hi = x.astype(BF16)
    rem = x - hi.astype(F32)
    mid = rem.astype(BF16)
    lo = (rem - mid.astype(F32)).astype(BF16)
    n = x.shape[1]
    r = jnp.dot(tri.astype(BF16), jnp.concatenate([hi, mid, lo], axis=1), preferred_element_type=F32)
    return r[:, :n] + r[:, n:2 * n] + r[:, 2 * n:]


def _params(sem):
    return pltpu.CompilerParams(dimension_semantics=sem, vmem_limit_bytes=VMEM_LIMIT)


def _adaln_kernel(c_ref, w_ref, b_ref, o_ref):
    s = _silu(c_ref[...])
    o_ref[0] = _dot(s, w_ref[0]) + b_ref[0]


def _adaln(cvec, ada_w, ada_b):
    nb = 4
    wb = 6 * D_MODEL // nb
    return pl.pallas_call(
        _adaln_kernel,
        out_shape=jax.ShapeDtypeStruct((DEPTH, MOD_ROWS, 6 * D_MODEL), F32),
        grid=(DEPTH, nb),
        in_specs=[pl.BlockSpec((MOD_ROWS, D_MODEL), lambda l, j: (0, 0)),
                  pl.BlockSpec((1, D_MODEL, wb), lambda l, j: (l, 0, j)),
                  pl.BlockSpec((1, 1, wb), lambda l, j: (l, 0, j))],
        out_specs=pl.BlockSpec((1, MOD_ROWS, wb), lambda l, j: (l, 0, j)),
        compiler_params=_params(("parallel", "parallel")),
        name="adaln",
    )(cvec, ada_w, ada_b.reshape(DEPTH, 1, 6 * D_MODEL))


def _row_specs(x, width):
    if isinstance(x, tuple):
        return ([pl.BlockSpec((TM, width), lambda t: (jnp.minimum(t, PROMPT_TILES - 1), 0)),
                 pl.BlockSpec((TM, width), lambda t: (jnp.maximum(t - PROMPT_TILES, 0), 0))], list(x))
    return [pl.BlockSpec((TM, width), lambda t: (t, 0))], [x]


def _read_rows(refs):
    if len(refs) == 2:
        return jnp.where(pl.program_id(0) < PROMPT_TILES, refs[0][...], refs[1][...])
    return refs[0][...]


def _inproj_kernel(*refs, n_x, rope_cols):
    x = _read_rows(refs[:n_x])
    if rope_cols is None:
        mod_ref, nw_ref, w_ref, o_ref = refs[n_x:]
    else:
        mod_ref, nw_ref, w_ref, cos_ref, sin_ref, o_ref = refs[n_x:]
    h = _rms(x, nw_ref[...]) * (1.0 + mod_ref[0, 1:2, :]) + mod_ref[0, 0:1, :]
    r = jnp.dot(h.astype(BF16), w_ref[...], preferred_element_type=F32)
    if rope_cols is None:
        o_ref[...] = r
        return
    lo, hi = rope_cols
    o_ref[:, :lo] = r[:, :lo]
    o_ref[:, hi:] = r[:, hi:]
    cos = cos_ref[...]
    sin = sin_ref[...]
    lane = lax.broadcasted_iota(jnp.int32, (TM, LANE), 1)
    upper = (lane & 16) != 0
    for c0 in range(lo, hi, LANE):
        v = r[:, c0:c0 + LANE]
        partner = jnp.where(upper, pltpu.roll(v, 16, 1), pltpu.roll(v, LANE - 16, 1))
        o_ref[:, c0:c0 + LANE] = v * cos + partner * sin


def _inproj(x, mod, norm_w, w_bf16, rope=None):
    p = w_bf16.shape[1]
    in_specs, args = _row_specs(x, D_MODEL)
    n_x = len(args)
    in_specs += [pl.BlockSpec((1, MOD_ROWS, D_MODEL), lambda t: (_mod_index(t), 0, 0)),
                 pl.BlockSpec((1, D_MODEL), lambda t: (0, 0)),
                 pl.BlockSpec((D_MODEL, p), lambda t: (0, 0))]
    args += [mod, norm_w.reshape(1, D_MODEL), w_bf16]
    rope_cols = None
    if rope is not None:
        cos, sin, rope_cols = rope

        def rope_index(t):
            return (jnp.where(t < PROMPT_TILES, 0, 1 + (t - PROMPT_TILES) % TILES_PER_SAMPLE), 0)

        in_specs += [pl.BlockSpec((TM, LANE), rope_index), pl.BlockSpec((TM, LANE), rope_index)]
        args += [cos, sin]
    return pl.pallas_call(
        functools.partial(_inproj_kernel, n_x=n_x, rope_cols=rope_cols),
        out_shape=jax.ShapeDtypeStruct((N_TOK, p), F32),
        grid=(N_TILES,),
        in_specs=in_specs,
        out_specs=pl.BlockSpec((TM, p), lambda t: (t, 0)),
        compiler_params=_params(("parallel",)),
        name="inproj",
    )(*args)


def _rope_tables():
    lane = jnp.arange(LANE)
    d = lane % DH_B
    use_col = (d // 32) == 1
    j = d % 16
    upper = ((d % 32) // 16) == 1
    inv_freq = ROPE_BASE ** (-j.astype(F32) / 16.0)
    t = jnp.arange(DEC_SEQ)
    row = (t // GRID_W).astype(F32)
    col = (t % GRID_W).astype(F32)
    pos = jnp.where(use_col[None, :], col[:, None], row[:, None])
    ang = pos * inv_freq[None, :]
    cos = jnp.cos(ang)
    sin = jnp.where(upper[None, :], jnp.sin(ang), -jnp.sin(ang))
    cos = jnp.concatenate([jnp.ones((TM, LANE), F32), cos], axis=0)
    sin = jnp.concatenate([jnp.zeros((TM, LANE), F32), sin], axis=0)
    return cos, sin


def _route(logits):
    lane = lax.broadcasted_iota(jnp.int32, logits.shape, 1)
    lanef = lane.astype(F32)
    neg = jnp.float32(-jnp.inf)
    gl = jnp.where(lane < N_GROUPS, logits, neg)
    gmax = jnp.max(gl, axis=1, keepdims=True)
    gsel = jnp.min(jnp.where(gl == gmax, lanef, float(LANE)), axis=1, keepdims=True)
    p_grp = 1.0 / jnp.sum(jnp.exp(gl - gmax), axis=1, keepdims=True)
    lo = float(N_GROUPS) + gsel * float(E_PER_GROUP)
    el = jnp.where((lanef >= lo) & (lanef < lo + float(E_PER_GROUP)), logits, neg)
    v1 = jnp.max(el, axis=1, keepdims=True)
    i1 = jnp.min(jnp.where(el == v1, lanef, float(LANE)), axis=1, keepdims=True)
    el2 = jnp.where(lanef == i1, neg, el)
    v2 = jnp.max(el2, axis=1, keepdims=True)
    i2 = jnp.min(jnp.where(el2 == v2, lanef, float(LANE)), axis=1, keepdims=True)
    t = jnp.exp(v2 - v1)
    w1 = p_grp / (1.0 + t)
    w2 = p_grp * t / (1.0 + t)
    out = jnp.where(lane == 0, i1 - float(N_GROUPS), 0.0)
    out = jnp.where(lane == 1, i2 - float(N_GROUPS), out)
    out = jnp.where(lane == 2, w1, out)
    out = jnp.where(lane == 3, w2, out)
    return out


def _outproj_kernel(*refs, widths, n_x):
    n_in = len(widths)
    x_refs = refs[2 * n_in:2 * n_in + n_x]
    w_ref, mod_ref, nw_ref, wr_ref, xn_ref, h2_ref, rt_ref = refs[2 * n_in + n_x:]
    y = None
    c0 = 0
    for k, wd in enumerate(widths):
        o = _read_rows(refs[2 * k:2 * k + 2])
        part = jnp.dot(o.astype(BF16), w_ref[c0:c0 + wd, :], preferred_element_type=F32)
        y = part if y is None else y + part
        c0 += wd
    xn = _read_rows(x_refs) + mod_ref[0, 2:3, :] * y
    xn_ref[...] = xn
    h2 = _rms(xn, nw_ref[...]) * (1.0 + mod_ref[0, 4:5, :]) + mod_ref[0, 3:4, :]
    for s in range(ROW_TILE):
        h2_ref[pl.ds(s, TM, stride=ROW_TILE), :] = h2[:, s * LANE:(s + 1) * LANE]
    h_hi = h2.astype(BF16)
    h_lo = (h2 - h_hi.astype(F32)).astype(BF16)
    wr = wr_ref[...]
    hw = jnp.dot(h_hi, wr, preferred_element_type=F32)
    logits = hw[:, :LANE] + hw[:, LANE:] + jnp.dot(h_lo, wr[:, :LANE], preferred_element_type=F32)
    rt_ref[...] = _route(logits)


def _outproj(parts, w_bf16, x, mod, norm_w2, w_router):
    widths = tuple(p[0].shape[1] for p in parts)
    in_specs, args = [], []
    for p, wd in zip(parts, widths):
        specs, ops = _row_specs(p, wd)
        in_specs += specs
        args += ops
    specs, ops = _row_specs(x, D_MODEL)
    in_specs += specs
    args += ops
    in_specs += [pl.BlockSpec((D_MODEL, D_MODEL), lambda t: (0, 0)),
                 pl.BlockSpec((1, MOD_ROWS, D_MODEL), lambda t: (_mod_index(t), 0, 0)),
                 pl.BlockSpec((1, D_MODEL), lambda t: (0, 0)),
                 pl.BlockSpec((D_MODEL, 2 * LANE), lambda t: (0, 0))]
    args += [w_bf16, mod, norm_w2.reshape(1, D_MODEL), w_router]
    return pl.pallas_call(
        functools.partial(_outproj_kernel, widths=widths, n_x=len(ops)),
        out_shape=(jax.ShapeDtypeStruct((N_TOK, D_MODEL), F32),
                   jax.ShapeDtypeStruct((N_TOK * ROW_TILE, LANE), F32),
                   jax.ShapeDtypeStruct((N_TOK, LANE), F32)),
        grid=(N_TILES,),
        in_specs=in_specs,
        out_specs=(pl.BlockSpec((TM, D_MODEL), lambda t: (t, 0)),
                   pl.BlockSpec((TM * ROW_TILE, LANE), lambda t: (t, 0)),
                   pl.BlockSpec((TM, LANE), lambda t: (t, 0))),
        compiler_params=_params(("parallel",)),
        name="outproj",
    )(*args)


def _expert_kernel(blk_e_ref, nact_ref, rowtok_ref, h2_hbm, wg_ref, wu_ref, wd_ref, y_ref,
                   x_0, x_1, x_2, sems, wg_s, wu_s, wd_s):
    i = pl.program_id(0)
    n_act = nact_ref[0]
    e = blk_e_ref[i]
    prev = blk_e_ref[jnp.maximum(i - 1, 0)]
    bufs = (x_0, x_1, x_2)
    ring = MOE_AHEAD + 1
    assert ring == len(bufs)

    def row_copy(blk, r, slot):
        tok = rowtok_ref[blk * MOE_ROWS + r]
        return pltpu.make_async_copy(
            h2_hbm.at[pl.ds(pl.multiple_of(tok * ROW_TILE, ROW_TILE), ROW_TILE), :],
            bufs[slot].at[pl.ds(pl.multiple_of(r * ROW_TILE, ROW_TILE), ROW_TILE), :],
            sems.at[slot])

    def wait_block(slot):
        pltpu.make_async_copy(h2_hbm.at[pl.ds(0, MOE_ROWS * ROW_TILE), :], bufs[slot], sems.at[slot]).wait()

    @pl.when(i == 0)
    def _():
        for blk in range(MOE_AHEAD):
            def start_row(r, carry):
                row_copy(blk, r, blk).start()
                return carry
            lax.fori_loop(0, MOE_ROWS, start_row, 0)

    @pl.when((i == 0) | (e != prev))
    def _():
        wg_s[...] = wg_ref[...].astype(BF16)
        wu_s[...] = wu_ref[...].astype(BF16)
        wd_s[...] = wd_ref[...].astype(BF16)

    for slot in range(ring):
        @pl.when((i >= n_act) & (i < n_act + MOE_AHEAD) & (i % ring == slot))
        def _():
            wait_block(slot)

        @pl.when((i < n_act) & (i % ring == slot))
        def _():
            wait_block(slot)
            for r in range(MOE_ROWS):
                row_copy(i + MOE_AHEAD, r, (slot + MOE_AHEAD) % ring).start(priority=r % 2)
            x = jnp.concatenate([bufs[slot][pl.ds(s, MOE_ROWS, stride=ROW_TILE), :] for s in range(ROW_TILE)],
                                axis=1).astype(BF16)
            g = jnp.dot(x, wg_s[...], preferred_element_type=F32)
            u = jnp.dot(x, wu_s[...], preferred_element_type=F32)
            y_ref[...] = jnp.dot((_silu(g) * u).astype(BF16), wd_s[...], preferred_element_type=F32)

    @pl.when(i >= nact_ref[0])
    def _():
        y_ref[...] = jnp.zeros_like(y_ref)


def _experts(h2_tiles, row_tok, blk_e, n_active, w_gate, w_up, w_down, layer):
    n_rows = row_tok.shape[0]
    n_blocks = n_rows // MOE_ROWS
    grid_spec = pltpu.PrefetchScalarGridSpec(
        num_scalar_prefetch=3,
        grid=(n_blocks,),
        in_specs=[pl.BlockSpec(memory_space=pl.ANY),
                  pl.BlockSpec((None, None, D_MODEL, D_EXPERT), lambda i, be, na, rt: (layer, be[i], 0, 0)),
                  pl.BlockSpec((None, None, D_MODEL, D_EXPERT), lambda i, be, na, rt: (layer, be[i], 0, 0)),
                  pl.BlockSpec((None, None, D_EXPERT, D_MODEL), lambda i, be, na, rt: (layer, be[i], 0, 0))],
        out_specs=pl.BlockSpec((MOE_ROWS, D_MODEL), lambda i, be, na, rt: (i, 0)),
        scratch_shapes=[pltpu.VMEM((MOE_ROWS * ROW_TILE, LANE), F32),
                        pltpu.VMEM((MOE_ROWS * ROW_TILE, LANE), F32),
                        pltpu.VMEM((MOE_ROWS * ROW_TILE, LANE), F32),
                        pltpu.SemaphoreType.DMA((MOE_AHEAD + 1,)),
                        pltpu.VMEM((D_MODEL, D_EXPERT), BF16),
                        pltpu.VMEM((D_MODEL, D_EXPERT), BF16),
                        pltpu.VMEM((D_EXPERT, D_MODEL), BF16)])
    return pl.pallas_call(
        _expert_kernel,
        out_shape=jax.ShapeDtypeStruct((n_rows, D_MODEL), F32),
        grid_spec=grid_spec,
        compiler_params=_params(("arbitrary",)),
        name="experts",
    )(blk_e, n_active, row_tok, h2_tiles, w_gate, w_up, w_down)


def _combine_kernel(x_ref, y0_ref, y1_ref, rt_ref, mod_ref, fw_ref, o_ref, *, final):
    rt = rt_ref[...]
    y = rt[:, 2:3] * y0_ref[...] + rt[:, 3:4] * y1_ref[...]
    x = x_ref[...] + mod_ref[0, 5:6, :] * y
    o_ref[...] = _rms(x, fw_ref[...]) if final else x


def _combine(x, y0, y1, route, mod, final_w, final=False, tile0=0, n_rows=N_TOK):
    row = pl.BlockSpec((TM, D_MODEL), lambda t: (t + tile0, 0))
    return pl.pallas_call(
        functools.partial(_combine_kernel, final=final),
        out_shape=jax.ShapeDtypeStruct((n_rows, D_MODEL), F32),
        grid=(n_rows // TM,),
        in_specs=[row, row, row,
                  pl.BlockSpec((TM, LANE), lambda t: (t + tile0, 0)),
                  pl.BlockSpec((1, MOD_ROWS, D_MODEL), lambda t: (_mod_index(t + tile0), 0, 0)),
                  pl.BlockSpec((1, D_MODEL), lambda t: (0, 0))],
        out_specs=pl.BlockSpec((TM, D_MODEL), lambda t: (t, 0)),
        compiler_params=_params(("parallel",)),
        name="combine",
    )(x, y0, y1, route, mod, final_w.reshape(1, D_MODEL))


def _take_rows(a, idx):
    return a.at[idx].get(mode="promise_in_bounds")


def _moe(x, h2, route, mod, w_gate, w_up, w_down, layer, final_w):
    n_asg = 2 * N_TOK
    flat_e = route[:, :2].astype(jnp.int32).reshape(n_asg)
    onehot = (flat_e[:, None] == jnp.arange(N_EXPERTS, dtype=jnp.int32)[None, :]).astype(jnp.int32)
    csum = jnp.cumsum(onehot, axis=0)
    rank = jnp.take_along_axis(csum, flat_e[:, None], axis=1)[:, 0] - 1
    counts = csum[-1]
    padded = (counts + MOE_ROWS - 1) // MOE_ROWS * MOE_ROWS
    pad_end = jnp.cumsum(padded)
    pad_start = pad_end - padded
    dest = pad_start[flat_e] + rank
    n_rows = n_asg + (N_EXPERTS + MOE_AHEAD) * MOE_ROWS
    n_blocks = n_rows // MOE_ROWS
    blk_start = jnp.arange(n_blocks, dtype=jnp.int32) * MOE_ROWS
    blk_e = jnp.sum((blk_start[:, None] >= pad_end[None, :]).astype(jnp.int32), axis=1)
    blk_e = jnp.minimum(blk_e, N_EXPERTS - 1)
    order = jnp.argsort(flat_e, stable=True).astype(jnp.int32)
    row_e = jnp.repeat(blk_e, MOE_ROWS)
    start = jnp.cumsum(counts) - counts
    src = jnp.arange(n_rows, dtype=jnp.int32) + (start - pad_start)[row_e]
    row_tok = order.at[jnp.clip(src, 0, n_asg - 1)].get(mode="promise_in_bounds") // 2
    n_active = (pad_end[-1:] // MOE_ROWS).astype(jnp.int32)
    yb = _experts(h2, row_tok, blk_e, n_active, w_gate, w_up, w_down, layer)
    dest2 = dest.reshape(N_TOK, 2)
    y0 = _take_rows(yb, dest2[:, 0])
    y1 = _take_rows(yb, dest2[:, 1])
    if layer < DEPTH - 1:
        return _combine(x, y0, y1, route, mod, final_w)
    return (_combine(x, y0, y1, route, mod, final_w, final=True, n_rows=N_PROMPT),
            _combine(x, y0, y1, route, mod, final_w, final=True, tile0=PROMPT_TILES, n_rows=N_SAMPLE))


def _hgrn_kernel(*refs, seq_len, layer, context):
    if context:
        (q_ref, ff_ref, fb_ref, v_ref, g_ref, lbl_ref, nw_ref,
         o_ref, sfin_ref, of_s, ob_s, sf_s, sb_s) = refs
    else:
        (q_ref, ff_ref, fb_ref, v_ref, g_ref, lbl_ref, nw_ref, s0_ref,
         o_ref, of_s, ob_s, sf_s, sb_s) = refs
    c = CHUNK_A
    n = seq_len // c

    def lower_bound(d):
        z = lbl_ref[d]
        e = jnp.exp(z - jnp.max(z, axis=0, keepdims=True))
        return jnp.sum(e[:layer + 1], axis=0, keepdims=True) / jnp.sum(e, axis=0, keepdims=True)

    lb_f = lower_bound(0)
    lb_b = lower_bound(1)
    row = lax.broadcasted_iota(jnp.int32, (c, c), 0)
    col = lax.broadcasted_iota(jnp.int32, (c, c), 1)
    causal = row >= col
    tri_f = causal.astype(F32)
    tri_b = (row <= col).astype(F32)

    if context:
        sf_s[...] = jnp.zeros_like(sf_s)
        sb_s[...] = jnp.zeros_like(sb_s)
    else:
        sf_s[...] = s0_ref[0, 0, 0].T
        sb_s[...] = s0_ref[0, 1, 0].T

    group = min(16, n)
    dirs = ((ff_ref, lb_f, tri_f, causal, c - 1, sf_s, of_s),
            (fb_ref, lb_b, tri_b, row <= col, 0, sb_s, ob_s))

    def body(i, carry):
        items = []
        for g in range(group):
            j = i * group + g
            items.append((0, pl.ds(pl.multiple_of(j * c, c), c)))
            items.append((1, pl.ds(pl.multiple_of((n - 1 - j) * c, c), c)))
        fs = [dirs[d][1] + (1.0 - dirs[d][1]) * _sigmoid(dirs[d][0][rows, :]) for d, rows in items]
        bs = [_tri_matmul(dirs[d][2], jnp.log(f)) for (d, _), f in zip(items, fs)]
        b_lasts = [b[dirs[d][4]:dirs[d][4] + 1, :] for (d, _), b in zip(items, bs)]
        q_ins = [_silu(q_ref[rows, :]) * jnp.exp(b) for (_, rows), b in zip(items, bs)]
        a_s = [jnp.where(dirs[d][3], _dot_nt(q_in, (1.0 - f) * jnp.exp(-b)), 0.0)
               for (d, _), q_in, f, b in zip(items, q_ins, fs, bs)]
        vs = [v_ref[rows, :] for _, rows in items]
        o_intra = [_dot(a, v) for a, v in zip(a_s, vs)]
        u_ts = [_dot_tn(v, (1.0 - f) * jnp.exp(b_last - b)) for v, f, b, b_last in zip(vs, fs, bs, b_lasts)]
        for d in (0, 1):
            st_ref, out_s = dirs[d][5], dirs[d][6]
            st = st_ref[...]
            for k, (dk, rows) in enumerate(items):
                if dk == d:
                    out_s[rows, :] = o_intra[k] + _dot_nt(q_ins[k], st)
                    st = st * jnp.exp(b_lasts[k]) + u_ts[k]
            st_ref[...] = st
        return carry

    lax.fori_loop(0, n // group, body, 0)

    if context:
        sfin_ref[0, 0, 0] = sf_s[...].T
        sfin_ref[0, 1, 0] = sb_s[...].T

    nw = nw_ref[...]

    def epilogue(j, carry):
        rows = pl.ds(pl.multiple_of(j * TM, TM), TM)
        o = of_s[rows, :] + ob_s[rows, :]
        o_ref[rows, :] = _rms(o, nw) * _silu(g_ref[rows, :])
        return carry

    lax.fori_loop(0, seq_len // TM, epilogue, 0)


def _hgrn(proj, lb_logits, norm_w, layer, context, s0=None):
    seq_len = SEQ if context else DEC_SEQ
    bsz = BATCH if context else DEC_BATCH
    row0 = 0 if context else N_PROMPT // seq_len

    def col(k):
        return pl.BlockSpec((seq_len, LANE), lambda b, h: (row0 + b, k * H_A + h))

    in_specs = [col(0), col(1), col(2), col(3), col(4),
                pl.BlockSpec((2, DEPTH + 1, LANE), lambda b, h: (0, 0, h)),
                pl.BlockSpec((1, LANE), lambda b, h: (0, 0))]
    args = [proj, proj, proj, proj, proj, lb_logits, norm_w.reshape(1, LANE)]
    o_shape = jax.ShapeDtypeStruct((bsz * seq_len, W_A), F32)
    o_spec = pl.BlockSpec((seq_len, LANE), lambda b, h: (b, h))
    scratch = [pltpu.VMEM((seq_len, LANE), F32), pltpu.VMEM((seq_len, LANE), F32),
               pltpu.VMEM((LANE, LANE), F32), pltpu.VMEM((LANE, LANE), F32)]
    kern = functools.partial(_hgrn_kernel, seq_len=seq_len, layer=layer, context=context)
    if context:
        return pl.pallas_call(
            kern,
            out_shape=(o_shape, jax.ShapeDtypeStruct((bsz, 2, H_A, DK_A, DK_A), F32)),
            grid=(bsz, H_A),
            in_specs=in_specs,
            out_specs=(o_spec, pl.BlockSpec((1, 2, 1, DK_A, DK_A), lambda b, h: (b, 0, h, 0, 0))),
            scratch_shapes=scratch,
            compiler_params=_params(("parallel", "parallel")),
            name="hgrn_ctx",
        )(*args)
    in_specs += [pl.BlockSpec((1, 2, 1, DK_A, DK_A), lambda b, h: (b, 0, h, 0, 0))]
    args += [s0]
    return pl.pallas_call(
        kern,
        out_shape=o_shape,
        grid=(bsz, H_A),
        in_specs=in_specs,
        out_specs=o_spec,
        scratch_shapes=scratch,
        compiler_params=_params(("parallel", "parallel")),
        name="hgrn_dec",
    )(*args)


ATT_SUB = 256
ATT_TQ = 512
COL_QD = 5 * W_A // LANE
COL_KD = COL_QD + QK_B // LANE
COL_VD = COL_KD + QK_B // LANE


def _attn_kernel(*refs, seq_len, layer, context):
    if context:
        q_ref, k_ref, v_ref, lam_ref, nw_ref, o_ref, k_s, v_s = refs
    else:
        q_ref, k_ref, v_ref, ck_ref, cv_ref, lam_ref, nw_ref, o_ref, k_s, v_s = refs

    @pl.when(pl.program_id(2) == 0)
    def _():
        k_s[0:seq_len, :] = k_ref[...].astype(BF16)
        v_s[0:seq_len, :] = v_ref[...].astype(BF16)
        if not context:
            k_s[seq_len:, :] = ck_ref[0].astype(BF16)
            v_s[seq_len:, :] = cv_ref[0].astype(BF16)

    lam_init = 0.8 - 0.6 * math.exp(-0.3 * layer)
    lp = lam_ref[...]
    lam = (jnp.exp(jnp.sum(lp[0:1] * lp[1:2], axis=1, keepdims=True))
           - jnp.exp(jnp.sum(lp[2:3] * lp[3:4], axis=1, keepdims=True)) + lam_init)

    k = k_s[...]
    v = v_s[...]
    nw = nw_ref[...]
    lane = lax.broadcasted_iota(jnp.int32, (ATT_SUB, LANE), 1)
    for r0 in range(0, q_ref.shape[0], ATT_SUB):
        q = q_ref[r0:r0 + ATT_SUB, :] * (DH_B ** -0.5 * math.log2(math.e))
        s0 = _dot_nt(jnp.where(lane < DH_B, q, 0.0), k)
        s1 = _dot_nt(jnp.where(lane < DH_B, 0.0, q), k)
        p0 = jnp.exp2(s0 - jnp.max(s0, axis=1, keepdims=True))
        p1 = jnp.exp2(s1 - jnp.max(s1, axis=1, keepdims=True))
        l0 = jnp.sum(p0, axis=1, keepdims=True)
        l1 = jnp.sum(p1, axis=1, keepdims=True)
        a = p0 - (lam * l0 / l1) * p1
        o = jnp.dot(a.astype(BF16), v, preferred_element_type=F32) / l0
        o_ref[r0:r0 + ATT_SUB, :] = _rms(o, nw) * (1.0 - lam_init)


def _attn(proj, lam_p, norm_w, layer, context, cache_k=None, cache_v=None):
    seq_len = SEQ if context else DEC_SEQ
    bsz = BATCH if context else DEC_BATCH
    row0 = 0 if context else N_PROMPT // seq_len
    tq = min(ATT_TQ, seq_len)
    nq = seq_len // tq
    tile0 = row0 * nq
    t_k = seq_len if context else seq_len + PAST_LEN
    in_specs = [pl.BlockSpec((tq, LANE), lambda b, h, i: (tile0 + b * nq + i, COL_QD + h)),
                pl.BlockSpec((seq_len, LANE), lambda b, h, i: (row0 + b, COL_KD + h)),
                pl.BlockSpec((seq_len, LANE), lambda b, h, i: (row0 + b, COL_VD + h))]
    args = [proj, proj, proj]
    if not context:
        in_specs += [pl.BlockSpec((1, PAST_LEN, LANE), lambda b, h, i: (b, 0, h)),
                     pl.BlockSpec((1, PAST_LEN, LANE), lambda b, h, i: (b, 0, h))]
        args += [cache_k, cache_v]
    in_specs += [pl.BlockSpec((4, DH_B), lambda b, h, i: (0, 0)),
                 pl.BlockSpec((1, LANE), lambda b, h, i: (0, 0))]
    args += [lam_p, norm_w.reshape(1, LANE)]
    return pl.pallas_call(
        functools.partial(_attn_kernel, seq_len=seq_len, layer=layer, context=context),
        out_shape=jax.ShapeDtypeStruct((bsz * seq_len, W_B), F32),
        grid=(bsz, H_B, nq),
        in_specs=in_specs,
        out_specs=pl.BlockSpec((tq, LANE), lambda b, h, i: (b * nq + i, h)),
        scratch_shapes=[pltpu.VMEM((t_k, LANE), BF16), pltpu.VMEM((t_k, LANE), BF16)],
        compiler_params=_params(("parallel", "parallel", "arbitrary")),
        name="attn_ctx" if context else "attn_dec",
    )(*args)


COL_GATES = 4 * W_C // LANE
GATE_CHUNKS = 4
GDN_CTX_SEQS = 4


def _gdn_gates_kernel(g_ref, alog_ref, dtb_ref, col_ref, row_ref):
    c = GDN_CHUNK
    row = lax.broadcasted_iota(jnp.int32, (c, c), 0)
    col = lax.broadcasted_iota(jnp.int32, (c, c), 1)
    lane = lax.broadcasted_iota(jnp.int32, (c, LANE), 1)
    lower = (row >= col).astype(F32)
    upper = (row <= col).astype(F32)
    for j in range(GATE_CHUNKS):
        raw = g_ref[j * c:(j + 1) * c, :]
        z = raw + dtb_ref[...]
        softplus = jnp.maximum(z, 0.0) + jnp.log(1.0 + jnp.exp(-jnp.abs(z)))
        g = -jnp.exp(alog_ref[...]) * softplus
        out = jnp.where(lane < H_C, _tri_matmul(lower, g),
                        jnp.where(lane < 2 * H_C, _tri_matmul(upper, g), _sigmoid(raw)))
        col_ref[j * c:(j + 1) * c, :] = out
        row_ref[j] = out.T


def _gdn_gates(proj, alog_lane, dtb_lane):
    c = GDN_CHUNK
    n = N_TOK // c
    return pl.pallas_call(
        _gdn_gates_kernel,
        out_shape=(jax.ShapeDtypeStruct((N_TOK, LANE), F32), jax.ShapeDtypeStruct((n, LANE, c), F32)),
        grid=(n // GATE_CHUNKS,),
        in_specs=[pl.BlockSpec((GATE_CHUNKS * c, LANE), lambda i: (i, COL_GATES)),
                  pl.BlockSpec((1, LANE), lambda i: (0, 0)),
                  pl.BlockSpec((1, LANE), lambda i: (0, 0))],
        out_specs=(pl.BlockSpec((GATE_CHUNKS * c, LANE), lambda i: (i, 0)),
                   pl.BlockSpec((GATE_CHUNKS, LANE, c), lambda i: (i, 0, 0))),
        compiler_params=_params(("parallel",)),
        name="gdn_gates",
    )(proj, alog_lane, dtb_lane)


def _unit_tri_inverse_pairs(ms, row, col):
    c = ms[0].shape[0]

    def mm(xs, ys):
        out = []
        for x, y in zip(xs, ys):
            xb = x.astype(BF16)
            yb = y.astype(BF16)
            out.append(jnp.concatenate(
                [jnp.dot(xb[:, :c], yb[:, :c], preferred_element_type=F32),
                 jnp.dot(xb[:, c:], yb[:, c:], preferred_element_type=F32)], axis=1))
        return out

    def add(xs, ys):
        return [x + y for x, y in zip(xs, ys)]

    eye = (row == col).astype(F32)
    a = [jnp.where((row // 16) == (col // 16), m, 0.0) for m in ms]
    a2 = mm(a, a)
    a4 = mm(a2, a2)
    a8 = mm(a4, a4)
    t = [eye - x for x in a]
    t = add(t, mm(t, a2))
    t = add(t, mm(t, a4))
    t = add(t, mm(t, a8))
    blk = 32
    while blk <= c:
        off = ((row // blk) == (col // blk)) & ((row // (blk // 2)) != (col // (blk // 2)))
        corr = mm(mm(t, [jnp.where(off, m, 0.0) for m in ms]), t)
        t = [x - y for x, y in zip(t, corr)]
        blk *= 2
    return t


def _gdn_kernel(*refs, seq_len, n_seq, context):
    if context:
        (q_ref, k_ref, v_ref, go_ref, cwq_ref, cwk_ref, cwv_ref, gcol_ref, grow_ref, nw_ref,
         o_ref, sfin_ref, su_s, pr_s, of_s, ob_s, sf_s, sb_s) = refs
    else:
        (q_ref, k_ref, v_ref, go_ref, cwq_ref, cwk_ref, cwv_ref, gcol_ref, grow_ref, nw_ref, s0_ref,
         o_ref, su_s, pr_s, of_s, ob_s, sf_s, sb_s) = refs
    c = GDN_CHUNK
    cps = seq_len // c
    n = n_seq * cps
    head = pl.program_id(1)
    row = lax.broadcasted_iota(jnp.int32, (c, c), 0)
    col = lax.broadcasted_iota(jnp.int32, (c, c), 1)
    row2 = lax.broadcasted_iota(jnp.int32, (c, 2 * c), 0)
    col2 = lax.broadcasted_iota(jnp.int32, (c, 2 * c), 1) & (c - 1)
    lane = lax.broadcasted_iota(jnp.int32, (c, LANE), 1)
    rowi = lax.broadcasted_iota(jnp.int32, (c, LANE), 0)

    def gc_row_last(ci, d):
        gc_row = grow_ref[ci, pl.ds(d * H_C + head, 1), :]
        return gc_row, (gc_row[:, c - 1:c] if d == 0 else gc_row[:, 0:1])

    def chunk_inputs(ci):
        r0 = pl.multiple_of(ci * c, c)
        rows = pl.ds(r0, c)

        def conv(x_ref, w_ref):
            cur = x_ref[rows, :]
            before = x_ref[pl.ds(pl.multiple_of(jnp.maximum(r0 - 8, 0), 8), 8), :]
            after = x_ref[pl.ds(pl.multiple_of(jnp.minimum(r0 + c, n * c - 8), 8), 8), :]
            in_seq = lax.rem(ci, cps)
            prev_row = jnp.where(in_seq > 0, before[7:8, :], 0.0)
            next_row = jnp.where(in_seq < cps - 1, after[0:1, :], 0.0)
            xm1 = jnp.where(rowi == 0, prev_row, pltpu.roll(cur, 1, 0))
            xp1 = jnp.where(rowi == c - 1, next_row, pltpu.roll(cur, c - 1, 0))
            w = w_ref[...]
            return _silu(xm1 * w[0:1, :] + cur * w[1:2, :] + xp1 * w[2:3, :])

        q = conv(q_ref, cwq_ref)
        k = conv(k_ref, cwk_ref)
        vn = conv(v_ref, cwv_ref)
        qn = q * lax.rsqrt(jnp.sum(q * q, axis=1, keepdims=True) + EPS) * (DK_C ** -0.5)
        kn = k * lax.rsqrt(jnp.sum(k * k, axis=1, keepdims=True) + EPS)
        kq = _dot_nt(jnp.concatenate([kn, qn], axis=0), kn)
        gates = gcol_ref[rows, :]
        per_dir = []
        ms = []
        for d in (0, 1):
            gc = jnp.sum(jnp.where(lane == d * H_C + head, gates, 0.0), axis=1, keepdims=True)
            beta = jnp.sum(jnp.where(lane == (2 + d) * H_C + head, gates, 0.0), axis=1, keepdims=True)
            gc_row, gc_last = gc_row_last(ci, d)
            incl = (row >= col) if d == 0 else (row <= col)
            strict = (row > col) if d == 0 else (row < col)
            decay = jnp.where(incl, jnp.exp(gc - gc_row), 0.0)
            m = jnp.where(strict, kq[:c] * beta * decay, 0.0)
            per_dir.append((gc, beta, kq[c:] * decay, (kn * jnp.exp(gc_last - gc)).T))
            ms.append(m)
        return jnp.concatenate(ms, axis=1), (rows, qn, kn, vn, per_dir)

    def chunk_outputs(ci, t, rest):
        rows, qn, kn, vn, per_dir = rest
        base = pl.multiple_of(ci * 2 * c, 2 * c)
        for d in (0, 1):
            gc, beta, qk, kd_t = per_dir[d]
            e = jnp.exp(gc)
            uw = _dot(t[:, d * c:(d + 1) * c], jnp.concatenate([vn * beta, kn * (beta * e)], axis=1))
            cross = _dot(jnp.concatenate([kd_t, qk], axis=0), uw)
            su_s[d, rows, :] = cross[:c, :c]
            (of_s, ob_s)[d][rows, :] = cross[c:, :c]
            pr_s[d, pl.ds(base, c), :] = cross[:c, c:].astype(BF16)
            pr_s[d, pl.ds(base + c, c), :] = (qn * e - cross[c:, c:]).astype(BF16)

    group = min(8, n)

    def prepare(j, carry):
        cis = [j * group + g for g in range(group)]
        staged = [chunk_inputs(ci) for ci in cis]
        ts = _unit_tri_inverse_pairs([s[0] for s in staged], row2, col2)
        for ci, t, s in zip(cis, ts, staged):
            chunk_outputs(ci, t, s[1])
        return carry

    lax.fori_loop(0, n // group, prepare, 0)

    def advance(ci, d, s_ref, out_s):
        rows = pl.ds(pl.multiple_of(ci * c, c), c)
        _, gc_last = gc_row_last(ci, d)
        s = s_ref[...]
        ps = jnp.dot(pr_s[d, pl.ds(pl.multiple_of(ci * 2 * c, 2 * c), 2 * c), :], s.astype(BF16),
                     preferred_element_type=F32)
        out_s[rows, :] = out_s[rows, :] + ps[c:]
        s_ref[...] = s * jnp.exp(gc_last) - ps[:c] + su_s[d, rows, :]

    for s in range(n_seq):
        if context:
            sf_s[...] = jnp.zeros_like(sf_s)
            sb_s[...] = jnp.zeros_like(sb_s)
        else:
            sf_s[...] = s0_ref[s, 0, 0]
            sb_s[...] = s0_ref[s, 1, 0]

        def body(i, carry, first=s * cps):
            advance(first + i, 0, sf_s, of_s)
            advance(first + cps - 1 - i, 1, sb_s, ob_s)
            return carry

        lax.fori_loop(0, cps, body, 0)

        if context:
            sfin_ref[s, 0, 0] = sf_s[...]
            sfin_ref[s, 1, 0] = sb_s[...]

    nw = nw_ref[...]

    def epilogue(j, carry):
        rows = pl.ds(pl.multiple_of(j * TM, TM), TM)
        o = of_s[rows, :] + ob_s[rows, :]
        o_ref[rows, :] = _rms(o, nw) * _silu(go_ref[rows, :])
        return carry

    lax.fori_loop(0, n * c // TM, epilogue, 0)


def _gdn(proj, conv_w, gcol, grow, norm_w, context, s0=None):
    seq_len = SEQ if context else DEC_SEQ
    n_seq = GDN_CTX_SEQS if context else 1
    n_blk = (BATCH if context else DEC_BATCH) // n_seq
    rows = n_seq * seq_len
    row0 = 0 if context else N_PROMPT // rows
    nc = rows // GDN_CHUNK

    def col(k):
        return pl.BlockSpec((rows, LANE), lambda b, h: (row0 + b, k * H_C + h))

    def cw(k):
        return pl.BlockSpec((3, LANE), lambda b, h: (0, k * H_C + h))

    in_specs = [col(0), col(1), col(2), col(3), cw(0), cw(1), cw(2),
                pl.BlockSpec((rows, LANE), lambda b, h: (row0 + b, 0)),
                pl.BlockSpec((nc, LANE, GDN_CHUNK), lambda b, h: (row0 + b, 0, 0)),
                pl.BlockSpec((1, LANE), lambda b, h: (0, 0))]
    args = [proj, proj, proj, proj, conv_w, conv_w, conv_w, gcol, grow, norm_w.reshape(1, LANE)]
    o_shape = jax.ShapeDtypeStruct((n_blk * rows, W_C), F32)
    o_spec = pl.BlockSpec((rows, LANE), lambda b, h: (b, h))
    seq_buf = pltpu.VMEM((rows, LANE), F32)
    state_buf = pltpu.VMEM((DK_C, DK_C), F32)
    scratch = [pltpu.VMEM((2, rows, LANE), F32), pltpu.VMEM((2, 2 * rows, LANE), BF16),
               seq_buf, seq_buf, state_buf, state_buf]
    kern = functools.partial(_gdn_kernel, seq_len=seq_len, n_seq=n_seq, context=context)
    state_spec = pl.BlockSpec((n_seq, 2, 1, DK_C, DK_C), lambda b, h: (b, 0, h, 0, 0))
    if context:
        return pl.pallas_call(
            kern,
            out_shape=(o_shape, jax.ShapeDtypeStruct((BATCH, 2, H_C, DK_C, DK_C), F32)),
            grid=(n_blk, H_C),
            in_specs=in_specs,
            out_specs=(o_spec, state_spec),
            scratch_shapes=scratch,
            compiler_params=_params(("parallel", "parallel")),
            name="gdn_ctx",
        )(*args)
    in_specs += [state_spec]
    args += [s0]
    return pl.pallas_call(
        kern,
        out_shape=o_shape,
        grid=(n_blk, H_C),
        in_specs=in_specs,
        out_specs=o_spec,
        scratch_shapes=scratch,
        compiler_params=_params(("parallel", "parallel")),
        name="gdn_dec",
    )(*args)


def kernel(x_prompt, x_sample, c, cache_diff_k, cache_diff_v, state_hgrn, state_gdn, c_ctx,
           ada_w, ada_b, norm_w, final_norm_w, w_in_ab, hgrn_lb_logits, hgrn_norm_w,
           diff_lambda, diff_norm_w, w_in_c, gdn_conv_w, gdn_a_log, gdn_dt_bias, gdn_norm_w,
           w_out, moe_router_group, moe_router_expert, moe_w_gate, moe_w_up, moe_w_down):
    x = (x_prompt.reshape(N_PROMPT, D_MODEL), x_sample.reshape(N_SAMPLE, D_MODEL))
    cvec = jnp.concatenate([c_ctx[None, :], c, jnp.zeros((MOD_ROWS - N_MOD, D_MODEL), F32)], axis=0)
    mod_all = _adaln(cvec, ada_w, ada_b)[:, :N_MOD].reshape(DEPTH, N_MOD, 6, D_MODEL)
    mod_all = jnp.pad(mod_all, ((0, 0), (0, 0), (0, MOD_ROWS - 6), (0, 0)))
    cos, sin = _rope_tables()
    new_k, new_v, new_hgrn, new_gdn = [], [], [], []
    for l in range(DEPTH):
        i = l // 2
        mod = mod_all[l]
        if l % 2 == 0:
            proj = _inproj(x, mod, norm_w[l, 0], w_in_ab[i].astype(BF16),
                           rope=(cos, sin, (COL_QD * LANE, COL_VD * LANE)))
            oh_ctx, s_h = _hgrn(proj, hgrn_lb_logits, hgrn_norm_w[i], l, True)
            oh_dec = _hgrn(proj, hgrn_lb_logits, hgrn_norm_w[i], l, False, s0=state_hgrn[:, i])
            od_ctx = _attn(proj, diff_lambda[i], diff_norm_w[i], l, True)
            od_dec = _attn(proj, diff_lambda[i], diff_norm_w[i], l, False,
                           cache_k=cache_diff_k[:, i].reshape(DEC_BATCH, PAST_LEN, QK_B),
                           cache_v=cache_diff_v[:, i].reshape(DEC_BATCH, PAST_LEN, W_B))
            parts = ((oh_ctx, oh_dec), (od_ctx, od_dec))
            new_k.append(proj[:N_PROMPT, COL_KD * LANE:COL_VD * LANE].reshape(BATCH, SEQ, H_B, 2, DH_B))
            new_v.append(proj[:N_PROMPT, COL_VD * LANE:].reshape(BATCH, SEQ, H_B, DV_B))
            new_hgrn.append(s_h)
        else:
            w_c = jnp.pad(w_in_c[i], ((0, 0), (0, (COL_GATES + 1) * LANE - IN_C))).astype(BF16)
            proj = _inproj(x, mod, norm_w[l, 0], w_c)
            pad = jnp.zeros((LANE - 2 * H_C,), F32)
            alog_lane = jnp.concatenate([gdn_a_log[i, 0], gdn_a_log[i, 1], pad]).reshape(1, LANE)
            dtb_lane = jnp.concatenate([gdn_dt_bias[i, 0], gdn_dt_bias[i, 1], pad]).reshape(1, LANE)
            gcol, grow = _gdn_gates(proj, alog_lane, dtb_lane)
            oc_ctx, s_c = _gdn(proj, gdn_conv_w[i], gcol, grow, gdn_norm_w[i], True)
            oc_dec = _gdn(proj, gdn_conv_w[i], gcol, grow, gdn_norm_w[i], False, s0=state_gdn[:, i])
            parts = ((oc_ctx, oc_dec),)
            new_gdn.append(s_c)
        w_router = jnp.concatenate(
            [moe_router_group[l], moe_router_expert[l],
             jnp.zeros((D_MODEL, LANE - N_GROUPS - N_EXPERTS), F32)], axis=1)
        w_router_hi = w_router.astype(BF16)
        w_router = jnp.concatenate([w_router_hi, (w_router - w_router_hi.astype(F32)).astype(BF16)], axis=1)
        x, h2, route = _outproj(parts, w_out[l].astype(BF16), x, mod, norm_w[l, 1], w_router)
        x = _moe(x, h2, route, mod, moe_w_gate, moe_w_up, moe_w_down, l, final_norm_w)
    y_prompt = x[0].reshape(BATCH, SEQ, D_MODEL)
    y_sample = x[1].reshape(DEC_BATCH, DEC_SEQ, D_MODEL)
    return (y_prompt, y_sample, jnp.stack(new_k, axis=1), jnp.stack(new_v, axis=1),
            jnp.stack(new_hgrn, axis=1), jnp.stack(new_gdn, axis=1))
```

```python
import functools
import math

import jax
import jax.numpy as jnp
from jax import lax
from jax.experimental import pallas as pl
from jax.experimental.pallas import tpu as pltpu

F32 = jnp.float32
BF16 = jnp.bfloat16
HIGHEST = lax.Precision.HIGHEST

D_MODEL = 1024
BATCH = 16
SEQ = 256
DEPTH = 2
DEC_BATCH = 4
DEC_SEQ = 4096
PAST_LEN = 256
GRID_W = 64
H_A = 4
DK_A = 128
W_A = 512
CHUNK_A = 32
H_B = 4
DH_B = 64
DV_B = 128
QK_B = 512
W_B = 512
ROPE_BASE = 10000.0
H_C = 8
DK_C = 128
W_C = 1024
N_GROUPS = 4
E_PER_GROUP = 8
N_EXPERTS = 32
D_EXPERT = 512
EPS = 1e-6
IN_AB = 5 * W_A + 2 * QK_B + W_B
IN_C = 4 * W_C + 4 * H_C

LANE = 128
N_PROMPT = BATCH * SEQ
N_SAMPLE = DEC_BATCH * DEC_SEQ
N_TOK = N_PROMPT + N_SAMPLE
TM = 256
N_TILES = N_TOK // TM
PROMPT_TILES = N_PROMPT // TM
TILES_PER_SAMPLE = DEC_SEQ // TM
N_MOD = 1 + DEC_BATCH
MOD_ROWS = 8
MOE_ROWS = 256
MOE_AHEAD = 2
ROW_TILE = D_MODEL // LANE
GDN_CHUNK = 128
VMEM_LIMIT = 56 * 1024 * 1024


def _mod_index(t):
    return jnp.where(t < PROMPT_TILES, 0, 1 + (t - PROMPT_TILES) // TILES_PER_SAMPLE)


def _sigmoid(x):
    return 1.0 / (1.0 + jnp.exp(-x))


def _silu(x):
    return x * _sigmoid(x)


def _rms(x, w):
    return x * lax.rsqrt(jnp.mean(x * x, axis=-1, keepdims=True) + EPS) * w


def _dot(a, b):
    return jnp.dot(a.astype(BF16), b.astype(BF16), preferred_element_type=F32)


def _dot_nt(a, b):
    return lax.dot_general(a.astype(BF16), b.astype(BF16), (((1,), (1,)), ((), ())),
                           preferred_element_type=F32)


def _dot_tn(a, b):
    return lax.dot_general(a.astype(BF16), b.astype(BF16), (((0,), (0,)), ((), ())),
                           preferred_element_type=F32)


def _dot_f32(a, b):
    return jnp.dot(a, b, precision=HIGHEST, preferred_element_type=F32)


def _tri_matmul(tri, x):
    hi = x.astype(BF16)
    rem = x - hi.astype(F32)
    mid = rem.astype(BF16)
    lo = (rem - mid.astype(F32)).astype(BF16)
    n = x.shape[1]
    r = jnp.dot(tri.astype(BF16), jnp.concatenate([hi, mid, lo], axis=1), preferred_element_type=F32)
    return r[:, :n] + r[:, n:2 * n] + r[:, 2 * n:]


def _params(sem):
    return pltpu.CompilerParams(dimension_semantics=sem, vmem_limit_bytes=VMEM_LIMIT)


def _adaln_kernel(c_ref, w_ref, b_ref, o_ref):
    s = _silu(c_ref[...])
    o_ref[0] = _dot(s, w_ref[0]) + b_ref[0]


def _adaln(cvec, ada_w, ada_b):
    nb = 4
    wb = 6 * D_MODEL // nb
    return pl.pallas_call(
        _adaln_kernel,
        out_shape=jax.ShapeDtypeStruct((DEPTH, MOD_ROWS, 6 * D_MODEL), F32),
        grid=(DEPTH, nb),
        in_specs=[pl.BlockSpec((MOD_ROWS, D_MODEL), lambda l, j: (0, 0)),
                  pl.BlockSpec((1, D_MODEL, wb), lambda l, j: (l, 0, j)),
                  pl.BlockSpec((1, 1, wb), lambda l, j: (l, 0, j))],
        out_specs=pl.BlockSpec((1, MOD_ROWS, wb), lambda l, j: (l, 0, j)),
        compiler_params=_params(("parallel", "parallel")),
        name="adaln",
    )(cvec, ada_w, ada_b.reshape(DEPTH, 1, 6 * D_MODEL))


def _row_specs(x, width):
    if isinstance(x, tuple):
        return ([pl.BlockSpec((TM, width), lambda t: (jnp.minimum(t, PROMPT_TILES - 1), 0)),
                 pl.BlockSpec((TM, width), lambda t: (jnp.maximum(t - PROMPT_TILES, 0), 0))], list(x))
    return [pl.BlockSpec((TM, width), lambda t: (t, 0))], [x]


def _read_rows(refs):
    if len(refs) == 2:
        return jnp.where(pl.program_id(0) < PROMPT_TILES, refs[0][...], refs[1][...])
    return refs[0][...]


def _inproj_kernel(*refs, n_x, rope_cols):
    x = _read_rows(refs[:n_x])
    if rope_cols is None:
        mod_ref, nw_ref, w_ref, o_ref = refs[n_x:]
    else:
        mod_ref, nw_ref, w_ref, cos_ref, sin_ref, o_ref = refs[n_x:]
    h = _rms(x, nw_ref[...]) * (1.0 + mod_ref[0, 1:2, :]) + mod_ref[0, 0:1, :]
    r = jnp.dot(h.astype(BF16), w_ref[...], preferred_element_type=F32)
    if rope_cols is None:
        o_ref[...] = r
        return
    lo, hi = rope_cols
    o_ref[:, :lo] = r[:, :lo]
    o_ref[:, hi:] = r[:, hi:]
    cos = cos_ref[...]
    sin = sin_ref[...]
    lane = lax.broadcasted_iota(jnp.int32, (TM, LANE), 1)
    upper = (lane & 16) != 0
    for c0 in range(lo, hi, LANE):
        v = r[:, c0:c0 + LANE]
        partner = jnp.where(upper, pltpu.roll(v, 16, 1), pltpu.roll(v, LANE - 16, 1))
        o_ref[:, c0:c0 + LANE] = v * cos + partner * sin


def _inproj(x, mod, norm_w, w_bf16, rope=None):
    p = w_bf16.shape[1]
    in_specs, args = _row_specs(x, D_MODEL)
    n_x = len(args)
    in_specs += [pl.BlockSpec((1, MOD_ROWS, D_MODEL), lambda t: (_mod_index(t), 0, 0)),
                 pl.BlockSpec((1, D_MODEL), lambda t: (0, 0)),
                 pl.BlockSpec((D_MODEL, p), lambda t: (0, 0))]
    args += [mod, norm_w.reshape(1, D_MODEL), w_bf16]
    rope_cols = None
    if rope is not None:
        cos, sin, rope_cols = rope

        def rope_index(t):
            return (jnp.where(t < PROMPT_TILES, 0, 1 + (t - PROMPT_TILES) % TILES_PER_SAMPLE), 0)

        in_specs += [pl.BlockSpec((TM, LANE), rope_index), pl.BlockSpec((TM, LANE), rope_index)]
        args += [cos, sin]
    return pl.pallas_call(
        functools.partial(_inproj_kernel, n_x=n_x, rope_cols=rope_cols),
        out_shape=jax.ShapeDtypeStruct((N_TOK, p), F32),
        grid=(N_TILES,),
        in_specs=in_specs,
        out_specs=pl.BlockSpec((TM, p), lambda t: (t, 0)),
        compiler_params=_params(("parallel",)),
        name="inproj",
    )(*args)


def _rope_tables():
    lane = jnp.arange(LANE)
    d = lane % DH_B
    use_col = (d // 32) == 1
    j = d % 16
    upper = ((d % 32) // 16) == 1
    inv_freq = ROPE_BASE ** (-j.astype(F32) / 16.0)
    t = jnp.arange(DEC_SEQ)
    row = (t // GRID_W).astype(F32)
    col = (t % GRID_W).astype(F32)
    pos = jnp.where(use_col[None, :], col[:, None], row[:, None])
    ang = pos * inv_freq[None, :]
    cos = jnp.cos(ang)
    sin = jnp.where(upper[None, :], jnp.sin(ang), -jnp.sin(ang))
    cos = jnp.concatenate([jnp.ones((TM, LANE), F32), cos], axis=0)
    sin = jnp.concatenate([jnp.zeros((TM, LANE), F32), sin], axis=0)
    return cos, sin


def _route(logits):
    lane = lax.broadcasted_iota(jnp.int32, logits.shape, 1)
    lanef = lane.astype(F32)
    neg = jnp.float32(-jnp.inf)
    gl = jnp.where(lane < N_GROUPS, logits, neg)
    gmax = jnp.max(gl, axis=1, keepdims=True)
    gsel = jnp.min(jnp.where(gl == gmax, lanef, float(LANE)), axis=1, keepdims=True)
    p_grp = 1.0 / jnp.sum(jnp.exp(gl - gmax), axis=1, keepdims=True)
    lo = float(N_GROUPS) + gsel * float(E_PER_GROUP)
    el = jnp.where((lanef >= lo) & (lanef < lo + float(E_PER_GROUP)), logits, neg)
    v1 = jnp.max(el, axis=1, keepdims=True)
    i1 = jnp.min(jnp.where(el == v1, lanef, float(LANE)), axis=1, keepdims=True)
    el2 = jnp.where(lanef == i1, neg, el)
    v2 = jnp.max(el2, axis=1, keepdims=True)
    i2 = jnp.min(jnp.where(el2 == v2, lanef, float(LANE)), axis=1, keepdims=True)
    t = jnp.exp(v2 - v1)
    w1 = p_grp / (1.0 + t)
    w2 = p_grp * t / (1.0 + t)
    out = jnp.where(lane == 0, i1 - float(N_GROUPS), 0.0)
    out = jnp.where(lane == 1, i2 - float(N_GROUPS), out)
    out = jnp.where(lane == 2, w1, out)
    out = jnp.where(lane == 3, w2, out)
    return out


def _outproj_kernel(*refs, widths, n_x):
    n_in = len(widths)
    x_refs = refs[2 * n_in:2 * n_in + n_x]
    w_ref, mod_ref, nw_ref, wr_ref, xn_ref, h2_ref, rt_ref = refs[2 * n_in + n_x:]
    y = None
    c0 = 0
    for k, wd in enumerate(widths):
        o = _read_rows(refs[2 * k:2 * k + 2])
        part = jnp.dot(o.astype(BF16), w_ref[c0:c0 + wd, :], preferred_element_type=F32)
        y = part if y is None else y + part
        c0 += wd
    xn = _read_rows(x_refs) + mod_ref[0, 2:3, :] * y
    xn_ref[...] = xn
    h2 = _rms(xn, nw_ref[...]) * (1.0 + mod_ref[0, 4:5, :]) + mod_ref[0, 3:4, :]
    for s in range(ROW_TILE):
        h2_ref[pl.ds(s, TM, stride=ROW_TILE), :] = h2[:, s * LANE:(s + 1) * LANE]
    h_hi = h2.astype(BF16)
    h_lo = (h2 - h_hi.astype(F32)).astype(BF16)
    wr = wr_ref[...]
    hw = jnp.dot(h_hi, wr, preferred_element_type=F32)
    logits = hw[:, :LANE] + hw[:, LANE:] + jnp.dot(h_lo, wr[:, :LANE], preferred_element_type=F32)
    rt_ref[...] = _route(logits)


def _outproj(parts, w_bf16, x, mod, norm_w2, w_router):
    widths = tuple(p[0].shape[1] for p in parts)
    in_specs, args = [], []
    for p, wd in zip(parts, widths):
        specs, ops = _row_specs(p, wd)
        in_specs += specs
        args += ops
    specs, ops = _row_specs(x, D_MODEL)
    in_specs += specs
    args += ops
    in_specs += [pl.BlockSpec((D_MODEL, D_MODEL), lambda t: (0, 0)),
                 pl.BlockSpec((1, MOD_ROWS, D_MODEL), lambda t: (_mod_index(t), 0, 0)),
                 pl.BlockSpec((1, D_MODEL), lambda t: (0, 0)),
                 pl.BlockSpec((D_MODEL, 2 * LANE), lambda t: (0, 0))]
    args += [w_bf16, mod, norm_w2.reshape(1, D_MODEL), w_router]
    return pl.pallas_call(
        functools.partial(_outproj_kernel, widths=widths, n_x=len(ops)),
        out_shape=(jax.ShapeDtypeStruct((N_TOK, D_MODEL), F32),
                   jax.ShapeDtypeStruct((N_TOK * ROW_TILE, LANE), F32),
                   jax.ShapeDtypeStruct((N_TOK, LANE), F32)),
        grid=(N_TILES,),
        in_specs=in_specs,
        out_specs=(pl.BlockSpec((TM, D_MODEL), lambda t: (t, 0)),
                   pl.BlockSpec((TM * ROW_TILE, LANE), lambda t: (t, 0)),
                   pl.BlockSpec((TM, LANE), lambda t: (t, 0))),
        compiler_params=_params(("parallel",)),
        name="outproj",
    )(*args)


def _expert_kernel(blk_e_ref, nact_ref, rowtok_ref, h2_hbm, wg_ref, wu_ref, wd_ref, y_ref,
                   x_0, x_1, x_2, sems, wg_s, wu_s, wd_s):
    i = pl.program_id(0)
    n_act = nact_ref[0]
    e = blk_e_ref[i]
    prev = blk_e_ref[jnp.maximum(i - 1, 0)]
    bufs = (x_0, x_1, x_2)
    ring = MOE_AHEAD + 1
    assert ring == len(bufs)

    def row_copy(blk, r, slot):
        tok = rowtok_ref[blk * MOE_ROWS + r]
        return pltpu.make_async_copy(
            h2_hbm.at[pl.ds(pl.multiple_of(tok * ROW_TILE, ROW_TILE), ROW_TILE), :],
            bufs[slot].at[pl.ds(pl.multiple_of(r * ROW_TILE, ROW_TILE), ROW_TILE), :],
            sems.at[slot])

    def wait_block(slot):
        pltpu.make_async_copy(h2_hbm.at[pl.ds(0, MOE_ROWS * ROW_TILE), :], bufs[slot], sems.at[slot]).wait()

    @pl.when(i == 0)
    def _():
        for blk in range(MOE_AHEAD):
            def start_row(r, carry):
                row_copy(blk, r, blk).start()
                return carry
            lax.fori_loop(0, MOE_ROWS, start_row, 0)

    @pl.when((i == 0) | (e != prev))
    def _():
        wg_s[...] = wg_ref[...].astype(BF16)
        wu_s[...] = wu_ref[...].astype(BF16)
        wd_s[...] = wd_ref[...].astype(BF16)

    for slot in range(ring):
        @pl.when((i >= n_act) & (i < n_act + MOE_AHEAD) & (i % ring == slot))
        def _():
            wait_block(slot)

        @pl.when((i < n_act) & (i % ring == slot))
        def _():
            wait_block(slot)
            for r in range(MOE_ROWS):
                row_copy(i + MOE_AHEAD, r, (slot + MOE_AHEAD) % ring).start(priority=r % 2)
            x = jnp.concatenate([bufs[slot][pl.ds(s, MOE_ROWS, stride=ROW_TILE), :] for s in range(ROW_TILE)],
                                axis=1).astype(BF16)
            g = jnp.dot(x, wg_s[...], preferred_element_type=F32)
            u = jnp.dot(x, wu_s[...], preferred_element_type=F32)
            y_ref[...] = jnp.dot((_silu(g) * u).astype(BF16), wd_s[...], preferred_element_type=F32)

    @pl.when(i >= nact_ref[0])
    def _():
        y_ref[...] = jnp.zeros_like(y_ref)


def _experts(h2_tiles, row_tok, blk_e, n_active, w_gate, w_up, w_down, layer):
    n_rows = row_tok.shape[0]
    n_blocks = n_rows // MOE_ROWS
    grid_spec = pltpu.PrefetchScalarGridSpec(
        num_scalar_prefetch=3,
        grid=(n_blocks,),
        in_specs=[pl.BlockSpec(memory_space=pl.ANY),
                  pl.BlockSpec((None, None, D_MODEL, D_EXPERT), lambda i, be, na, rt: (layer, be[i], 0, 0)),
                  pl.BlockSpec((None, None, D_MODEL, D_EXPERT), lambda i, be, na, rt: (layer, be[i], 0, 0)),
                  pl.BlockSpec((None, None, D_EXPERT, D_MODEL), lambda i, be, na, rt: (layer, be[i], 0, 0))],
        out_specs=pl.BlockSpec((MOE_ROWS, D_MODEL), lambda i, be, na, rt: (i, 0)),
        scratch_shapes=[pltpu.VMEM((MOE_ROWS * ROW_TILE, LANE), F32),
                        pltpu.VMEM((MOE_ROWS * ROW_TILE, LANE), F32),
                        pltpu.VMEM((MOE_ROWS * ROW_TILE, LANE), F32),
                        pltpu.SemaphoreType.DMA((MOE_AHEAD + 1,)),
                        pltpu.VMEM((D_MODEL, D_EXPERT), BF16),
                        pltpu.VMEM((D_MODEL, D_EXPERT), BF16),
                        pltpu.VMEM((D_EXPERT, D_MODEL), BF16)])
    return pl.pallas_call(
        _expert_kernel,
        out_shape=jax.ShapeDtypeStruct((n_rows, D_MODEL), F32),
        grid_spec=grid_spec,
        compiler_params=_params(("arbitrary",)),
        name="experts",
    )(blk_e, n_active, row_tok, h2_tiles, w_gate, w_up, w_down)


def _combine_kernel(x_ref, y0_ref, y1_ref, rt_ref, mod_ref, fw_ref, o_ref, *, final):
    rt = rt_ref[...]
    y = rt[:, 2:3] * y0_ref[...] + rt[:, 3:4] * y1_ref[...]
    x = x_ref[...] + mod_ref[0, 5:6, :] * y
    o_ref[...] = _rms(x, fw_ref[...]) if final else x


def _combine(x, y0, y1, route, mod, final_w, final=False, tile0=0, n_rows=N_TOK):
    row = pl.BlockSpec((TM, D_MODEL), lambda t: (t + tile0, 0))
    return pl.pallas_call(
        functools.partial(_combine_kernel, final=final),
        out_shape=jax.ShapeDtypeStruct((n_rows, D_MODEL), F32),
        grid=(n_rows // TM,),
        in_specs=[row, row, row,
                  pl.BlockSpec((TM, LANE), lambda t: (t + tile0, 0)),
                  pl.BlockSpec((1, MOD_ROWS, D_MODEL), lambda t: (_mod_index(t + tile0), 0, 0)),
                  pl.BlockSpec((1, D_MODEL), lambda t: (0, 0))],
        out_specs=pl.BlockSpec((TM, D_MODEL), lambda t: (t, 0)),
        compiler_params=_params(("parallel",)),
        name="combine",
    )(x, y0, y1, route, mod, final_w.reshape(1, D_MODEL))


def _take_rows(a, idx):
    return a.at[idx].get(mode="promise_in_bounds")


def _moe(x, h2, route, mod, w_gate, w_up, w_down, layer, final_w):
    n_asg = 2 * N_TOK
    flat_e = route[:, :2].astype(jnp.int32).reshape(n_asg)
    onehot = (flat_e[:, None] == jnp.arange(N_EXPERTS, dtype=jnp.int32)[None, :]).astype(jnp.int32)
    csum = jnp.cumsum(onehot, axis=0)
    rank = jnp.take_along_axis(csum, flat_e[:, None], axis=1)[:, 0] - 1
    counts = csum[-1]
    padded = (counts + MOE_ROWS - 1) // MOE_ROWS * MOE_ROWS
    pad_end = jnp.cumsum(padded)
    pad_start = pad_end - padded
    dest = pad_start[flat_e] + rank
    n_rows = n_asg + (N_EXPERTS + MOE_AHEAD) * MOE_ROWS
    n_blocks = n_rows // MOE_ROWS
    blk_start = jnp.arange(n_blocks, dtype=jnp.int32) * MOE_ROWS
    blk_e = jnp.sum((blk_start[:, None] >= pad_end[None, :]).astype(jnp.int32), axis=1)
    blk_e = jnp.minimum(blk_e, N_EXPERTS - 1)
    order = jnp.argsort(flat_e, stable=True).astype(jnp.int32)
    row_e = jnp.repeat(blk_e, MOE_ROWS)
    start = jnp.cumsum(counts) - counts
    rows = jnp.arange(n_rows, dtype=jnp.int32)
    src = rows + (start - pad_start)[row_e]
    real = rows < (pad_start + counts)[row_e]
    row_tok = jnp.where(real, order.at[jnp.clip(src, 0, n_asg - 1)].get(mode="promise_in_bounds") // 2,
                        rows % N_TOK)
    n_active = (pad_end[-1:] // MOE_ROWS).astype(jnp.int32)
    yb = _experts(h2, row_tok, blk_e, n_active, w_gate, w_up, w_down, layer)
    dest2 = dest.reshape(N_TOK, 2)
    y0 = _take_rows(yb, dest2[:, 0])
    y1 = _take_rows(yb, dest2[:, 1])
    if layer < DEPTH - 1:
        return _combine(x, y0, y1, route, mod, final_w)
    return (_combine(x, y0, y1, route, mod, final_w, final=True, n_rows=N_PROMPT),
            _combine(x, y0, y1, route, mod, final_w, final=True, tile0=PROMPT_TILES, n_rows=N_SAMPLE))


def _hgrn_kernel(*refs, seq_len, layer, context):
    if context:
        (q_ref, ff_ref, fb_ref, v_ref, g_ref, lbl_ref, nw_ref,
         o_ref, sfin_ref, of_s, ob_s, sf_s, sb_s) = refs
    else:
        (q_ref, ff_ref, fb_ref, v_ref, g_ref, lbl_ref, nw_ref, s0_ref,
         o_ref, of_s, ob_s, sf_s, sb_s) = refs
    c = CHUNK_A
    n = seq_len // c

    def lower_bound(d):
        z = lbl_ref[d]
        e = jnp.exp(z - jnp.max(z, axis=0, keepdims=True))
        return jnp.sum(e[:layer + 1], axis=0, keepdims=True) / jnp.sum(e, axis=0, keepdims=True)

    lb_f = lower_bound(0)
    lb_b = lower_bound(1)
    row = lax.broadcasted_iota(jnp.int32, (c, c), 0)
    col = lax.broadcasted_iota(jnp.int32, (c, c), 1)
    causal = row >= col
    tri_f = causal.astype(F32)
    tri_b = (row <= col).astype(F32)

    if context:
        sf_s[...] = jnp.zeros_like(sf_s)
        sb_s[...] = jnp.zeros_like(sb_s)
    else:
        sf_s[...] = s0_ref[0, 0, 0].T
        sb_s[...] = s0_ref[0, 1, 0].T

    group = min(16, n)
    dirs = ((ff_ref, lb_f, tri_f, causal, c - 1, sf_s, of_s),
            (fb_ref, lb_b, tri_b, row <= col, 0, sb_s, ob_s))

    def body(i, carry):
        items = []
        for g in range(group):
            j = i * group + g
            items.append((0, pl.ds(pl.multiple_of(j * c, c), c)))
            items.append((1, pl.ds(pl.multiple_of((n - 1 - j) * c, c), c)))
        fs = [dirs[d][1] + (1.0 - dirs[d][1]) * _sigmoid(dirs[d][0][rows, :]) for d, rows in items]
        bs = [_tri_matmul(dirs[d][2], jnp.log(f)) for (d, _), f in zip(items, fs)]
        b_lasts = [b[dirs[d][4]:dirs[d][4] + 1, :] for (d, _), b in zip(items, bs)]
        q_ins = [_silu(q_ref[rows, :]) * jnp.exp(b) for (_, rows), b in zip(items, bs)]
        a_s = [jnp.where(dirs[d][3], _dot_nt(q_in, (1.0 - f) * jnp.exp(-b)), 0.0)
               for (d, _), q_in, f, b in zip(items, q_ins, fs, bs)]
        vs = [v_ref[rows, :] for _, rows in items]
        o_intra = [_dot(a, v) for a, v in zip(a_s, vs)]
        u_ts = [_dot_tn(v, (1.0 - f) * jnp.exp(b_last - b)) for v, f, b, b_last in zip(vs, fs, bs, b_lasts)]
        for d in (0, 1):
            st_ref, out_s = dirs[d][5], dirs[d][6]
            st = st_ref[...]
            for k, (dk, rows) in enumerate(items):
                if dk == d:
                    out_s[rows, :] = o_intra[k] + _dot_nt(q_ins[k], st)
                    st = st * jnp.exp(b_lasts[k]) + u_ts[k]
            st_ref[...] = st
        return carry

    lax.fori_loop(0, n // group, body, 0)

    if context:
        sfin_ref[0, 0, 0] = sf_s[...].T
        sfin_ref[0, 1, 0] = sb_s[...].T

    nw = nw_ref[...]

    def epilogue(j, carry):
        rows = pl.ds(pl.multiple_of(j * TM, TM), TM)
        o = of_s[rows, :] + ob_s[rows, :]
        o_ref[rows, :] = _rms(o, nw) * _silu(g_ref[rows, :])
        return carry

    lax.fori_loop(0, seq_len // TM, epilogue, 0)


def _hgrn(proj, lb_logits, norm_w, layer, context, s0=None):
    seq_len = SEQ if context else DEC_SEQ
    bsz = BATCH if context else DEC_BATCH
    row0 = 0 if context else N_PROMPT // seq_len

    def col(k):
        return pl.BlockSpec((seq_len, LANE), lambda b, h: (row0 + b, k * H_A + h))

    in_specs = [col(0), col(1), col(2), col(3), col(4),
                pl.BlockSpec((2, DEPTH + 1, LANE), lambda b, h: (0, 0, h)),
                pl.BlockSpec((1, LANE), lambda b, h: (0, 0))]
    args = [proj, proj, proj, proj, proj, lb_logits, norm_w.reshape(1, LANE)]
    o_shape = jax.ShapeDtypeStruct((bsz * seq_len, W_A), F32)
    o_spec = pl.BlockSpec((seq_len, LANE), lambda b, h: (b, h))
    scratch = [pltpu.VMEM((seq_len, LANE), F32), pltpu.VMEM((seq_len, LANE), F32),
               pltpu.VMEM((LANE, LANE), F32), pltpu.VMEM((LANE, LANE), F32)]
    kern = functools.partial(_hgrn_kernel, seq_len=seq_len, layer=layer, context=context)
    if context:
        return pl.pallas_call(
            kern,
            out_shape=(o_shape, jax.ShapeDtypeStruct((bsz, 2, H_A, DK_A, DK_A), F32)),
            grid=(bsz, H_A),
            in_specs=in_specs,
            out_specs=(o_spec, pl.BlockSpec((1, 2, 1, DK_A, DK_A), lambda b, h: (b, 0, h, 0, 0))),
            scratch_shapes=scratch,
            compiler_params=_params(("parallel", "parallel")),
            name="hgrn_ctx",
        )(*args)
    in_specs += [pl.BlockSpec((1, 2, 1, DK_A, DK_A), lambda b, h: (b, 0, h, 0, 0))]
    args += [s0]
    return pl.pallas_call(
        kern,
        out_shape=o_shape,
        grid=(bsz, H_A),
        in_specs=in_specs,
        out_specs=o_spec,
        scratch_shapes=scratch,
        compiler_params=_params(("parallel", "parallel")),
        name="hgrn_dec",
    )(*args)


ATT_SUB = 256
ATT_TQ = 512
COL_QD = 5 * W_A // LANE
COL_KD = COL_QD + QK_B // LANE
COL_VD = COL_KD + QK_B // LANE


def _attn_kernel(*refs, seq_len, layer, context):
    if context:
        q_ref, k_ref, v_ref, lam_ref, nw_ref, o_ref, k_s, v_s = refs
    else:
        q_ref, k_ref, v_ref, ck_ref, cv_ref, lam_ref, nw_ref, o_ref, k_s, v_s = refs

    @pl.when(pl.program_id(2) == 0)
    def _():
        k_s[0:seq_len, :] = k_ref[...].astype(BF16)
        v_s[0:seq_len, :] = v_ref[...].astype(BF16)
        if not context:
            k_s[seq_len:, :] = ck_ref[0].astype(BF16)
            v_s[seq_len:, :] = cv_ref[0].astype(BF16)

    lam_init = 0.8 - 0.6 * math.exp(-0.3 * layer)
    lp = lam_ref[...]
    lam = (jnp.exp(jnp.sum(lp[0:1] * lp[1:2], axis=1, keepdims=True))
           - jnp.exp(jnp.sum(lp[2:3] * lp[3:4], axis=1, keepdims=True)) + lam_init)

    k = k_s[...]
    v = v_s[...]
    nw = nw_ref[...]
    lane = lax.broadcasted_iota(jnp.int32, (ATT_SUB, LANE), 1)
    for r0 in range(0, q_ref.shape[0], ATT_SUB):
        q = q_ref[r0:r0 + ATT_SUB, :] * (DH_B ** -0.5 * math.log2(math.e))
        s0 = _dot_nt(jnp.where(lane < DH_B, q, 0.0), k)
        s1 = _dot_nt(jnp.where(lane < DH_B, 0.0, q), k)
        p0 = jnp.exp2(s0 - jnp.max(s0, axis=1, keepdims=True))
        p1 = jnp.exp2(s1 - jnp.max(s1, axis=1, keepdims=True))
        l0 = jnp.sum(p0, axis=1, keepdims=True)
        l1 = jnp.sum(p1, axis=1, keepdims=True)
        a = p0 - (lam * l0 / l1) * p1
        o = jnp.dot(a.astype(BF16), v, preferred_element_type=F32) / l0
        o_ref[r0:r0 + ATT_SUB, :] = _rms(o, nw) * (1.0 - lam_init)


def _attn(proj, lam_p, norm_w, layer, context, cache_k=None, cache_v=None):
    seq_len = SEQ if context else DEC_SEQ
    bsz = BATCH if context else DEC_BATCH
    row0 = 0 if context else N_PROMPT // seq_len
    tq = min(ATT_TQ, seq_len)
    nq = seq_len // tq
    tile0 = row0 * nq
    t_k = seq_len if context else seq_len + PAST_LEN
    in_specs = [pl.BlockSpec((tq, LANE), lambda b, h, i: (tile0 + b * nq + i, COL_QD + h)),
                pl.BlockSpec((seq_len, LANE), lambda b, h, i: (row0 + b, COL_KD + h)),
                pl.BlockSpec((seq_len, LANE), lambda b, h, i: (row0 + b, COL_VD + h))]
    args = [proj, proj, proj]
    if not context:
        in_specs += [pl.BlockSpec((1, PAST_LEN, LANE), lambda b, h, i: (b, 0, h)),
                     pl.BlockSpec((1, PAST_LEN, LANE), lambda b, h, i: (b, 0, h))]
        args += [cache_k, cache_v]
    in_specs += [pl.BlockSpec((4, DH_B), lambda b, h, i: (0, 0)),
                 pl.BlockSpec((1, LANE), lambda b, h, i: (0, 0))]
    args += [lam_p, norm_w.reshape(1, LANE)]
    return pl.pallas_call(
        functools.partial(_attn_kernel, seq_len=seq_len, layer=layer, context=context),
        out_shape=jax.ShapeDtypeStruct((bsz * seq_len, W_B), F32),
        grid=(bsz, H_B, nq),
        in_specs=in_specs,
        out_specs=pl.BlockSpec((tq, LANE), lambda b, h, i: (b * nq + i, h)),
        scratch_shapes=[pltpu.VMEM((t_k, LANE), BF16), pltpu.VMEM((t_k, LANE), BF16)],
        compiler_params=_params(("parallel", "parallel", "arbitrary")),
        name="attn_ctx" if context else "attn_dec",
    )(*args)


COL_GATES = 4 * W_C // LANE
GATE_CHUNKS = 4
GDN_CTX_SEQS = 4


def _gdn_gates_kernel(g_ref, alog_ref, dtb_ref, col_ref, row_ref):
    c = GDN_CHUNK
    row = lax.broadcasted_iota(jnp.int32, (c, c), 0)
    col = lax.broadcasted_iota(jnp.int32, (c, c), 1)
    lane = lax.broadcasted_iota(jnp.int32, (c, LANE), 1)
    lower = (row >= col).astype(F32)
    upper = (row <= col).astype(F32)
    for j in range(GATE_CHUNKS):
        raw = g_ref[j * c:(j + 1) * c, :]
        z = raw + dtb_ref[...]
        softplus = jnp.maximum(z, 0.0) + jnp.log(1.0 + jnp.exp(-jnp.abs(z)))
        g = -jnp.exp(alog_ref[...]) * softplus
        out = jnp.where(lane < H_C, _tri_matmul(lower, g),
                        jnp.where(lane < 2 * H_C, _tri_matmul(upper, g), _sigmoid(raw)))
        col_ref[j * c:(j + 1) * c, :] = out
        row_ref[j] = out.T


def _gdn_gates(proj, alog_lane, dtb_lane):
    c = GDN_CHUNK
    n = N_TOK // c
    return pl.pallas_call(
        _gdn_gates_kernel,
        out_shape=(jax.ShapeDtypeStruct((N_TOK, LANE), F32), jax.ShapeDtypeStruct((n, LANE, c), F32)),
        grid=(n // GATE_CHUNKS,),
        in_specs=[pl.BlockSpec((GATE_CHUNKS * c, LANE), lambda i: (i, COL_GATES)),
                  pl.BlockSpec((1, LANE), lambda i: (0, 0)),
                  pl.BlockSpec((1, LANE), lambda i: (0, 0))],
        out_specs=(pl.BlockSpec((GATE_CHUNKS * c, LANE), lambda i: (i, 0)),
                   pl.BlockSpec((GATE_CHUNKS, LANE, c), lambda i: (i, 0, 0))),
        compiler_params=_params(("parallel",)),
        name="gdn_gates",
    )(proj, alog_lane, dtb_lane)


def _unit_tri_inverse_pairs(ms, row, col):
    c = ms[0].shape[0]

    def mm(xs, ys):
        out = []
        for x, y in zip(xs, ys):
            xb = x.astype(BF16)
            yb = y.astype(BF16)
            out.append(jnp.concatenate(
                [jnp.dot(xb[:, :c], yb[:, :c], preferred_element_type=F32),
                 jnp.dot(xb[:, c:], yb[:, c:], preferred_element_type=F32)], axis=1))
        return out

    def add(xs, ys):
        return [x + y for x, y in zip(xs, ys)]

    eye = (row == col).astype(F32)
    a = [jnp.where((row // 16) == (col // 16), m, 0.0) for m in ms]
    a2 = mm(a, a)
    a4 = mm(a2, a2)
    a8 = mm(a4, a4)
    t = [eye - x for x in a]
    t = add(t, mm(t, a2))
    t = add(t, mm(t, a4))
    t = add(t, mm(t, a8))
    blk = 32
    while blk <= c:
        off = ((row // blk) == (col // blk)) & ((row // (blk // 2)) != (col // (blk // 2)))
        corr = mm(mm(t, [jnp.where(off, m, 0.0) for m in ms]), t)
        t = [x - y for x, y in zip(t, corr)]
        blk *= 2
    return t


def _gdn_kernel(*refs, seq_len, n_seq, context):
    if context:
        (q_ref, k_ref, v_ref, go_ref, cwq_ref, cwk_ref, cwv_ref, gcol_ref, grow_ref, nw_ref,
         o_ref, sfin_ref, su_s, pr_s, of_s, ob_s, sf_s, sb_s) = refs
    else:
        (q_ref, k_ref, v_ref, go_ref, cwq_ref, cwk_ref, cwv_ref, gcol_ref, grow_ref, nw_ref, s0_ref,
         o_ref, su_s, pr_s, of_s, ob_s, sf_s, sb_s) = refs
    c = GDN_CHUNK
    cps = seq_len // c
    n = n_seq * cps
    head = pl.program_id(1)
    row = lax.broadcasted_iota(jnp.int32, (c, c), 0)
    col = lax.broadcasted_iota(jnp.int32, (c, c), 1)
    row2 = lax.broadcasted_iota(jnp.int32, (c, 2 * c), 0)
    col2 = lax.broadcasted_iota(jnp.int32, (c, 2 * c), 1) & (c - 1)
    lane = lax.broadcasted_iota(jnp.int32, (c, LANE), 1)
    rowi = lax.broadcasted_iota(jnp.int32, (c, LANE), 0)

    def gc_row_last(ci, d):
        gc_row = grow_ref[ci, pl.ds(d * H_C + head, 1), :]
        return gc_row, (gc_row[:, c - 1:c] if d == 0 else gc_row[:, 0:1])

    def chunk_inputs(ci):
        r0 = pl.multiple_of(ci * c, c)
        rows = pl.ds(r0, c)

        def conv(x_ref, w_ref):
            cur = x_ref[rows, :]
            before = x_ref[pl.ds(pl.multiple_of(jnp.maximum(r0 - 8, 0), 8), 8), :]
            after = x_ref[pl.ds(pl.multiple_of(jnp.minimum(r0 + c, n * c - 8), 8), 8), :]
            in_seq = lax.rem(ci, cps)
            prev_row = jnp.where(in_seq > 0, before[7:8, :], 0.0)
            next_row = jnp.where(in_seq < cps - 1, after[0:1, :], 0.0)
            xm1 = jnp.where(rowi == 0, prev_row, pltpu.roll(cur, 1, 0))
            xp1 = jnp.where(rowi == c - 1, next_row, pltpu.roll(cur, c - 1, 0))
            w = w_ref[...]
            return _silu(xm1 * w[0:1, :] + cur * w[1:2, :] + xp1 * w[2:3, :])

        q = conv(q_ref, cwq_ref)
        k = conv(k_ref, cwk_ref)
        vn = conv(v_ref, cwv_ref)
        qn = q * lax.rsqrt(jnp.sum(q * q, axis=1, keepdims=True) + EPS) * (DK_C ** -0.5)
        kn = k * lax.rsqrt(jnp.sum(k * k, axis=1, keepdims=True) + EPS)
        kq = _dot_nt(jnp.concatenate([kn, qn], axis=0), kn)
        gates = gcol_ref[rows, :]
        per_dir = []
        ms = []
        for d in (0, 1):
            gc = jnp.sum(jnp.where(lane == d * H_C + head, gates, 0.0), axis=1, keepdims=True)
            beta = jnp.sum(jnp.where(lane == (2 + d) * H_C + head, gates, 0.0), axis=1, keepdims=True)
            gc_row, gc_last = gc_row_last(ci, d)
            incl = (row >= col) if d == 0 else (row <= col)
            strict = (row > col) if d == 0 else (row < col)
            decay = jnp.where(incl, jnp.exp(gc - gc_row), 0.0)
            m = jnp.where(strict, kq[:c] * beta * decay, 0.0)
            per_dir.append((gc, beta, kq[c:] * decay, (kn * jnp.exp(gc_last - gc)).T))
            ms.append(m)
        return jnp.concatenate(ms, axis=1), (rows, qn, kn, vn, per_dir)

    def chunk_outputs(ci, t, rest):
        rows, qn, kn, vn, per_dir = rest
        base = pl.multiple_of(ci * 2 * c, 2 * c)
        for d in (0, 1):
            gc, beta, qk, kd_t = per_dir[d]
            e = jnp.exp(gc)
            uw = _dot(t[:, d * c:(d + 1) * c], jnp.concatenate([vn * beta, kn * (beta * e)], axis=1))
            cross = _dot(jnp.concatenate([kd_t, qk], axis=0), uw)
            su_s[d, rows, :] = cross[:c, :c]
            (of_s, ob_s)[d][rows, :] = cross[c:, :c]
            pr_s[d, pl.ds(base, c), :] = cross[:c, c:].astype(BF16)
            pr_s[d, pl.ds(base + c, c), :] = (qn * e - cross[c:, c:]).astype(BF16)

    group = min(8, n)

    def prepare(j, carry):
        cis = [j * group + g for g in range(group)]
        staged = [chunk_inputs(ci) for ci in cis]
        ts = _unit_tri_inverse_pairs([s[0] for s in staged], row2, col2)
        for ci, t, s in zip(cis, ts, staged):
            chunk_outputs(ci, t, s[1])
        return carry

    lax.fori_loop(0, n // group, prepare, 0)

    def advance(ci, d, s_ref, out_s):
        rows = pl.ds(pl.multiple_of(ci * c, c), c)
        _, gc_last = gc_row_last(ci, d)
        s = s_ref[...]
        ps = jnp.dot(pr_s[d, pl.ds(pl.multiple_of(ci * 2 * c, 2 * c), 2 * c), :], s.astype(BF16),
                     preferred_element_type=F32)
        out_s[rows, :] = out_s[rows, :] + ps[c:]
        s_ref[...] = s * jnp.exp(gc_last) - ps[:c] + su_s[d, rows, :]

    for s in range(n_seq):
        if context:
            sf_s[...] = jnp.zeros_like(sf_s)
            sb_s[...] = jnp.zeros_like(sb_s)
        else:
            sf_s[...] = s0_ref[s, 0, 0]
            sb_s[...] = s0_ref[s, 1, 0]

        def body(i, carry, first=s * cps):
            advance(first + i, 0, sf_s, of_s)
            advance(first + cps - 1 - i, 1, sb_s, ob_s)
            return carry

        lax.fori_loop(0, cps, body, 0)

        if context:
            sfin_ref[s, 0, 0] = sf_s[...]
            sfin_ref[s, 1, 0] = sb_s[...]

    nw = nw_ref[...]

    def epilogue(j, carry):
        rows = pl.ds(pl.multiple_of(j * TM, TM), TM)
        o = of_s[rows, :] + ob_s[rows, :]
        o_ref[rows, :] = _rms(o, nw) * _silu(go_ref[rows, :])
        return carry

    lax.fori_loop(0, n * c // TM, epilogue, 0)


def _gdn(proj, conv_w, gcol, grow, norm_w, context, s0=None):
    seq_len = SEQ if context else DEC_SEQ
    n_seq = GDN_CTX_SEQS if context else 1
    n_blk = (BATCH if context else DEC_BATCH) // n_seq
    rows = n_seq * seq_len
    row0 = 0 if context else N_PROMPT // rows
    nc = rows // GDN_CHUNK

    def col(k):
        return pl.BlockSpec((rows, LANE), lambda b, h: (row0 + b, k * H_C + h))

    def cw(k):
        return pl.BlockSpec((3, LANE), lambda b, h: (0, k * H_C + h))

    in_specs = [col(0), col(1), col(2), col(3), cw(0), cw(1), cw(2),
                pl.BlockSpec((rows, LANE), lambda b, h: (row0 + b, 0)),
                pl.BlockSpec((nc, LANE, GDN_CHUNK), lambda b, h: (row0 + b, 0, 0)),
                pl.BlockSpec((1, LANE), lambda b, h: (0, 0))]
    args = [proj, proj, proj, proj, conv_w, conv_w, conv_w, gcol, grow, norm_w.reshape(1, LANE)]
    o_shape = jax.ShapeDtypeStruct((n_blk * rows, W_C), F32)
    o_spec = pl.BlockSpec((rows, LANE), lambda b, h: (b, h))
    seq_buf = pltpu.VMEM((rows, LANE), F32)
    state_buf = pltpu.VMEM((DK_C, DK_C), F32)
    scratch = [pltpu.VMEM((2, rows, LANE), F32), pltpu.VMEM((2, 2 * rows, LANE), BF16),
               seq_buf, seq_buf, state_buf, state_buf]
    kern = functools.partial(_gdn_kernel, seq_len=seq_len, n_seq=n_seq, context=context)
    state_spec = pl.BlockSpec((n_seq, 2, 1, DK_C, DK_C), lambda b, h: (b, 0, h, 0, 0))
    if context:
        return pl.pallas_call(
            kern,
            out_shape=(o_shape, jax.ShapeDtypeStruct((BATCH, 2, H_C, DK_C, DK_C), F32)),
            grid=(n_blk, H_C),
            in_specs=in_specs,
            out_specs=(o_spec, state_spec),
            scratch_shapes=scratch,
            compiler_params=_params(("parallel", "parallel")),
            name="gdn_ctx",
        )(*args)
    in_specs += [state_spec]
    args += [s0]
    return pl.pallas_call(
        kern,
        out_shape=o_shape,
        grid=(n_blk, H_C),
        in_specs=in_specs,
        out_specs=o_spec,
        scratch_shapes=scratch,
        compiler_params=_params(("parallel", "parallel")),
        name="gdn_dec",
    )(*args)


def kernel(x_prompt, x_sample, c, cache_diff_k, cache_diff_v, state_hgrn, state_gdn, c_ctx,
           ada_w, ada_b, norm_w, final_norm_w, w_in_ab, hgrn_lb_logits, hgrn_norm_w,
           diff_lambda, diff_norm_w, w_in_c, gdn_conv_w, gdn_a_log, gdn_dt_bias, gdn_norm_w,
           w_out, moe_router_group, moe_router_expert, moe_w_gate, moe_w_up, moe_w_down):
    x = (x_prompt.reshape(N_PROMPT, D_MODEL), x_sample.reshape(N_SAMPLE, D_MODEL))
    cvec = jnp.concatenate([c_ctx[None, :], c, jnp.zeros((MOD_ROWS - N_MOD, D_MODEL), F32)], axis=0)
    mod_all = _adaln(cvec, ada_w, ada_b)[:, :N_MOD].reshape(DEPTH, N_MOD, 6, D_MODEL)
    mod_all = jnp.pad(mod_all, ((0, 0), (0, 0), (0, MOD_ROWS - 6), (0, 0)))
    cos, sin = _rope_tables()
    new_k, new_v, new_hgrn, new_gdn = [], [], [], []
    for l in range(DEPTH):
        i = l // 2
        mod = mod_all[l]
        if l % 2 == 0:
            proj = _inproj(x, mod, norm_w[l, 0], w_in_ab[i].astype(BF16),
                           rope=(cos, sin, (COL_QD * LANE, COL_VD * LANE)))
            oh_ctx, s_h = _hgrn(proj, hgrn_lb_logits, hgrn_norm_w[i], l, True)
            oh_dec = _hgrn(proj, hgrn_lb_logits, hgrn_norm_w[i], l, False, s0=state_hgrn[:, i])
            od_ctx = _attn(proj, diff_lambda[i], diff_norm_w[i], l, True)
            od_dec = _attn(proj, diff_lambda[i], diff_norm_w[i], l, False,
                           cache_k=cache_diff_k[:, i].reshape(DEC_BATCH, PAST_LEN, QK_B),
                           cache_v=cache_diff_v[:, i].reshape(DEC_BATCH, PAST_LEN, W_B))
            parts = ((oh_ctx, oh_dec), (od_ctx, od_dec))
            new_k.append(proj[:N_PROMPT, COL_KD * LANE:COL_VD * LANE].reshape(BATCH, SEQ, H_B, 2, DH_B))
            new_v.append(proj[:N_PROMPT, COL_VD * LANE:].reshape(BATCH, SEQ, H_B, DV_B))
            new_hgrn.append(s_h)
        else:
            w_c = jnp.pad(w_in_c[i], ((0, 0), (0, (COL_GATES + 1) * LANE - IN_C))).astype(BF16)
            proj = _inproj(x, mod, norm_w[l, 0], w_c)
            pad = jnp.zeros((LANE - 2 * H_C,), F32)
            alog_lane = jnp.concatenate([gdn_a_log[i, 0], gdn_a_log[i, 1], pad]).reshape(1, LANE)
            dtb_lane = jnp.concatenate([gdn_dt_bias[i, 0], gdn_dt_bias[i, 1], pad]).reshape(1, LANE)
            gcol, grow = _gdn_gates(proj, alog_lane, dtb_lane)
            oc_ctx, s_c = _gdn(proj, gdn_conv_w[i], gcol, grow, gdn_norm_w[i], True)
            oc_dec = _gdn(proj, gdn_conv_w[i], gcol, grow, gdn_norm_w[i], False, s0=state_gdn[:, i])
            parts = ((oc_ctx, oc_dec),)
            new_gdn.append(s_c)
        w_router = jnp.concatenate(
            [moe_router_group[l], moe_router_expert[l],
             jnp.zeros((D_MODEL, LANE - N_GROUPS - N_EXPERTS), F32)], axis=1)
        w_router_hi = w_router.astype(BF16)
        w_router = jnp.concatenate([w_router_hi, (w_router - w_router_hi.astype(F32)).astype(BF16)], axis=1)
        x, h2, route = _outproj(parts, w_out[l].astype(BF16), x, mod, norm_w[l, 1], w_router)
        x = _moe(x, h2, route, mod, moe_w_gate, moe_w_up, moe_w_down, l, final_norm_w)
    y_prompt = x[0].reshape(BATCH, SEQ, D_MODEL)
    y_sample = x[1].reshape(DEC_BATCH, DEC_SEQ, D_MODEL)
    return (y_prompt, y_sample, jnp.stack(new_k, axis=1), jnp.stack(new_v, axis=1),
            jnp.stack(new_hgrn, axis=1), jnp.stack(new_gdn, axis=1))
```
